```python
import math
import jax
import jax.numpy as jnp
from jax import lax
import numpy as np

D_MODEL = 1024
BATCH = 4
SEQ = 4096
DEPTH = 4

GRID_W = 64
CTX_LEN = 256
EPS = 1e-6
N_MOD = 6

HY_WIDTH = 512
HY_ORDER = 2
HY_BANDS = 16
HY_EMB = 1 + 2 * HY_BANDS
HY_FFN = 64
SHORT_K = 3
HY_FAST_DECAY = 0.3
HY_SLOW_DECAY = 1.5
HY_TARGET = 1e-2

MLA_HEADS = 8
QK_NOPE = 64
QK_ROPE = 32
V_DIM = 64
Q_LORA = 384
KV_LORA = 256
ROPE_PAIRS = QK_ROPE // 4
ROPE_BASE = 10000.0
ATTN_SCALE = (QK_NOPE + QK_ROPE) ** -0.5
Q_BLOCK = 128

N_GROUPS = 8
EXP_PER_GROUP = 8
N_EXPERTS = N_GROUPS * EXP_PER_GROUP
TOP_K = 2
D_EXPERT = 256
MOE_BLOCK = 128

HY_COLS = (HY_ORDER + 1) * HY_WIDTH
MLA_END = HY_COLS + Q_LORA + KV_LORA + QK_ROPE
IN_COLS = MLA_END + 2 * D_MODEL

kernel_name = "hyena_mla_hier_moe_dit_trunk"


def rmsnorm(x, g):
    xf = x.astype(jnp.float32)
    y = xf * lax.rsqrt(jnp.mean(xf * xf, axis=-1, keepdims=True) + EPS)
    return (y * g.astype(jnp.float32)).astype(x.dtype)


def modulate(h, g, shift, scale):
    return rmsnorm(h, g) * (1 + scale) + shift


def axial_rope_tables(L):
    rows = L // GRID_W
    r, col = jnp.meshgrid(jnp.arange(rows), jnp.arange(GRID_W), indexing="ij")
    pos = jnp.stack([r.reshape(-1), col.reshape(-1)], axis=-1).astype(jnp.float32)
    inv = ROPE_BASE ** (-jnp.arange(ROPE_PAIRS, dtype=jnp.float32) / ROPE_PAIRS)
    ang = pos[:, :, None] * inv
    return jnp.cos(ang), jnp.sin(ang)


def apply_axial_rope(x, cos, sin):
    shp = x.shape
    xa = x.reshape(shp[:-1] + (2, 2, ROPE_PAIRS))
    x1, x2 = xa[..., 0, :], xa[..., 1, :]
    bshape = (shp[1],) + (1,) * (x.ndim - 3) + (2, ROPE_PAIRS)
    cs = cos.reshape(bshape).astype(x.dtype)
    sn = sin.reshape(bshape).astype(x.dtype)
    return jnp.stack([x1 * cs - x2 * sn, x1 * sn + x2 * cs], axis=-2).reshape(shp)


def short_conv(z, w, b):
    L = z.shape[1]
    pad = SHORT_K // 2
    zp = jnp.pad(z, ((0, 0), (pad, pad), (0, 0)))
    out = b + zp[:, 0:L] * w[0]
    for k in range(1, SHORT_K):
        out = out + zp[:, k:k + L] * w[k]
    return out


def hyena_filters(L, w1, b1, freq, w2, b2, w3):
    f32 = jnp.float32
    t = jnp.linspace(0.0, 1.0, L, dtype=f32)[:, None]
    w = 2.0 * math.pi * jnp.arange(L, dtype=f32)[:, None] / L
    bands = jnp.linspace(1e-4, HY_BANDS - 1, HY_BANDS, dtype=f32)[None, :]
    z = jnp.concatenate([t, jnp.cos(bands * w), -jnp.sin(bands * w)], axis=-1)
    fr = freq.astype(f32)
    h = jnp.sin(fr * (z @ w1.astype(f32) + b1.astype(f32)))
    h = jnp.sin(fr * (h @ w2.astype(f32) + b2.astype(f32)))
    h = (h @ w3.astype(f32)).reshape(L, HY_ORDER, 2, HY_WIDTH)
    deltas = jnp.linspace(math.log(HY_TARGET) / HY_FAST_DECAY, math.log(HY_TARGET) / HY_SLOW_DECAY,
                          HY_WIDTH, dtype=f32)
    h = h * jnp.exp(-t[:, :, None, None] * jnp.abs(deltas))
    h = h * lax.rsqrt(jnp.sum(h * h, axis=0, keepdims=True) + EPS)
    h_fwd, h_bwd = h[:, :, 0], h[:, :, 1]
    circ = jnp.concatenate([h_fwd, jnp.zeros_like(h_fwd[:1]), h_bwd[1:][::-1]], axis=0)
    return jnp.fft.rfft(circ, axis=0)


def long_conv(z, h_freq):
    L = z.shape[1]
    zf = jnp.fft.rfft(z.astype(jnp.float32), n=2 * L, axis=1)
    return jnp.fft.irfft(zf * h_freq, n=2 * L, axis=1)[:, :L].astype(z.dtype)


def hyena_branch(p_hy, sc_w, sc_b, flt, flt_bias):
    L = p_hy.shape[1]
    u = short_conv(p_hy, sc_w, sc_b)
    v, x1, x2 = jnp.split(u, 3, axis=-1)
    h_freq = hyena_filters(L, *flt)
    z = v
    for o, gate in enumerate((x1, x2)):
        z = gate * (long_conv(z, h_freq[:, o]) + flt_bias[o] * z)
    return z


def mla_queries(p_q, q_norm_g, w_uq):
    B, L, _ = p_q.shape
    q = (rmsnorm(p_q, q_norm_g) @ w_uq).reshape(B, L, MLA_HEADS, QK_NOPE + QK_ROPE)
    return q[..., :QK_NOPE], q[..., QK_NOPE:]


def mla_keys_values(p_kv, kv_norm_g, w_ukv):
    B, L, _ = p_kv.shape
    c_kv, k_rope = p_kv[..., :KV_LORA], p_kv[..., KV_LORA:]
    kv = (rmsnorm(c_kv, kv_norm_g) @ w_ukv).reshape(B, L, MLA_HEADS, QK_NOPE + V_DIM)
    return kv[..., :QK_NOPE], kv[..., QK_NOPE:], k_rope


def ctx_attention(qn, qr, kn, kr, v):
    s = jnp.einsum("bqhd,bkhd->bhqk", qn, kn) + jnp.einsum("bqhr,bkr->bhqk", qr, kr)
    p = jax.nn.softmax(s.astype(jnp.float32) * ATTN_SCALE, axis=-1).astype(v.dtype)
    B, C = qn.shape[:2]
    return jnp.einsum("bhqk,bkhd->bqhd", p, v).reshape(B, C, MLA_HEADS * V_DIM)


def latent_attention(qn, qr_rot, qr_raw, kn, kr_rot, v, kn_c, kr_c, v_c):
    B, L = qn.shape[:2]
    nb = L // Q_BLOCK

    def blockify(t):
        return jnp.moveaxis(t.reshape((B, nb, Q_BLOCK) + t.shape[2:]), 1, 0)

    def one_block(args):
        bqn, bqr, bqr0 = args
        s_lat = jnp.einsum("bqhd,bkhd->bhqk", bqn, kn) + jnp.einsum("bqhr,bkr->bhqk", bqr, kr_rot)
        s_ctx = jnp.einsum("bqhd,bkhd->bhqk", bqn, kn_c) + jnp.einsum("bqhr,bkr->bhqk", bqr0, kr_c)
        s = jnp.concatenate([s_lat, s_ctx], axis=-1).astype(jnp.float32) * ATTN_SCALE
        p = jax.nn.softmax(s, axis=-1).astype(v.dtype)
        return (jnp.einsum("bhqk,bkhd->bqhd", p[..., :L], v)
                + jnp.einsum("bhqk,bkhd->bqhd", p[..., L:], v_c))

    out = lax.map(one_block, (blockify(qn), blockify(qr_rot), blockify(qr_raw)))
    return jnp.moveaxis(out, 0, 1).reshape(B, L, MLA_HEADS * V_DIM)


def merge_branches(p_gate, y_hy, y_att, w_hy_o, w_mla_o, w_out):
    g = jax.nn.sigmoid(p_gate)
    return (g[..., :D_MODEL] * (y_hy @ w_hy_o) + g[..., D_MODEL:] * (y_att @ w_mla_o)) @ w_out


def grouped_experts(t, expert, weight, w_gate, w_up, w_down):
    T = t.shape[0]
    A = expert.shape[0]
    token = jnp.repeat(jnp.arange(T, dtype=jnp.int32), TOP_K)
    order = jnp.argsort(expert)
    e_sorted = expert[order]
    counts = jnp.zeros((N_EXPERTS,), jnp.int32).at[expert].add(1)
    padded = (counts + MOE_BLOCK - 1) // MOE_BLOCK * MOE_BLOCK
    start = jnp.cumsum(counts) - counts
    p_end = jnp.cumsum(padded)
    p_start = p_end - padded
    dest = p_start[e_sorted] + jnp.arange(A, dtype=jnp.int32) - start[e_sorted]
    n_blocks = -(-A // MOE_BLOCK) + N_EXPERTS
    slots = n_blocks * MOE_BLOCK
    slot_token = jnp.full((slots,), T, jnp.int32).at[dest].set(token[order])
    slot_weight = jnp.zeros((slots,), weight.dtype).at[dest].set(weight[order])
    block_expert = jnp.minimum(
        jnp.searchsorted(p_end, jnp.arange(n_blocks, dtype=jnp.int32) * MOE_BLOCK, side="right"),
        N_EXPERTS - 1)
    t_pad = jnp.concatenate([t, jnp.zeros((1, t.shape[1]), t.dtype)], axis=0)

    def run_block(args):
        e, tok, wt = args
        xb = t_pad[tok]
        hb = jax.nn.silu(xb @ w_gate[e]) * (xb @ w_up[e])
        return (hb @ w_down[e]) * wt[:, None].astype(t.dtype)

    yb = lax.map(run_block, (block_expert, slot_token.reshape(n_blocks, MOE_BLOCK),
                             slot_weight.reshape(n_blocks, MOE_BLOCK)))
    y = jnp.zeros_like(t_pad).at[slot_token].add(yb.reshape(slots, t.shape[1]))
    return y[:T]


def hier_moe(u, w_rg, b_rg, w_re, b_re, w_gate, w_up, w_down):
    shp = u.shape
    t = u.reshape(-1, D_MODEL)
    T = t.shape[0]
    lg = (t @ w_rg).astype(jnp.float32)
    g_sel = jnp.argmax(lg + b_rg.astype(jnp.float32), axis=-1)
    p_g = jnp.take_along_axis(jax.nn.softmax(lg, axis=-1), g_sel[:, None], axis=-1)
    le = (t @ w_re).astype(jnp.float32).reshape(T, N_GROUPS, EXP_PER_GROUP)
    le_g = jnp.take_along_axis(le, g_sel[:, None, None], axis=1)[:, 0]
    be_g = b_re.astype(jnp.float32).reshape(N_GROUPS, EXP_PER_GROUP)[g_sel]
    _, local = lax.top_k(le_g + be_g, TOP_K)
    w_local = jax.nn.softmax(jnp.take_along_axis(le_g, local, axis=-1), axis=-1)
    expert = (g_sel[:, None] * EXP_PER_GROUP + local).astype(jnp.int32)
    weight = p_g * w_local
    y = grouped_experts(t, expert.reshape(-1), weight.reshape(-1), w_gate, w_up, w_down)
    return y.reshape(shp)


def setup_inputs(seed: int = 0) -> dict:
    key = jax.random.key(seed)
    ks = iter(jax.random.split(key, 40))

    def nrm(shape, scale):
        return jax.random.normal(next(ks), shape, jnp.float32) * scale

    def gain(shape):
        return 1.0 + nrm(shape, 0.05)

    return {
        "x": nrm((BATCH, SEQ, D_MODEL), 1.0),
        "c": nrm((BATCH, D_MODEL), 1.0),
        "ctx": nrm((BATCH, CTX_LEN, D_MODEL), 1.0),
        "c_ctx": nrm((D_MODEL,), 1.0),
        "w_ada": nrm((DEPTH, D_MODEL, N_MOD * D_MODEL), 0.5 * D_MODEL ** -0.5),
        "b_ada": nrm((DEPTH, N_MOD * D_MODEL), 0.02),
        "norm1_g": gain((DEPTH, D_MODEL)),
        "w_in": nrm((DEPTH, D_MODEL, IN_COLS), D_MODEL ** -0.5),
        "q_norm_g": gain((DEPTH, Q_LORA)),
        "kv_norm_g": gain((DEPTH, KV_LORA)),
        "w_uq": nrm((DEPTH, Q_LORA, MLA_HEADS * (QK_NOPE + QK_ROPE)), Q_LORA ** -0.5),
        "w_ukv": nrm((DEPTH, KV_LORA, MLA_HEADS * (QK_NOPE + V_DIM)), KV_LORA ** -0.5),
        "sc_w": nrm((DEPTH, SHORT_K, HY_COLS), SHORT_K ** -0.5),
        "sc_b": nrm((DEPTH, HY_COLS), 0.02),
        "flt_w1": nrm((DEPTH, HY_EMB, HY_FFN), HY_EMB ** -0.5),
        "flt_b1": nrm((DEPTH, HY_FFN), 0.1),
        "flt_freq": gain((DEPTH, HY_FFN)),
        "flt_w2": nrm((DEPTH, HY_FFN, HY_FFN), HY_FFN ** -0.5),
        "flt_b2": nrm((DEPTH, HY_FFN), 0.1),
        "flt_w3": nrm((DEPTH, HY_FFN, HY_ORDER * 2 * HY_WIDTH), HY_FFN ** -0.5),
        "flt_bias": nrm((DEPTH, HY_ORDER, HY_WIDTH), 0.5),
        "w_hy_o": nrm((DEPTH, HY_WIDTH, D_MODEL), HY_WIDTH ** -0.5),
        "w_mla_o": nrm((DEPTH, MLA_HEADS * V_DIM, D_MODEL), (MLA_HEADS * V_DIM) ** -0.5),
        "w_out": nrm((DEPTH, D_MODEL, D_MODEL), D_MODEL ** -0.5),
        "norm2_g": gain((DEPTH, D_MODEL)),
        "w_router_g": nrm((DEPTH, D_MODEL, N_GROUPS), D_MODEL ** -0.5),
        "b_router_g": nrm((DEPTH, N_GROUPS), 0.01),
        "w_router_e": nrm((DEPTH, D_MODEL, N_EXPERTS), D_MODEL ** -0.5),
        "b_router_e": nrm((DEPTH, N_EXPERTS), 0.01),
        "w_gate_e": nrm((DEPTH, N_EXPERTS, D_MODEL, D_EXPERT), D_MODEL ** -0.5),
        "w_up_e": nrm((DEPTH, N_EXPERTS, D_MODEL, D_EXPERT), D_MODEL ** -0.5),
        "w_down_e": nrm((DEPTH, N_EXPERTS, D_EXPERT, D_MODEL), D_EXPERT ** -0.5),
        "final_g": gain((D_MODEL,)),
    }


def reference(x, c, ctx, c_ctx, w_ada, b_ada, norm1_g, w_in, q_norm_g, kv_norm_g, w_uq, w_ukv,
              sc_w, sc_b, flt_w1, flt_b1, flt_freq, flt_w2, flt_b2, flt_w3, flt_bias,
              w_hy_o, w_mla_o, w_out, norm2_g, w_router_g, b_router_g, w_router_e, b_router_e,
              w_gate_e, w_up_e, w_down_e, final_g):
    L = x.shape[1]
    cos, sin = axial_rope_tables(L)
    s_x = jax.nn.silu(c)
    s_c = jax.nn.silu(c_ctx)
    hx, hc = x, ctx
    for l in range(DEPTH):
        last = l == DEPTH - 1
        mx = (s_x @ w_ada[l] + b_ada[l])[:, None, :]
        mc = s_c @ w_ada[l] + b_ada[l]
        sh1x, sc1x, g1x, sh2x, sc2x, g2x = jnp.split(mx, N_MOD, axis=-1)
        sh1c, sc1c, g1c, sh2c, sc2c, g2c = jnp.split(mc, N_MOD, axis=-1)
        flt = (flt_w1[l], flt_b1[l], flt_freq[l], flt_w2[l], flt_b2[l], flt_w3[l])

        ux = modulate(hx, norm1_g[l], sh1x, sc1x)
        uc = modulate(hc, norm1_g[l], sh1c, sc1c)
        px = ux @ w_in[l]
        pc = None if last else uc @ w_in[l]
        pc_mla = uc @ w_in[l][:, HY_COLS:MLA_END] if last else pc[..., HY_COLS:MLA_END]
        kn_c, v_c, kr_c = mla_keys_values(pc_mla[..., Q_LORA:], kv_norm_g[l], w_ukv[l])

        qn_x, qr_x = mla_queries(px[..., HY_COLS:HY_COLS + Q_LORA], q_norm_g[l], w_uq[l])
        kn_x, v_x, kr_x = mla_keys_values(px[..., HY_COLS + Q_LORA:MLA_END], kv_norm_g[l], w_ukv[l])
        att_x = latent_attention(qn_x, apply_axial_rope(qr_x, cos, sin), qr_x,
                                 kn_x, apply_axial_rope(kr_x, cos, sin), v_x, kn_c, kr_c, v_c)
        hy_x = hyena_branch(px[..., :HY_COLS], sc_w[l], sc_b[l], flt, flt_bias[l])
        hx = hx + g1x * merge_branches(px[..., MLA_END:], hy_x, att_x, w_hy_o[l], w_mla_o[l], w_out[l])
        if not last:
            qn_c, qr_c = mla_queries(pc[..., HY_COLS:HY_COLS + Q_LORA], q_norm_g[l], w_uq[l])
            att_c = ctx_attention(qn_c, qr_c, kn_c, kr_c, v_c)
            hy_c = hyena_branch(pc[..., :HY_COLS], sc_w[l], sc_b[l], flt, flt_bias[l])
            hc = hc + g1c * merge_branches(pc[..., MLA_END:], hy_c, att_c, w_hy_o[l], w_mla_o[l], w_out[l])

        moe_w = (w_router_g[l], b_router_g[l], w_router_e[l], b_router_e[l],
                 w_gate_e[l], w_up_e[l], w_down_e[l])
        hx = hx + g2x * hier_moe(modulate(hx, norm2_g[l], sh2x, sc2x), *moe_w)
        if not last:
            hc = hc + g2c * hier_moe(modulate(hc, norm2_g[l], sh2c, sc2c), *moe_w)
    return rmsnorm(hx, final_g)
```

```python
import functools
import math

import jax
import jax.numpy as jnp
import numpy as np
from jax import lax
from jax.experimental import pallas as pl
from jax.experimental.pallas import tpu as pltpu

F32 = jnp.float32
BF16 = jnp.bfloat16

D_MODEL = 1024
BATCH = 4
SEQ = 4096
DEPTH = 4
GRID_W = 64
CTX_LEN = 256
S_ALL = SEQ + CTX_LEN
EPS = 1e-6
N_MOD = 6

HY_WIDTH = 512
HY_ORDER = 2
HY_BANDS = 16
HY_EMB = 1 + 2 * HY_BANDS
HY_FFN = 64
SHORT_K = 3
HY_FAST_DECAY = 0.3
HY_SLOW_DECAY = 1.5
HY_TARGET = 1e-2

MLA_HEADS = 8
QK_NOPE = 64
QK_ROPE = 32
V_DIM = 64
Q_LORA = 384
KV_LORA = 256
ROPE_PAIRS = QK_ROPE // 4
ROPE_BASE = 10000.0
ATTN_SCALE = (QK_NOPE + QK_ROPE) ** -0.5

N_GROUPS = 8
EXP_PER_GROUP = 8
N_EXPERTS = N_GROUPS * EXP_PER_GROUP
TOP_K = 2
D_EXPERT = 256

HY_COLS = (HY_ORDER + 1) * HY_WIDTH
MLA_END = HY_COLS + Q_LORA + KV_LORA + QK_ROPE

LANES = 128
HEAD_PAD = LANES
VMEM_LIMIT = 56 << 20

TM = 256
N_LAT_TILES = SEQ // TM
N_ALL_TILES = S_ALL // TM
TQ = 256
TK = 512
FB = 256
MOE_BM = 256
ROUTER_PAD = LANES


def _cp(*sem):
    return pltpu.CompilerParams(dimension_semantics=sem, vmem_limit_bytes=VMEM_LIMIT)


def _dot(a, b):
    return jnp.dot(a, b, preferred_element_type=F32)


def _split(a):
    hi = a.astype(BF16)
    lo = (a - hi.astype(F32)).astype(BF16)
    return hi, lo


def _dot3(a, b):
    ah, al = _split(a)
    bh, bl = _split(b)
    return _dot(ah, bh) + (_dot(ah, bl) + _dot(al, bh))


def _rms(x, g):
    return x * lax.rsqrt(jnp.mean(x * x, axis=-1, keepdims=True) + EPS) * g


def _ada_kernel(c_ref, w_ref, b_ref, o_ref):
    cc = c_ref[...]
    s = cc * jax.nn.sigmoid(cc)
    o_ref[0] = _dot3(s, w_ref[0]) + b_ref[0]


def _ada(cc, w_ada, b_ada):
    tn = 1536
    n = N_MOD * D_MODEL
    return pl.pallas_call(
        _ada_kernel,
        out_shape=jax.ShapeDtypeStruct((DEPTH, 8, n), F32),
        grid=(DEPTH, n // tn),
        in_specs=[
            pl.BlockSpec((8, D_MODEL), lambda l, j: (0, 0)),
            pl.BlockSpec((1, D_MODEL, tn), lambda l, j: (l, 0, j)),
            pl.BlockSpec((1, 1, tn), lambda l, j: (l, 0, j)),
        ],
        out_specs=pl.BlockSpec((1, 8, tn), lambda l, j: (l, 0, j)),
        compiler_params=_cp("parallel", "parallel"),
        name="ada",
    )(cc, w_ada, b_ada.reshape(DEPTH, 1, n))


def _inproj_kernel(h_ref, mod_ref, g_ref, tab_ref, qg_ref, kvg_ref,
                   why_ref, wgt_ref, wq_ref, wkv_ref, wkr_ref, wuq_ref, wuqs_ref, wuk_ref, wuv_ref,
                   phy_ref, sg_ref, q_ref, k_ref, v_ref):
    h = h_ref[0]
    shift = mod_ref[0, 0, 0, 0:1, :]
    scale = mod_ref[0, 0, 0, 1:2, :]
    u = (_rms(h, g_ref[0]) * (1.0 + scale) + shift).astype(BF16)

    phy_ref[0] = _dot(u, why_ref[0])
    sg_ref[0] = jax.nn.sigmoid(_dot(u, wgt_ref[0])).astype(BF16)

    cosq, sinq, cosk, sink = tab_ref[0], tab_ref[1], tab_ref[2], tab_ref[3]
    cq = _rms(_dot(u, wq_ref[0]), qg_ref[0]).astype(BF16)
    qa = _dot(cq, wuq_ref[0])
    qs = _dot(cq, wuqs_ref[0])
    ckv = _rms(_dot(u, wkv_ref[0]), kvg_ref[0]).astype(BF16)
    ka = _dot(ckv, wuk_ref[0])
    va = _dot(ckv, wuv_ref[0])
    kr = _dot(u, wkr_ref[0])
    krk = kr[:, :HEAD_PAD] * cosk + kr[:, HEAD_PAD:] * sink
    ones_col = (lax.broadcasted_iota(jnp.int32, (1, HEAD_PAD), 1) == V_DIM).astype(F32)
    for hh in range(MLA_HEADS):
        sl = slice(hh * HEAD_PAD, (hh + 1) * HEAD_PAD)
        q_ref[0, hh] = (qa[:, sl] * cosq + qs[:, sl] * sinq).astype(BF16)
        k_ref[0, hh] = (ka[:, sl] + krk).astype(BF16)
        v_ref[0, hh] = (va[:, sl] + ones_col).astype(BF16)


def _inproj(l, h, mods, norm1_g, tabs, q_norm_g, kv_norm_g, wts):
    names = ("why", "wgt", "wq", "wkv", "wkr", "wuq", "wuqs", "wuk", "wuv")
    w_specs = [pl.BlockSpec((1,) + wts[n].shape[1:], lambda b, i: (l, 0, 0)) for n in names]
    hd = MLA_HEADS * HEAD_PAD
    qkv_shape = jax.ShapeDtypeStruct((BATCH, MLA_HEADS, S_ALL, HEAD_PAD), BF16)
    qkv_spec = pl.BlockSpec((1, MLA_HEADS, TM, HEAD_PAD), lambda b, i: (b, 0, i, 0))
    del hd
    return pl.pallas_call(
        _inproj_kernel,
        out_shape=(
            jax.ShapeDtypeStruct((BATCH, S_ALL, HY_COLS), F32),
            jax.ShapeDtypeStruct((BATCH, S_ALL, 2 * D_MODEL), BF16),
            qkv_shape, qkv_shape, qkv_shape,
        ),
        grid=(BATCH, N_ALL_TILES),
        in_specs=[
            pl.BlockSpec((1, TM, D_MODEL), lambda b, i: (b, i, 0)),
            pl.BlockSpec((1, 1, 1, N_MOD, D_MODEL), lambda b, i: (l, b, i // N_LAT_TILES, 0, 0)),
            pl.BlockSpec((1, 1, D_MODEL), lambda b, i: (l, 0, 0)),
            pl.BlockSpec((4, TM, HEAD_PAD), lambda b, i: (0, i, 0)),
            pl.BlockSpec((1, 1, Q_LORA), lambda b, i: (l, 0, 0)),
            pl.BlockSpec((1, 1, KV_LORA), lambda b, i: (l, 0, 0)),
        ] + w_specs,
        out_specs=(
            pl.BlockSpec((1, TM, HY_COLS), lambda b, i: (b, i, 0)),
            pl.BlockSpec((1, TM, 2 * D_MODEL), lambda b, i: (b, i, 0)),
            qkv_spec, qkv_spec, qkv_spec,
        ),
        compiler_params=_cp("parallel", "parallel"),
        name="inproj",
    )(h, mods, norm1_g, tabs, q_norm_g, kv_norm_g, *[wts[n] for n in names])


def _flash_step(q, k, v, m, acc):
    s = lax.dot_general(q, k, (((1,), (1,)), ((), ())), preferred_element_type=F32)
    m_new = jnp.maximum(m, jnp.max(s, axis=-1, keepdims=True))
    p = jnp.exp(s - m_new)
    acc = acc * jnp.exp(m - m_new) + _dot(p.astype(BF16), v)
    return m_new, acc


def _attn_kernel(q_ref, k_ref, v_ref, o_ref, *, n_full, tk, tail):
    tq = q_ref.shape[2]
    outs = []
    for hh in range(2):
        q = q_ref[0, hh]

        def body(j, carry, hh=hh, q=q):
            off = pl.multiple_of(j * tk, tk)
            return _flash_step(q, k_ref[0, hh, pl.ds(off, tk), :], v_ref[0, hh, pl.ds(off, tk), :], *carry)

        carry = (jnp.full((tq, 1), -1e30, F32), jnp.zeros((tq, HEAD_PAD), F32))
        if n_full:
            carry = lax.fori_loop(0, n_full, body, carry)
        if tail:
            lo = n_full * tk
            carry = _flash_step(q, k_ref[0, hh, lo:lo + tail, :], v_ref[0, hh, lo:lo + tail, :], *carry)
        acc = carry[1]
        outs.append(acc[:, :V_DIM] / acc[:, V_DIM:V_DIM + 1])
    o_ref[0] = jnp.concatenate(outs, axis=-1).astype(BF16)


def _attention(q, k, v, latent, prev=None):
    if latent:
        nq, q0, kv_rows, kv_blk = SEQ // TQ, 0, S_ALL, 0
        kern = functools.partial(_attn_kernel, n_full=SEQ // TK, tk=TK, tail=CTX_LEN)
    else:
        nq, q0, kv_rows, kv_blk = CTX_LEN // TQ, SEQ // TQ, CTX_LEN, SEQ // CTX_LEN
        kern = functools.partial(_attn_kernel, n_full=0, tk=TK, tail=CTX_LEN)
    out_shape = jax.ShapeDtypeStruct((BATCH, S_ALL, MLA_HEADS * V_DIM), BF16)
    out_spec = pl.BlockSpec((1, TQ, 2 * V_DIM), lambda b, hp, i: (b, q0 + i, hp))
    in_specs = [
        pl.BlockSpec((1, 2, TQ, HEAD_PAD), lambda b, hp, i: (b, hp, q0 + i, 0)),
        pl.BlockSpec((1, 2, kv_rows, HEAD_PAD), lambda b, hp, i: (b, hp, kv_blk, 0)),
        pl.BlockSpec((1, 2, kv_rows, HEAD_PAD), lambda b, hp, i: (b, hp, kv_blk, 0)),
    ]
    args = [q, k, v]
    aliases = {}
    if prev is not None:
        in_specs.append(pl.BlockSpec(memory_space=pl.ANY))
        args.append(prev)
        aliases = {3: 0}
        kern = functools.partial(_drop_last_input, kern)
    return pl.pallas_call(
        kern,
        out_shape=out_shape,
        grid=(BATCH, MLA_HEADS // 2, nq),
        in_specs=in_specs,
        out_specs=out_spec,
        input_output_aliases=aliases,
        compiler_params=_cp("parallel", "parallel", "parallel"),
        name="attn_lat" if latent else "attn_ctx",
    )(*args)


def _drop_last_input(kern, *refs):
    n_out = 1
    return kern(*refs[:-(n_out + 1)], *refs[-n_out:])


def _short_conv_kernel(p_ref, w_ref, b_ref, o_ref):
    x = p_ref[0]
    rows = lax.broadcasted_iota(jnp.int32, (S_ALL, 1), 0)
    first = (rows == 0) | (rows == SEQ)
    last = (rows == SEQ - 1) | (rows == S_ALL - 1)
    prev = jnp.where(first, 0.0, pltpu.roll(x, 1, 0))
    nxt = jnp.where(last, 0.0, pltpu.roll(x, S_ALL - 1, 0))
    w = w_ref[0]
    o_ref[0] = b_ref[0] + prev * w[0:1] + x * w[1:2] + nxt * w[2:3]


def _short_conv(l, p_hy, sc_w, sc_b):
    cb = LANES
    return pl.pallas_call(
        _short_conv_kernel,
        out_shape=jax.ShapeDtypeStruct(p_hy.shape, F32),
        grid=(BATCH, HY_COLS // cb),
        in_specs=[
            pl.BlockSpec((1, S_ALL, cb), lambda b, j: (b, 0, j)),
            pl.BlockSpec((1, SHORT_K, cb), lambda b, j: (l, 0, j)),
            pl.BlockSpec((1, 1, cb), lambda b, j: (l, 0, j)),
        ],
        out_specs=pl.BlockSpec((1, S_ALL, cb), lambda b, j: (b, 0, j)),
        compiler_params=_cp("parallel", "parallel"),
        name="short_conv",
    )(p_hy, sc_w, sc_b)


def _filter_kernel(z_ref, w1_ref, b1_ref, fr_ref, w2_ref, b2_ref, w3f_ref, w3b_ref, dl_ref, hsd_ref, nyq_ref):
    z = z_ref[...]
    fr = fr_ref[0]
    a = jnp.sin(fr * (_dot3(z, w1_ref[0]) + b1_ref[0]))
    a = jnp.sin(fr * (_dot3(a, w2_ref[0]) + b2_ref[0]))
    decay = jnp.exp(-z[:, 0:1] * jnp.abs(dl_ref[...]))
    rows = lax.broadcasted_iota(jnp.int32, (z.shape[0], 1), 0)

    def one_direction(w3_ref):
        h = _dot3(a, w3_ref[0]) * decay
        return h * lax.rsqrt(jnp.sum(h * h, axis=0, keepdims=True) + EPS)

    hf = one_direction(w3f_ref)
    hb = jnp.where(rows == 0, 0.0, one_direction(w3b_ref))
    hs = hf + hb
    hsd_ref[0] = hs.astype(BF16)
    hsd_ref[1] = (hf - hb).astype(BF16)
    sign = jnp.where((rows & 1) == 0, 1.0, -1.0)
    nyq_ref[0] = jnp.sum(hs * sign, axis=0, keepdims=True)


def _filters(l, L, zemb, deltas, flt):
    w1, b1, fr, w2, b2, w3 = flt
    ncb = HY_WIDTH // LANES
    full = lambda *shape: pl.BlockSpec((1,) + shape, lambda o, j: (l,) + (0,) * len(shape))
    return pl.pallas_call(
        _filter_kernel,
        out_shape=(
            jax.ShapeDtypeStruct((2, L, HY_ORDER * HY_WIDTH), BF16),
            jax.ShapeDtypeStruct((HY_ORDER, 1, HY_WIDTH), F32),
        ),
        grid=(HY_ORDER, ncb),
        in_specs=[
            pl.BlockSpec((L, LANES), lambda o, j: (0, 0)),
            full(LANES, HY_FFN), full(1, HY_FFN), full(1, HY_FFN),
            full(HY_FFN, HY_FFN), full(1, HY_FFN),
            pl.BlockSpec((1, HY_FFN, LANES), lambda o, j: (l, 0, o * 2 * ncb + j)),
            pl.BlockSpec((1, HY_FFN, LANES), lambda o, j: (l, 0, o * 2 * ncb + ncb + j)),
            pl.BlockSpec((1, LANES), lambda o, j: (0, j)),
        ],
        out_specs=(
            pl.BlockSpec((2, L, LANES), lambda o, j: (0, 0, o * ncb + j)),
            pl.BlockSpec((1, 1, LANES), lambda o, j: (o, 0, j)),
        ),
        compiler_params=_cp("parallel", "parallel"),
        name="hy_filter",
    )(zemb, w1, b1, fr, w2, b2, w3, w3, deltas)


def _spec_kernel(f_ref, h_ref, nyq_ref, o_ref):
    i, ri = pl.program_id(0), pl.program_id(1)
    out = _dot(f_ref[0, 0], h_ref[0])
    rows = lax.broadcasted_iota(jnp.int32, (FB, 1), 0)
    packed = (rows == 0) & (i == 0) & (ri == 1)
    o_ref[0, 0] = jnp.where(packed, nyq_ref[0], out)


def _spectrum(L, fmat, hsd, nyq):
    nfb = L // FB
    ncol = HY_ORDER * HY_WIDTH
    return pl.pallas_call(
        _spec_kernel,
        out_shape=jax.ShapeDtypeStruct((nfb, 2, FB, ncol), F32),
        grid=(nfb, 2, HY_ORDER),
        in_specs=[
            pl.BlockSpec((1, 1, FB, L), lambda i, ri, o: (i, ri, 0, 0)),
            pl.BlockSpec((1, L, HY_WIDTH), lambda i, ri, o: (ri, 0, o)),
            pl.BlockSpec((1, 1, HY_WIDTH), lambda i, ri, o: (o, 0, 0)),
        ],
        out_specs=pl.BlockSpec((1, 1, FB, HY_WIDTH), lambda i, ri, o: (i, ri, 0, o)),
        compiler_params=_cp("parallel", "parallel", "parallel"),
        name="hy_spectrum",
    )(fmat, hsd, nyq)


def _fwd_kernel(z_ref, f_ref, h_ref, p_ref, zb_ref):
    i = pl.program_id(1)

    @pl.when(i == 0)
    def _():
        zb_ref[...] = z_ref[0].astype(BF16)

    zb = zb_ref[...]
    zr = _dot(f_ref[0, 0], zb)
    zi = _dot(f_ref[0, 1], zb)
    hr, hi = h_ref[0, 0], h_ref[0, 1]
    rows = lax.broadcasted_iota(jnp.int32, (FB, 1), 0)
    packed = (rows == 0) & (i == 0)
    zihi = zi * hi
    p_ref[0, 0, 0] = (zr * hr - jnp.where(packed, 0.0, zihi)).astype(BF16)
    p_ref[0, 0, 1] = jnp.where(packed, zihi, zr * hi + zi * hr).astype(BF16)


def _hy_fwd(L, z, z_rowblk, z_colblk, fmat, spec, order):
    nfb = L // FB
    return pl.pallas_call(
        _fwd_kernel,
        out_shape=jax.ShapeDtypeStruct((BATCH, nfb, 2, FB, HY_WIDTH), BF16),
        grid=(BATCH, nfb),
        in_specs=[
            pl.BlockSpec((1, L, HY_WIDTH), lambda b, i: (b, z_rowblk, z_colblk)),
            pl.BlockSpec((1, 2, FB, L), lambda b, i: (i, 0, 0, 0)),
            pl.BlockSpec((1, 2, FB, HY_WIDTH), lambda b, i: (i, 0, 0, order)),
        ],
        out_specs=pl.BlockSpec((1, 1, 2, FB, HY_WIDTH), lambda b, i: (b, i, 0, 0, 0)),
        scratch_shapes=[pltpu.VMEM((L, HY_WIDTH), BF16)],
        compiler_params=_cp("parallel", "arbitrary"),
        name="hy_fwd",
    )(z, fmat, spec)


def _inv_kernel(p_ref, g_ref, gate_ref, z_ref, bias_ref, o_ref):
    y = _dot(g_ref[...], p_ref[0])
    o_ref[0] = gate_ref[0] * (y + bias_ref[0, 0] * z_ref[0])


def _hy_inv(l, L, p, gmat, u, row0, gate_colblk, z, z_row0, z_colblk, flt_bias, order, out_rows, prev=None):
    tmi = TM
    r0, zr0 = row0 // tmi, z_row0 // tmi
    in_specs = [
        pl.BlockSpec((1, 2 * L, HY_WIDTH), lambda b, i: (b, 0, 0)),
        pl.BlockSpec((tmi, 2 * L), lambda b, i: (i, 0)),
        pl.BlockSpec((1, tmi, HY_WIDTH), lambda b, i: (b, r0 + i, gate_colblk)),
        pl.BlockSpec((1, tmi, HY_WIDTH), lambda b, i: (b, zr0 + i, z_colblk)),
        pl.BlockSpec((1, 1, 1, HY_WIDTH), lambda b, i: (l, order, 0, 0)),
    ]
    args = [p.reshape(BATCH, 2 * L, HY_WIDTH), gmat, u, z, flt_bias]
    o0 = (row0 // tmi) if out_rows == S_ALL else 0
    kern = _inv_kernel
    aliases = {}
    if prev is not None:
        in_specs.append(pl.BlockSpec(memory_space=pl.ANY))
        args.append(prev)
        aliases = {5: 0}
        kern = functools.partial(_drop_last_input, _inv_kernel)
    return pl.pallas_call(
        kern,
        out_shape=jax.ShapeDtypeStruct((BATCH, out_rows, HY_WIDTH), F32),
        grid=(BATCH, L // tmi),
        in_specs=in_specs,
        out_specs=pl.BlockSpec((1, tmi, HY_WIDTH), lambda b, i: (b, o0 + i, 0)),
        input_output_aliases=aliases,
        compiler_params=_cp("parallel", "parallel"),
        name="hy_inv",
    )(*args)


def _merge_kernel(h_ref, sg_ref, hy_ref, att_ref, mod_ref, g_ref, who_ref, wmo_ref, wout_ref, wr_ref,
                  h1_ref, u2_ref, lg_ref):
    sg = sg_ref[0].astype(F32)
    m = (sg[:, :D_MODEL] * _dot(hy_ref[0].astype(BF16), who_ref[0])
         + sg[:, D_MODEL:] * _dot(att_ref[0], wmo_ref[0]))
    y = _dot(m.astype(BF16), wout_ref[0])
    g1 = mod_ref[0, 0, 0, 2:3, :]
    sh2 = mod_ref[0, 0, 0, 3:4, :]
    sc2 = mod_ref[0, 0, 0, 4:5, :]
    h1 = h_ref[0] + g1 * y
    h1_ref[0] = h1
    u2 = _rms(h1, g_ref[0]) * (1.0 + sc2) + sh2
    u2_ref[0] = u2.astype(BF16)
    lg_ref[0] = _dot3(u2, wr_ref[0])


def _merge(l, n_tiles, h, sg, hy, att, mods, norm2_g, who, wmo, wout, wr):
    rows = n_tiles * TM
    tile = lambda w: pl.BlockSpec((1, TM, w), lambda b, i: (b, i, 0))
    wfull = lambda a: pl.BlockSpec((1,) + a.shape[1:], lambda b, i: (l, 0, 0))
    return pl.pallas_call(
        _merge_kernel,
        out_shape=(
            jax.ShapeDtypeStruct((BATCH, rows, D_MODEL), F32),
            jax.ShapeDtypeStruct((BATCH, rows, D_MODEL), BF16),
            jax.ShapeDtypeStruct((BATCH, rows, ROUTER_PAD), F32),
        ),
        grid=(BATCH, n_tiles),
        in_specs=[
            tile(D_MODEL), tile(2 * D_MODEL), tile(HY_WIDTH), tile(MLA_HEADS * V_DIM),
            pl.BlockSpec((1, 1, 1, N_MOD, D_MODEL), lambda b, i: (l, b, i // N_LAT_TILES, 0, 0)),
            pl.BlockSpec((1, 1, D_MODEL), lambda b, i: (l, 0, 0)),
            wfull(who), wfull(wmo), wfull(wout), wfull(wr),
        ],
        out_specs=(tile(D_MODEL), tile(D_MODEL), tile(ROUTER_PAD)),
        compiler_params=_cp("parallel", "parallel"),
        name="merge",
    )(h, sg, hy, att, mods, norm2_g, who, wmo, wout, wr)


def _expert_kernel(be_ref, nu_ref, x_ref, wt_ref, wg_ref, wu_ref, wd_ref, o_ref):
    i = pl.program_id(0)

    @pl.when(i < nu_ref[0])
    def _():
        x = x_ref[...]
        hg = _dot(x, wg_ref[0, 0].astype(BF16))
        hu = _dot(x, wu_ref[0, 0].astype(BF16))
        hb = (hg * jax.nn.sigmoid(hg) * hu).astype(BF16)
        o_ref[...] = _dot(hb, wd_ref[0, 0].astype(BF16)) * wt_ref[...]

    @pl.when(i >= nu_ref[0])
    def _():
        o_ref[...] = jnp.zeros_like(o_ref)


def _experts(l, block_expert, n_used, xs, slot_w, w_gate_e, w_up_e, w_down_e):
    slots = xs.shape[0]
    n_blocks = slots // MOE_BM
    grid_spec = pltpu.PrefetchScalarGridSpec(
        num_scalar_prefetch=2,
        grid=(n_blocks,),
        in_specs=[
            pl.BlockSpec((MOE_BM, D_MODEL), lambda i, be, nu: (i, 0)),
            pl.BlockSpec((MOE_BM, 1), lambda i, be, nu: (i, 0)),
            pl.BlockSpec((1, 1, D_MODEL, D_EXPERT), lambda i, be, nu: (l, be[i], 0, 0)),
            pl.BlockSpec((1, 1, D_MODEL, D_EXPERT), lambda i, be, nu: (l, be[i], 0, 0)),
            pl.BlockSpec((1, 1, D_EXPERT, D_MODEL), lambda i, be, nu: (l, be[i], 0, 0)),
        ],
        out_specs=pl.BlockSpec((MOE_BM, D_MODEL), lambda i, be, nu: (i, 0)),
    )
    return pl.pallas_call(
        _expert_kernel,
        out_shape=jax.ShapeDtypeStruct((slots, D_MODEL), F32),
        grid_spec=grid_spec,
        compiler_params=_cp("arbitrary"),
        name="experts",
    )(block_expert, n_used, xs, slot_w, w_gate_e, w_up_e, w_down_e)


def _route(logits, b_rg, b_re):
    T = logits.shape[0]
    lg = logits[:, :N_GROUPS]
    le = logits[:, N_GROUPS:N_GROUPS + N_EXPERTS].reshape(T, N_GROUPS, EXP_PER_GROUP)
    g_sel = jnp.argmax(lg + b_rg, axis=-1)
    p_g = jnp.take_along_axis(jax.nn.softmax(lg, axis=-1), g_sel[:, None], axis=-1)
    le_g = jnp.take_along_axis(le, g_sel[:, None, None], axis=1)[:, 0]
    be_g = b_re.reshape(N_GROUPS, EXP_PER_GROUP)[g_sel]
    _, local = lax.top_k(le_g + be_g, TOP_K)
    w_local = jax.nn.softmax(jnp.take_along_axis(le_g, local, axis=-1), axis=-1)
    expert = (g_sel[:, None] * EXP_PER_GROUP + local).astype(jnp.int32)
    return expert.reshape(-1), (p_g * w_local).reshape(-1)


def _dispatch(expert, weight, T):
    A = expert.shape[0]
    token = jnp.repeat(jnp.arange(T, dtype=jnp.int32), TOP_K)
    order = jnp.argsort(expert)
    e_sorted = expert[order]
    counts = jnp.zeros((N_EXPERTS,), jnp.int32).at[expert].add(1)
    padded = (counts + MOE_BM - 1) // MOE_BM * MOE_BM
    start = jnp.cumsum(counts) - counts
    p_end = jnp.cumsum(padded)
    p_start = p_end - padded
    dest = p_start[e_sorted] + jnp.arange(A, dtype=jnp.int32) - start[e_sorted]
    n_blocks = -(-A // MOE_BM) + N_EXPERTS
    slots = n_blocks * MOE_BM
    slot_token = jnp.full((slots,), T, jnp.int32).at[dest].set(token[order])
    slot_weight = jnp.zeros((slots,), F32).at[dest].set(weight[order])
    block_expert = jnp.minimum(
        jnp.searchsorted(p_end, jnp.arange(n_blocks, dtype=jnp.int32) * MOE_BM, side="right"),
        N_EXPERTS - 1).astype(jnp.int32)
    n_used = (p_end[-1:] // MOE_BM).astype(jnp.int32)
    return slot_token, slot_weight, block_expert, n_used


def _combine_kernel(h_ref, y_ref, mod_ref, o_ref):
    o_ref[0] = h_ref[0] + mod_ref[0, 0, 0, 5:6, :] * y_ref[0]


def _final_kernel(h_ref, y_ref, mod_ref, g_ref, o_ref):
    h2 = h_ref[0] + mod_ref[0, 0, 0, 5:6, :] * y_ref[0]
    o_ref[0] = _rms(h2, g_ref[...])


def _combine(l, n_tiles, h1, y, mods, final_g=None):
    rows = n_tiles * TM
    tile = pl.BlockSpec((1, TM, D_MODEL), lambda b, i: (b, i, 0))
    in_specs = [tile, tile,
                pl.BlockSpec((1, 1, 1, N_MOD, D_MODEL), lambda b, i: (l, b, i // N_LAT_TILES, 0, 0))]
    args = [h1, y, mods]
    kern = _combine_kernel
    if final_g is not None:
        in_specs.append(pl.BlockSpec((1, D_MODEL), lambda b, i: (0, 0)))
        args.append(final_g.reshape(1, D_MODEL))
        kern = _final_kernel
    return pl.pallas_call(
        kern,
        out_shape=jax.ShapeDtypeStruct((BATCH, rows, D_MODEL), F32),
        grid=(BATCH, n_tiles),
        in_specs=in_specs,
        out_specs=tile,
        compiler_params=_cp("parallel", "parallel"),
        name="combine",
    )(*args)


def _rope_tables():
    n = jnp.arange(SEQ)
    pos = jnp.stack([n // GRID_W, n % GRID_W], axis=-1).astype(F32)
    inv = ROPE_BASE ** (-jnp.arange(ROPE_PAIRS, dtype=F32) / ROPE_PAIRS)
    ang = pos[:, :, None] * inv
    cos, sin = jnp.cos(ang), jnp.sin(ang)
    cos32 = jnp.stack([cos, cos], axis=2).reshape(SEQ, QK_ROPE)
    sin32 = jnp.stack([-sin, sin], axis=2).reshape(SEQ, QK_ROPE)
    one = jnp.ones((SEQ, QK_ROPE), F32)
    zero = jnp.zeros((SEQ, QK_ROPE), F32)
    cosq = jnp.concatenate([jnp.ones((SEQ, QK_NOPE), F32), cos32, one], axis=1) * ATTN_SCALE
    sinq = jnp.concatenate([jnp.zeros((SEQ, QK_NOPE), F32), sin32, zero], axis=1) * ATTN_SCALE
    cosk = jnp.concatenate([jnp.zeros((SEQ, QK_NOPE), F32), cos32, zero], axis=1)
    sink = jnp.concatenate([jnp.zeros((SEQ, QK_NOPE), F32), sin32, zero], axis=1)
    lat = jnp.stack([cosq, sinq, cosk, sink])
    ctx_row = jnp.concatenate([jnp.ones((QK_NOPE + 2 * QK_ROPE,), F32)]) * ATTN_SCALE
    ctx_cosk = jnp.concatenate([jnp.zeros((QK_NOPE + QK_ROPE,), F32), jnp.ones((QK_ROPE,), F32)])
    ctx = jnp.stack([ctx_row, jnp.zeros_like(ctx_row), ctx_cosk, jnp.zeros_like(ctx_row)])
    ctx = jnp.broadcast_to(ctx[:, None, :], (4, CTX_LEN, HEAD_PAD))
    return jnp.concatenate([lat, ctx], axis=1)


def _dft_matrices(L):
    N = 2 * L
    k = jnp.arange(L, dtype=jnp.int32)[:, None]
    t = jnp.arange(L, dtype=jnp.int32)[None, :]
    ang = ((k * t) % N).astype(F32) * (2.0 * math.pi / N)
    c = jnp.cos(ang)
    s = -jnp.sin(ang)
    alt = jnp.where((t % 2) == 0, 1.0, -1.0).astype(F32)
    s = jnp.where(k == 0, alt, s)
    f = jnp.stack([c.reshape(L // FB, FB, L), s.reshape(L // FB, FB, L)], axis=1)
    wk = jnp.where(k == 0, 1.0 / N, 2.0 / N).astype(F32)
    g = jnp.stack([(c * wk).reshape(L // FB, FB, L), (s * wk).reshape(L // FB, FB, L)], axis=1)
    g = g.reshape(2 * L, L).T
    return f.astype(BF16), g.astype(BF16)


def _filter_embedding(L):
    t = jnp.linspace(0.0, 1.0, L, dtype=F32)[:, None]
    w = 2.0 * math.pi * jnp.arange(L, dtype=F32)[:, None] / L
    bands = jnp.linspace(1e-4, HY_BANDS - 1, HY_BANDS, dtype=F32)[None, :]
    return jnp.concatenate([t, jnp.cos(bands * w), -jnp.sin(bands * w),
                            jnp.zeros((L, LANES - HY_EMB), F32)], axis=-1)


def _rope_swap_perm():
    idx = np.arange(QK_ROPE)
    axis, half, pair = idx // (2 * ROPE_PAIRS), (idx // ROPE_PAIRS) % 2, idx % ROPE_PAIRS
    return axis * 2 * ROPE_PAIRS + (1 - half) * ROPE_PAIRS + pair


def _prep_weights(w_in, w_uq, w_ukv, w_hy_o, w_mla_o, w_out, w_router_g, w_router_e):
    perm = _rope_swap_perm()
    c_q = HY_COLS
    c_kv = HY_COLS + Q_LORA
    c_kr = c_kv + KV_LORA
    w_kr = w_in[:, :, c_kr:MLA_END]
    zeros = lambda n: jnp.zeros((DEPTH, D_MODEL, n), F32)
    wkr = jnp.concatenate([zeros(QK_NOPE), w_kr, w_kr,
                           zeros(QK_NOPE), w_kr[:, :, perm], zeros(QK_ROPE)], axis=-1)
    uq = w_uq.reshape(DEPTH, Q_LORA, MLA_HEADS, QK_NOPE + QK_ROPE)
    uq_r = uq[..., QK_NOPE:]
    wuq = jnp.concatenate([uq, uq_r], axis=-1).reshape(DEPTH, Q_LORA, MLA_HEADS * HEAD_PAD)
    wuqs = jnp.concatenate([jnp.zeros_like(uq[..., :QK_NOPE]), uq_r[..., perm], jnp.zeros_like(uq_r)],
                           axis=-1).reshape(DEPTH, Q_LORA, MLA_HEADS * HEAD_PAD)
    ukv = w_ukv.reshape(DEPTH, KV_LORA, MLA_HEADS, QK_NOPE + V_DIM)
    zpad = jnp.zeros_like(ukv[..., :HEAD_PAD - QK_NOPE])
    wuk = jnp.concatenate([ukv[..., :QK_NOPE], zpad], axis=-1).reshape(DEPTH, KV_LORA, MLA_HEADS * HEAD_PAD)
    wuv = jnp.concatenate([ukv[..., QK_NOPE:], zpad], axis=-1).reshape(DEPTH, KV_LORA, MLA_HEADS * HEAD_PAD)
    wr = jnp.concatenate([w_router_g, w_router_e,
                          jnp.zeros((DEPTH, D_MODEL, ROUTER_PAD - N_GROUPS - N_EXPERTS), F32)], axis=-1)
    bf = lambda a: a.astype(BF16)
    return dict(
        why=bf(w_in[:, :, :HY_COLS]), wgt=bf(w_in[:, :, MLA_END:]),
        wq=bf(w_in[:, :, c_q:c_kv]), wkv=bf(w_in[:, :, c_kv:c_kr]), wkr=bf(wkr),
        wuq=bf(wuq), wuqs=bf(wuqs), wuk=bf(wuk), wuv=bf(wuv),
        who=bf(w_hy_o), wmo=bf(w_mla_o), wout=bf(w_out), wr=wr,
    )


def _hyena(l, L, row0, u, fmat, gmat, zemb, deltas, flt, flt_bias, hy_prev):
    hsd, nyq = _filters(l, L, zemb, deltas, flt)
    spec = _spectrum(L, fmat, hsd, nyq)
    rb = row0 // L
    p = _hy_fwd(L, u, rb, 0, fmat, spec, 0)
    z1 = _hy_inv(l, L, p, gmat, u, row0, 1, u, row0, 0, flt_bias, 0, L)
    p = _hy_fwd(L, z1, 0, 0, fmat, spec, 1)
    return _hy_inv(l, L, p, gmat, u, row0, 2, z1, 0, 0, flt_bias, 1, S_ALL, prev=hy_prev)


def kernel(x, c, ctx, c_ctx, w_ada, b_ada, norm1_g, w_in, q_norm_g, kv_norm_g, w_uq, w_ukv,
           sc_w, sc_b, flt_w1, flt_b1, flt_freq, flt_w2, flt_b2, flt_w3, flt_bias,
           w_hy_o, w_mla_o, w_out, norm2_g, w_router_g, b_router_g, w_router_e, b_router_e,
           w_gate_e, w_up_e, w_down_e, final_g):
    wts = _prep_weights(w_in, w_uq, w_ukv, w_hy_o, w_mla_o, w_out, w_router_g, w_router_e)
    tabs = _rope_tables()
    f_lat, g_lat = _dft_matrices(SEQ)
    f_ctx, g_ctx = _dft_matrices(CTX_LEN)
    z_lat, z_ctx = _filter_embedding(SEQ), _filter_embedding(CTX_LEN)
    deltas = jnp.linspace(math.log(HY_TARGET) / HY_FAST_DECAY, math.log(HY_TARGET) / HY_SLOW_DECAY,
                          HY_WIDTH, dtype=F32)[None, :]

    cc = jnp.concatenate([c, c_ctx[None, :], jnp.zeros((3, D_MODEL), F32)], axis=0)
    mod = _ada(cc, w_ada, b_ada).reshape(DEPTH, 8, N_MOD, D_MODEL)
    mods = jnp.stack([mod[:, :BATCH], jnp.broadcast_to(mod[:, BATCH:BATCH + 1], (DEPTH, BATCH, N_MOD, D_MODEL))],
                     axis=2)

    r3 = lambda a: a.reshape(DEPTH, 1, a.shape[-1])
    norm1, norm2, qg, kvg = r3(norm1_g), r3(norm2_g), r3(q_norm_g), r3(kv_norm_g)
    scb = r3(sc_b)
    fb1, ffr, fb2 = r3(flt_b1), r3(flt_freq), r3(flt_b2)
    fbias = flt_bias.reshape(DEPTH, HY_ORDER, 1, HY_WIDTH)
    w1_pad = jnp.concatenate([flt_w1, jnp.zeros((DEPTH, LANES - HY_EMB, HY_FFN), F32)], axis=1)
    flt = (w1_pad, fb1, ffr, flt_w2, fb2, flt_w3)

    h = jnp.concatenate([x, ctx], axis=1)
    for l in range(DEPTH):
        last = l == DEPTH - 1
        n_tiles = N_LAT_TILES if last else N_ALL_TILES
        p_hy, sg, q, k, v = _inproj(l, h, mods, norm1, tabs, qg, kvg, wts)

        att = _attention(q, k, v, latent=True)
        if not last:
            att = _attention(q, k, v, latent=False, prev=att)

        u = _short_conv(l, p_hy, sc_w, scb)
        hy = None
        if not last:
            hy = _hyena(l, CTX_LEN, SEQ, u, f_ctx, g_ctx, z_ctx, deltas, flt, fbias, None)
        hy = _hyena(l, SEQ, 0, u, f_lat, g_lat, z_lat, deltas, flt, fbias, hy)

        h1, u2, logits = _merge(l, n_tiles, h, sg, hy, att, mods, norm2,
                                wts["who"], wts["wmo"], wts["wout"], wts["wr"])

        T = BATCH * n_tiles * TM
        expert, weight = _route(logits.reshape(T, ROUTER_PAD), b_router_g[l], b_router_e[l])
        slot_token, slot_weight, block_expert, n_used = _dispatch(expert, weight, T)
        u2f = jnp.concatenate([u2.reshape(T, D_MODEL), jnp.zeros((1, D_MODEL), BF16)], axis=0)
        yb = _experts(l, block_expert, n_used, u2f[slot_token], slot_weight[:, None],
                      w_gate_e, w_up_e, w_down_e)
        y = jnp.zeros((T + 1, D_MODEL), F32).at[slot_token].add(yb)[:T]
        h = _combine(l, n_tiles, h1, y.reshape(BATCH, n_tiles * TM, D_MODEL), mods,
                     final_g if last else None)
    return h
```

```python
import functools
import math

import jax
import jax.numpy as jnp
import numpy as np
from jax import lax
from jax.experimental import pallas as pl
from jax.experimental.pallas import tpu as pltpu

F32 = jnp.float32
BF16 = jnp.bfloat16

D_MODEL = 1024
BATCH = 4
SEQ = 4096
DEPTH = 4
GRID_W = 64
CTX_LEN = 256
S_ALL = SEQ + CTX_LEN
EPS = 1e-6
N_MOD = 6

HY_WIDTH = 512
HY_ORDER = 2
HY_BANDS = 16
HY_EMB = 1 + 2 * HY_BANDS
HY_FFN = 64
SHORT_K = 3
HY_FAST_DECAY = 0.3
HY_SLOW_DECAY = 1.5
HY_TARGET = 1e-2

MLA_HEADS = 8
QK_NOPE = 64
QK_ROPE = 32
V_DIM = 64
Q_LORA = 384
KV_LORA = 256
ROPE_PAIRS = QK_ROPE // 4
ROPE_BASE = 10000.0
ATTN_SCALE = (QK_NOPE + QK_ROPE) ** -0.5
Q_SCALE = ATTN_SCALE * math.log2(math.e)

N_GROUPS = 8
EXP_PER_GROUP = 8
N_EXPERTS = N_GROUPS * EXP_PER_GROUP
TOP_K = 2
D_EXPERT = 256

HY_COLS = (HY_ORDER + 1) * HY_WIDTH
MLA_END = HY_COLS + Q_LORA + KV_LORA + QK_ROPE

LANES = 128
HEAD_PAD = LANES
VMEM_LIMIT = 56 << 20

TM = 256
N_LAT_TILES = SEQ // TM
N_ALL_TILES = S_ALL // TM
TQ = 1024
TK = 512
FB = 256
MOE_BM = 256
ROUTER_PAD = LANES


def _cp(*sem):
    return pltpu.CompilerParams(dimension_semantics=sem, vmem_limit_bytes=VMEM_LIMIT)


def _dot(a, b):
    return jnp.dot(a, b, preferred_element_type=F32)


def _split(a):
    hi = a.astype(BF16)
    lo = (a - hi.astype(F32)).astype(BF16)
    return hi, lo


def _dot3(a, b):
    ah, al = _split(a)
    bh, bl = _split(b)
    return _dot(ah, bh) + (_dot(ah, bl) + _dot(al, bh))


def _rms(x, g):
    return x * lax.rsqrt(jnp.mean(x * x, axis=-1, keepdims=True) + EPS) * g


def _ada_kernel(c_ref, w_ref, b_ref, o_ref):
    cc = c_ref[...]
    s = cc * jax.nn.sigmoid(cc)
    o_ref[0] = _dot3(s, w_ref[0]) + b_ref[0]


def _ada(cc, w_ada, b_ada):
    tn = 1536
    n = N_MOD * D_MODEL
    return pl.pallas_call(
        _ada_kernel,
        out_shape=jax.ShapeDtypeStruct((DEPTH, 8, n), F32),
        grid=(DEPTH, n // tn),
        in_specs=[
            pl.BlockSpec((8, D_MODEL), lambda l, j: (0, 0)),
            pl.BlockSpec((1, D_MODEL, tn), lambda l, j: (l, 0, j)),
            pl.BlockSpec((1, 1, tn), lambda l, j: (l, 0, j)),
        ],
        out_specs=pl.BlockSpec((1, 8, tn), lambda l, j: (l, 0, j)),
        compiler_params=_cp("parallel", "parallel"),
        name="ada",
    )(cc, w_ada, b_ada.reshape(DEPTH, 1, n))


def _inproj_kernel(h_ref, mod_ref, g_ref, tab_ref, qg_ref, kvg_ref,
                   why_ref, wgt_ref, wq_ref, wkv_ref, wkr_ref, wuq_ref, wuqs_ref, wuk_ref, wuv_ref,
                   phy_ref, sg_ref, q_ref, k_ref, v_ref):
    h = h_ref[0]
    shift = mod_ref[0, 0, 0, 0:1, :]
    scale = mod_ref[0, 0, 0, 1:2, :]
    u = (_rms(h, g_ref[0]) * (1.0 + scale) + shift).astype(BF16)

    phy_ref[0] = _dot(u, why_ref[0])
    sg_ref[0] = jax.nn.sigmoid(_dot(u, wgt_ref[0])).astype(BF16)

    cosq, sinq, cosk, sink = tab_ref[0], tab_ref[1], tab_ref[2], tab_ref[3]
    cq = _rms(_dot(u, wq_ref[0]), qg_ref[0]).astype(BF16)
    qa = _dot(cq, wuq_ref[0])
    qs = _dot(cq, wuqs_ref[0])
    ckv = _rms(_dot(u, wkv_ref[0]), kvg_ref[0]).astype(BF16)
    ka = _dot(ckv, wuk_ref[0])
    va = _dot(ckv, wuv_ref[0])
    kr = _dot(u, wkr_ref[0])
    krk = kr[:, :HEAD_PAD] * cosk + kr[:, HEAD_PAD:] * sink
    ones_col = (lax.broadcasted_iota(jnp.int32, (1, HEAD_PAD), 1) == V_DIM).astype(F32)
    for hh in range(MLA_HEADS):
        sl = slice(hh * HEAD_PAD, (hh + 1) * HEAD_PAD)
        q_ref[0, hh] = (qa[:, sl] * cosq + qs[:, sl] * sinq).astype(BF16)
        k_ref[0, hh] = (ka[:, sl] + krk).astype(BF16)
        v_ref[0, hh] = (va[:, sl] + ones_col).astype(BF16)


def _inproj(l, h, mods, norm1_g, tabs, q_norm_g, kv_norm_g, wts):
    names = ("why", "wgt", "wq", "wkv", "wkr", "wuq", "wuqs", "wuk", "wuv")
    w_specs = [pl.BlockSpec((1,) + wts[n].shape[1:], lambda b, i: (l, 0, 0)) for n in names]
    hd = MLA_HEADS * HEAD_PAD
    qkv_shape = jax.ShapeDtypeStruct((BATCH, MLA_HEADS, S_ALL, HEAD_PAD), BF16)
    qkv_spec = pl.BlockSpec((1, MLA_HEADS, TM, HEAD_PAD), lambda b, i: (b, 0, i, 0))
    del hd
    return pl.pallas_call(
        _inproj_kernel,
        out_shape=(
            jax.ShapeDtypeStruct((BATCH, S_ALL, HY_COLS), F32),
            jax.ShapeDtypeStruct((BATCH, S_ALL, 2 * D_MODEL), BF16),
            qkv_shape, qkv_shape, qkv_shape,
        ),
        grid=(BATCH, N_ALL_TILES),
        in_specs=[
            pl.BlockSpec((1, TM, D_MODEL), lambda b, i: (b, i, 0)),
            pl.BlockSpec((1, 1, 1, N_MOD, D_MODEL), lambda b, i: (l, b, i // N_LAT_TILES, 0, 0)),
            pl.BlockSpec((1, 1, D_MODEL), lambda b, i: (l, 0, 0)),
            pl.BlockSpec((4, TM, HEAD_PAD), lambda b, i: (0, i, 0)),
            pl.BlockSpec((1, 1, Q_LORA), lambda b, i: (l, 0, 0)),
            pl.BlockSpec((1, 1, KV_LORA), lambda b, i: (l, 0, 0)),
        ] + w_specs,
        out_specs=(
            pl.BlockSpec((1, TM, HY_COLS), lambda b, i: (b, i, 0)),
            pl.BlockSpec((1, TM, 2 * D_MODEL), lambda b, i: (b, i, 0)),
            qkv_spec, qkv_spec, qkv_spec,
        ),
        compiler_params=_cp("parallel", "parallel"),
        name="inproj",
    )(h, mods, norm1_g, tabs, q_norm_g, kv_norm_g, *[wts[n] for n in names])


def _flash_step(q_ref, k_ref, v_ref, rows, carry):
    out = []
    for hh in range(2):
        m, acc = carry[hh]
        s = lax.dot_general(q_ref[0, hh], k_ref[0, hh, rows, :], (((1,), (1,)), ((), ())),
                            preferred_element_type=F32)
        m_new = jnp.maximum(m, jnp.max(s, axis=-1, keepdims=True))
        p = jnp.exp2(s - m_new)
        acc = acc * jnp.exp2(m - m_new) + _dot(p.astype(BF16), v_ref[0, hh, rows, :])
        out.append((m_new, acc))
    return tuple(out)


def _attn_kernel(q_ref, k_ref, v_ref, o_ref, *, n_full, tk, tail):
    tq = q_ref.shape[2]
    init = (jnp.full((tq, 1), -1e30, F32), jnp.zeros((tq, HEAD_PAD), F32))
    carry = (init, init)
    if n_full:
        def body(j, carry):
            return _flash_step(q_ref, k_ref, v_ref, pl.ds(pl.multiple_of(j * tk, tk), tk), carry)
        carry = lax.fori_loop(0, n_full, body, carry)
    if tail:
        carry = _flash_step(q_ref, k_ref, v_ref, pl.ds(n_full * tk, tail), carry)
    outs = [acc[:, :V_DIM] / acc[:, V_DIM:V_DIM + 1] for _, acc in carry]
    o_ref[0] = jnp.concatenate(outs, axis=-1).astype(BF16)


def _attention(q, k, v, latent):
    if latent:
        tq, nq, q0, kv_rows, kv_blk = TQ, SEQ // TQ, 0, S_ALL, 0
        kern = functools.partial(_attn_kernel, n_full=SEQ // TK, tk=TK, tail=CTX_LEN)
    else:
        tq, nq, q0, kv_rows, kv_blk = CTX_LEN, 1, SEQ // CTX_LEN, CTX_LEN, SEQ // CTX_LEN
        kern = functools.partial(_attn_kernel, n_full=0, tk=TK, tail=CTX_LEN)
    return pl.pallas_call(
        kern,
        out_shape=jax.ShapeDtypeStruct((BATCH, nq * tq, MLA_HEADS * V_DIM), BF16),
        grid=(BATCH, MLA_HEADS // 2, nq),
        in_specs=[
            pl.BlockSpec((1, 2, tq, HEAD_PAD), lambda b, hp, i: (b, hp, q0 + i, 0)),
            pl.BlockSpec((1, 2, kv_rows, HEAD_PAD), lambda b, hp, i: (b, hp, kv_blk, 0)),
            pl.BlockSpec((1, 2, kv_rows, HEAD_PAD), lambda b, hp, i: (b, hp, kv_blk, 0)),
        ],
        out_specs=pl.BlockSpec((1, tq, 2 * V_DIM), lambda b, hp, i: (b, i, hp)),
        compiler_params=_cp("parallel", "parallel", "parallel"),
        name="attn_lat" if latent else "attn_ctx",
    )(q, k, v)


def _short_conv_kernel(p_ref, w_ref, b_ref, o_ref):
    x = p_ref[0]
    rows = lax.broadcasted_iota(jnp.int32, (S_ALL, 1), 0)
    first = (rows == 0) | (rows == SEQ)
    last = (rows == SEQ - 1) | (rows == S_ALL - 1)
    prev = jnp.where(first, 0.0, pltpu.roll(x, 1, 0))
    nxt = jnp.where(last, 0.0, pltpu.roll(x, S_ALL - 1, 0))
    w = w_ref[0]
    o_ref[0] = b_ref[0] + prev * w[0:1] + x * w[1:2] + nxt * w[2:3]


def _short_conv(l, p_hy, sc_w, sc_b):
    cb = LANES
    return pl.pallas_call(
        _short_conv_kernel,
        out_shape=jax.ShapeDtypeStruct(p_hy.shape, F32),
        grid=(BATCH, HY_COLS // cb),
        in_specs=[
            pl.BlockSpec((1, S_ALL, cb), lambda b, j: (b, 0, j)),
            pl.BlockSpec((1, SHORT_K, cb), lambda b, j: (l, 0, j)),
            pl.BlockSpec((1, 1, cb), lambda b, j: (l, 0, j)),
        ],
        out_specs=pl.BlockSpec((1, S_ALL, cb), lambda b, j: (b, 0, j)),
        compiler_params=_cp("parallel", "parallel"),
        name="short_conv",
    )(p_hy, sc_w, sc_b)


def _filter_kernel(z_ref, w1_ref, b1_ref, fr_ref, w2_ref, b2_ref, w3f_ref, w3b_ref, dl_ref, hsd_ref, nyq_ref):
    z = z_ref[...]
    fr = fr_ref[0]
    a = jnp.sin(fr * (_dot3(z, w1_ref[0]) + b1_ref[0]))
    a = jnp.sin(fr * (_dot3(a, w2_ref[0]) + b2_ref[0]))
    decay = jnp.exp(-z[:, 0:1] * jnp.abs(dl_ref[...]))
    rows = lax.broadcasted_iota(jnp.int32, (z.shape[0], 1), 0)

    def one_direction(w3_ref):
        h = _dot3(a, w3_ref[0]) * decay
        return h * lax.rsqrt(jnp.sum(h * h, axis=0, keepdims=True) + EPS)

    hf = one_direction(w3f_ref)
    hb = jnp.where(rows == 0, 0.0, one_direction(w3b_ref))
    hs = hf + hb
    hsd_ref[0] = hs.astype(BF16)
    hsd_ref[1] = (hf - hb).astype(BF16)
    sign = jnp.where((rows & 1) == 0, 1.0, -1.0)
    nyq_ref[0] = jnp.sum(hs * sign, axis=0, keepdims=True)


def _filters(l, L, zemb, deltas, flt):
    w1, b1, fr, w2, b2, w3 = flt
    ncb = HY_WIDTH // LANES
    full = lambda *shape: pl.BlockSpec((1,) + shape, lambda o, j: (l,) + (0,) * len(shape))
    return pl.pallas_call(
        _filter_kernel,
        out_shape=(
            jax.ShapeDtypeStruct((2, L, HY_ORDER * HY_WIDTH), BF16),
            jax.ShapeDtypeStruct((HY_ORDER, 1, HY_WIDTH), F32),
        ),
        grid=(HY_ORDER, ncb),
        in_specs=[
            pl.BlockSpec((L, LANES), lambda o, j: (0, 0)),
            full(LANES, HY_FFN), full(1, HY_FFN), full(1, HY_FFN),
            full(HY_FFN, HY_FFN), full(1, HY_FFN),
            pl.BlockSpec((1, HY_FFN, LANES), lambda o, j: (l, 0, o * 2 * ncb + j)),
            pl.BlockSpec((1, HY_FFN, LANES), lambda o, j: (l, 0, o * 2 * ncb + ncb + j)),
            pl.BlockSpec((1, LANES), lambda o, j: (0, j)),
        ],
        out_specs=(
            pl.BlockSpec((2, L, LANES), lambda o, j: (0, 0, o * ncb + j)),
            pl.BlockSpec((1, 1, LANES), lambda o, j: (o, 0, j)),
        ),
        compiler_params=_cp("parallel", "parallel"),
        name="hy_filter",
    )(zemb, w1, b1, fr, w2, b2, w3, w3, deltas)


def _spec_kernel(f_ref, h_ref, nyq_ref, o_ref):
    i, ri = pl.program_id(0), pl.program_id(1)
    out = _dot(f_ref[0, 0], h_ref[0])
    rows = lax.broadcasted_iota(jnp.int32, (FB, 1), 0)
    packed = (rows == 0) & (i == 0) & (ri == 1)
    o_ref[0, 0] = jnp.where(packed, nyq_ref[0], out)


def _spectrum(L, fmat, hsd, nyq):
    nfb = L // FB
    ncol = HY_ORDER * HY_WIDTH
    return pl.pallas_call(
        _spec_kernel,
        out_shape=jax.ShapeDtypeStruct((nfb, 2, FB, ncol), F32),
        grid=(nfb, 2, HY_ORDER),
        in_specs=[
            pl.BlockSpec((1, 1, FB, L), lambda i, ri, o: (i, ri, 0, 0)),
            pl.BlockSpec((1, L, HY_WIDTH), lambda i, ri, o: (ri, 0, o)),
            pl.BlockSpec((1, 1, HY_WIDTH), lambda i, ri, o: (o, 0, 0)),
        ],
        out_specs=pl.BlockSpec((1, 1, FB, HY_WIDTH), lambda i, ri, o: (i, ri, 0, o)),
        compiler_params=_cp("parallel", "parallel", "parallel"),
        name="hy_spectrum",
    )(fmat, hsd, nyq)


def _fwd_kernel(z_ref, f_ref, h_ref, p_ref, zb_ref):
    i = pl.program_id(1)

    @pl.when(i == 0)
    def _():
        zb_ref[...] = z_ref[0].astype(BF16)

    zb = zb_ref[...]
    zr = _dot(f_ref[0, 0], zb)
    zi = _dot(f_ref[0, 1], zb)
    hr, hi = h_ref[0, 0], h_ref[0, 1]
    rows = lax.broadcasted_iota(jnp.int32, (FB, 1), 0)
    packed = (rows == 0) & (i == 0)
    zihi = zi * hi
    p_ref[0, 0, 0] = (zr * hr - jnp.where(packed, 0.0, zihi)).astype(BF16)
    p_ref[0, 0, 1] = jnp.where(packed, zihi, zr * hi + zi * hr).astype(BF16)


def _hy_fwd(L, z, z_rowblk, z_colblk, fmat, spec, order):
    nfb = L // FB
    return pl.pallas_call(
        _fwd_kernel,
        out_shape=jax.ShapeDtypeStruct((BATCH, nfb, 2, FB, HY_WIDTH), BF16),
        grid=(BATCH, nfb),
        in_specs=[
            pl.BlockSpec((1, L, HY_WIDTH), lambda b, i: (b, z_rowblk, z_colblk)),
            pl.BlockSpec((1, 2, FB, L), lambda b, i: (i, 0, 0, 0)),
            pl.BlockSpec((1, 2, FB, HY_WIDTH), lambda b, i: (i, 0, 0, order)),
        ],
        out_specs=pl.BlockSpec((1, 1, 2, FB, HY_WIDTH), lambda b, i: (b, i, 0, 0, 0)),
        scratch_shapes=[pltpu.VMEM((L, HY_WIDTH), BF16)],
        compiler_params=_cp("parallel", "arbitrary"),
        name="hy_fwd",
    )(z, fmat, spec)


def _inv_kernel(p_ref, g_ref, gate_ref, z_ref, bias_ref, o_ref):
    y = _dot(g_ref[...], p_ref[0])
    o_ref[0] = gate_ref[0] * (y + bias_ref[0, 0] * z_ref[0])


def _hy_inv(l, L, p, gmat, u, row0, gate_colblk, z, z_row0, z_colblk, flt_bias, order):
    tmi = TM
    r0, zr0 = row0 // tmi, z_row0 // tmi
    return pl.pallas_call(
        _inv_kernel,
        out_shape=jax.ShapeDtypeStruct((BATCH, L, HY_WIDTH), F32),
        grid=(BATCH, L // tmi),
        in_specs=[
            pl.BlockSpec((1, 2 * L, HY_WIDTH), lambda b, i: (b, 0, 0)),
            pl.BlockSpec((tmi, 2 * L), lambda b, i: (i, 0)),
            pl.BlockSpec((1, tmi, HY_WIDTH), lambda b, i: (b, r0 + i, gate_colblk)),
            pl.BlockSpec((1, tmi, HY_WIDTH), lambda b, i: (b, zr0 + i, z_colblk)),
            pl.BlockSpec((1, 1, 1, HY_WIDTH), lambda b, i: (l, order, 0, 0)),
        ],
        out_specs=pl.BlockSpec((1, tmi, HY_WIDTH), lambda b, i: (b, i, 0)),
        compiler_params=_cp("parallel", "parallel"),
        name="hy_inv",
    )(p.reshape(BATCH, 2 * L, HY_WIDTH), gmat, u, z, flt_bias)


def _merge_kernel(h_ref, sg_ref, hyl_ref, hyc_ref, attl_ref, attc_ref, mod_ref, g_ref,
                  who_ref, wmo_ref, wout_ref, wr_ref, h1_ref, u2_ref, lg_ref):
    is_ctx = pl.program_id(1) >= N_LAT_TILES
    hy = jnp.where(is_ctx, hyc_ref[0], hyl_ref[0]).astype(BF16)
    att = jnp.where(is_ctx, attc_ref[0], attl_ref[0])
    sg = sg_ref[0].astype(F32)
    m = sg[:, :D_MODEL] * _dot(hy, who_ref[0]) + sg[:, D_MODEL:] * _dot(att, wmo_ref[0])
    y = _dot(m.astype(BF16), wout_ref[0])
    g1 = mod_ref[0, 0, 0, 2:3, :]
    sh2 = mod_ref[0, 0, 0, 3:4, :]
    sc2 = mod_ref[0, 0, 0, 4:5, :]
    h1 = h_ref[0] + g1 * y
    h1_ref[0] = h1
    u2 = _rms(h1, g_ref[0]) * (1.0 + sc2) + sh2
    u2_ref[0] = u2
    lg_ref[0] = _dot3(u2, wr_ref[0])


def _merge(l, n_tiles, h, sg, hy_lat, hy_ctx, att_lat, att_ctx, mods, norm2_g, who, wmo, wout, wr):
    rows = n_tiles * TM
    tile = lambda w: pl.BlockSpec((1, TM, w), lambda b, i: (b, i, 0))
    lat = lambda w: pl.BlockSpec((1, TM, w), lambda b, i: (b, jnp.minimum(i, N_LAT_TILES - 1), 0))
    ctx = lambda w: pl.BlockSpec((1, TM, w), lambda b, i: (b, 0, 0))
    wfull = lambda a: pl.BlockSpec((1,) + a.shape[1:], lambda b, i: (l, 0, 0))
    return pl.pallas_call(
        _merge_kernel,
        out_shape=(
            jax.ShapeDtypeStruct((BATCH, rows, D_MODEL), F32),
            jax.ShapeDtypeStruct((BATCH, rows, D_MODEL), F32),
            jax.ShapeDtypeStruct((BATCH, rows, ROUTER_PAD), F32),
        ),
        grid=(BATCH, n_tiles),
        in_specs=[
            tile(D_MODEL), tile(2 * D_MODEL),
            lat(HY_WIDTH), ctx(HY_WIDTH), lat(MLA_HEADS * V_DIM), ctx(MLA_HEADS * V_DIM),
            pl.BlockSpec((1, 1, 1, N_MOD, D_MODEL), lambda b, i: (l, b, i // N_LAT_TILES, 0, 0)),
            pl.BlockSpec((1, 1, D_MODEL), lambda b, i: (l, 0, 0)),
            wfull(who), wfull(wmo), wfull(wout), wfull(wr),
        ],
        out_specs=(tile(D_MODEL), tile(D_MODEL), tile(ROUTER_PAD)),
        compiler_params=_cp("parallel", "parallel"),
        name="merge",
    )(h, sg, hy_lat, hy_ctx, att_lat, att_ctx, mods, norm2_g, who, wmo, wout, wr)


def _expert_kernel(be_ref, nr_ref, sa_ref, u_hbm, wt_ref, wg_ref, wu_ref, wd_ref, y_hbm,
                   xbuf, ybuf, gsem, ssem):
    i = pl.program_id(0)
    nb = pl.num_programs(0)
    slot = i % 2

    def in_copy(blk, sl, r):
        tok = lax.shift_right_logical(sa_ref[blk * MOE_BM + r], 1)
        return pltpu.make_async_copy(u_hbm.at[pl.ds(tok, 1)], xbuf.at[sl, pl.ds(r, 1)], gsem.at[sl])

    def out_copy(blk, sl, r):
        return pltpu.make_async_copy(ybuf.at[sl, pl.ds(r, 1)], y_hbm.at[pl.ds(sa_ref[blk * MOE_BM + r], 1)],
                                     ssem.at[sl])

    def for_rows(blk, fn):
        def body(r, carry):
            fn(r)
            return carry
        lax.fori_loop(0, nr_ref[blk], body, 0)

    @pl.when(i == 0)
    def _():
        xbuf[...] = jnp.zeros_like(xbuf)
        for_rows(0, lambda r: in_copy(0, 0, r).start())

    @pl.when(i + 1 < nb)
    def _():
        for_rows(i + 1, lambda r: in_copy(i + 1, 1 - slot, r).start())

    for_rows(i, lambda r: in_copy(i, slot, r).wait())

    @pl.when(i >= 2)
    def _():
        for_rows(i - 2, lambda r: out_copy(i - 2, slot, r).wait())

    @pl.when(nr_ref[i] > 0)
    def _():
        x = xbuf[slot].astype(BF16)
        hg = _dot(x, wg_ref[0, 0].astype(BF16))
        hu = _dot(x, wu_ref[0, 0].astype(BF16))
        hb = (hg * jax.nn.sigmoid(hg) * hu).astype(BF16)
        ybuf[slot] = _dot(hb, wd_ref[0, 0].astype(BF16)) * wt_ref[...]

    for_rows(i, lambda r: out_copy(i, slot, r).start())

    @pl.when(i == nb - 1)
    def _():
        for_rows(i - 1, lambda r: out_copy(i - 1, 1 - slot, r).wait())
        for_rows(i, lambda r: out_copy(i, slot, r).wait())


def _experts(l, block_expert, block_rows, slot_assign, u, slot_w, w_gate_e, w_up_e, w_down_e):
    n_blocks = slot_assign.shape[0] // MOE_BM
    wspec = lambda r, c: pl.BlockSpec((1, 1, r, c), lambda i, be, nr, sa: (l, be[i], 0, 0))
    grid_spec = pltpu.PrefetchScalarGridSpec(
        num_scalar_prefetch=3,
        grid=(n_blocks,),
        in_specs=[
            pl.BlockSpec(memory_space=pl.ANY),
            pl.BlockSpec((MOE_BM, 1), lambda i, be, nr, sa: (i, 0)),
            wspec(D_MODEL, D_EXPERT), wspec(D_MODEL, D_EXPERT), wspec(D_EXPERT, D_MODEL),
        ],
        out_specs=pl.BlockSpec(memory_space=pl.ANY),
        scratch_shapes=[
            pltpu.VMEM((2, MOE_BM, D_MODEL), F32),
            pltpu.VMEM((2, MOE_BM, D_MODEL), F32),
            pltpu.SemaphoreType.DMA((2,)),
            pltpu.SemaphoreType.DMA((2,)),
        ],
    )
    return pl.pallas_call(
        _expert_kernel,
        out_shape=jax.ShapeDtypeStruct((TOP_K * u.shape[0], D_MODEL), F32),
        grid_spec=grid_spec,
        compiler_params=_cp("arbitrary"),
        name="experts",
    )(block_expert, block_rows, slot_assign, u, slot_w, w_gate_e, w_up_e, w_down_e)


def _route(logits, b_rg, b_re):
    T = logits.shape[0]
    lg = logits[:, :N_GROUPS]
    le = logits[:, N_GROUPS:N_GROUPS + N_EXPERTS].reshape(T, N_GROUPS, EXP_PER_GROUP)
    g_sel = jnp.argmax(lg + b_rg, axis=-1)
    p_g = jnp.take_along_axis(jax.nn.softmax(lg, axis=-1), g_sel[:, None], axis=-1)
    le_g = jnp.take_along_axis(le, g_sel[:, None, None], axis=1)[:, 0]
    be_g = b_re.reshape(N_GROUPS, EXP_PER_GROUP)[g_sel]
    _, local = lax.top_k(le_g + be_g, TOP_K)
    w_local = jax.nn.softmax(jnp.take_along_axis(le_g, local, axis=-1), axis=-1)
    expert = (g_sel[:, None] * EXP_PER_GROUP + local).astype(jnp.int32)
    return expert.reshape(-1), (p_g * w_local).reshape(-1)


def _dispatch(expert, weight):
    A = expert.shape[0]
    order = jnp.argsort(expert).astype(jnp.int32)
    e_sorted = expert[order]
    counts = jnp.zeros((N_EXPERTS,), jnp.int32).at[expert].add(1)
    padded = (counts + MOE_BM - 1) // MOE_BM * MOE_BM
    start = jnp.cumsum(counts) - counts
    p_end = jnp.cumsum(padded)
    p_start = p_end - padded
    dest = p_start[e_sorted] + jnp.arange(A, dtype=jnp.int32) - start[e_sorted]
    n_blocks = -(-A // MOE_BM) + N_EXPERTS
    slots = n_blocks * MOE_BM
    slot_assign = jnp.zeros((slots,), jnp.int32).at[dest].set(order)
    slot_weight = jnp.zeros((slots,), F32).at[dest].set(weight[order])
    block_row0 = jnp.arange(n_blocks, dtype=jnp.int32) * MOE_BM
    block_expert = jnp.minimum(jnp.searchsorted(p_end, block_row0, side="right"),
                               N_EXPERTS - 1).astype(jnp.int32)
    block_rows = jnp.clip(counts[block_expert] - (block_row0 - p_start[block_expert]), 0, MOE_BM)
    return slot_assign, slot_weight, block_expert, block_rows.astype(jnp.int32)


def _moe_sum(y_ref):
    y = y_ref[0]
    return y[:, :D_MODEL] + y[:, D_MODEL:]


def _combine_kernel(h_ref, y_ref, mod_ref, o_ref):
    o_ref[0] = h_ref[0] + mod_ref[0, 0, 0, 5:6, :] * _moe_sum(y_ref)


def _final_kernel(h_ref, y_ref, mod_ref, g_ref, o_ref):
    h2 = h_ref[0] + mod_ref[0, 0, 0, 5:6, :] * _moe_sum(y_ref)
    o_ref[0] = _rms(h2, g_ref[...])


def _combine(l, n_tiles, h1, y, mods, final_g=None):
    rows = n_tiles * TM
    tile = pl.BlockSpec((1, TM, D_MODEL), lambda b, i: (b, i, 0))
    in_specs = [tile, pl.BlockSpec((1, TM, TOP_K * D_MODEL), lambda b, i: (b, i, 0)),
                pl.BlockSpec((1, 1, 1, N_MOD, D_MODEL), lambda b, i: (l, b, i // N_LAT_TILES, 0, 0))]
    args = [h1, y, mods]
    kern = _combine_kernel
    if final_g is not None:
        in_specs.append(pl.BlockSpec((1, D_MODEL), lambda b, i: (0, 0)))
        args.append(final_g.reshape(1, D_MODEL))
        kern = _final_kernel
    return pl.pallas_call(
        kern,
        out_shape=jax.ShapeDtypeStruct((BATCH, rows, D_MODEL), F32),
        grid=(BATCH, n_tiles),
        in_specs=in_specs,
        out_specs=tile,
        compiler_params=_cp("parallel", "parallel"),
        name="combine",
    )(*args)


def _rope_tables():
    n = jnp.arange(SEQ)
    pos = jnp.stack([n // GRID_W, n % GRID_W], axis=-1).astype(F32)
    inv = ROPE_BASE ** (-jnp.arange(ROPE_PAIRS, dtype=F32) / ROPE_PAIRS)
    ang = pos[:, :, None] * inv
    cos, sin = jnp.cos(ang), jnp.sin(ang)
    cos32 = jnp.stack([cos, cos], axis=2).reshape(SEQ, QK_ROPE)
    sin32 = jnp.stack([-sin, sin], axis=2).reshape(SEQ, QK_ROPE)
    one = jnp.ones((SEQ, QK_ROPE), F32)
    zero = jnp.zeros((SEQ, QK_ROPE), F32)
    cosq = jnp.concatenate([jnp.ones((SEQ, QK_NOPE), F32), cos32, one], axis=1) * Q_SCALE
    sinq = jnp.concatenate([jnp.zeros((SEQ, QK_NOPE), F32), sin32, zero], axis=1) * Q_SCALE
    cosk = jnp.concatenate([jnp.zeros((SEQ, QK_NOPE), F32), cos32, zero], axis=1)
    sink = jnp.concatenate([jnp.zeros((SEQ, QK_NOPE), F32), sin32, zero], axis=1)
    lat = jnp.stack([cosq, sinq, cosk, sink])
    ctx_row = jnp.ones((HEAD_PAD,), F32) * Q_SCALE
    ctx_cosk = jnp.concatenate([jnp.zeros((QK_NOPE + QK_ROPE,), F32), jnp.ones((QK_ROPE,), F32)])
    ctx = jnp.stack([ctx_row, jnp.zeros_like(ctx_row), ctx_cosk, jnp.zeros_like(ctx_row)])
    ctx = jnp.broadcast_to(ctx[:, None, :], (4, CTX_LEN, HEAD_PAD))
    return jnp.concatenate([lat, ctx], axis=1)


def _dft_matrices(L):
    N = 2 * L
    k = jnp.arange(L, dtype=jnp.int32)[:, None]
    t = jnp.arange(L, dtype=jnp.int32)[None, :]
    ang = ((k * t) % N).astype(F32) * (2.0 * math.pi / N)
    c = jnp.cos(ang)
    s = -jnp.sin(ang)
    alt = jnp.where((t % 2) == 0, 1.0, -1.0).astype(F32)
    s = jnp.where(k == 0, alt, s)
    f = jnp.stack([c.reshape(L // FB, FB, L), s.reshape(L // FB, FB, L)], axis=1)
    wk = jnp.where(k == 0, 1.0 / N, 2.0 / N).astype(F32)
    g = jnp.stack([(c * wk).reshape(L // FB, FB, L), (s * wk).reshape(L // FB, FB, L)], axis=1)
    g = g.reshape(2 * L, L).T
    return f.astype(BF16), g.astype(BF16)


def _filter_embedding(L):
    t = jnp.linspace(0.0, 1.0, L, dtype=F32)[:, None]
    w = 2.0 * math.pi * jnp.arange(L, dtype=F32)[:, None] / L
    bands = jnp.linspace(1e-4, HY_BANDS - 1, HY_BANDS, dtype=F32)[None, :]
    return jnp.concatenate([t, jnp.cos(bands * w), -jnp.sin(bands * w),
                            jnp.zeros((L, LANES - HY_EMB), F32)], axis=-1)


def _rope_swap_perm():
    idx = np.arange(QK_ROPE)
    axis, half, pair = idx // (2 * ROPE_PAIRS), (idx // ROPE_PAIRS) % 2, idx % ROPE_PAIRS
    return axis * 2 * ROPE_PAIRS + (1 - half) * ROPE_PAIRS + pair


def _prep_weights(w_in, w_uq, w_ukv, w_hy_o, w_mla_o, w_out, w_router_g, w_router_e):
    perm = _rope_swap_perm()
    c_q = HY_COLS
    c_kv = HY_COLS + Q_LORA
    c_kr = c_kv + KV_LORA
    w_kr = w_in[:, :, c_kr:MLA_END]
    zeros = lambda n: jnp.zeros((DEPTH, D_MODEL, n), F32)
    wkr = jnp.concatenate([zeros(QK_NOPE), w_kr, w_kr,
                           zeros(QK_NOPE), w_kr[:, :, perm], zeros(QK_ROPE)], axis=-1)
    uq = w_uq.reshape(DEPTH, Q_LORA, MLA_HEADS, QK_NOPE + QK_ROPE)
    uq_r = uq[..., QK_NOPE:]
    wuq = jnp.concatenate([uq, uq_r], axis=-1).reshape(DEPTH, Q_LORA, MLA_HEADS * HEAD_PAD)
    wuqs = jnp.concatenate([jnp.zeros_like(uq[..., :QK_NOPE]), uq_r[..., perm], jnp.zeros_like(uq_r)],
                           axis=-1).reshape(DEPTH, Q_LORA, MLA_HEADS * HEAD_PAD)
    ukv = w_ukv.reshape(DEPTH, KV_LORA, MLA_HEADS, QK_NOPE + V_DIM)
    zpad = jnp.zeros_like(ukv[..., :HEAD_PAD - QK_NOPE])
    wuk = jnp.concatenate([ukv[..., :QK_NOPE], zpad], axis=-1).reshape(DEPTH, KV_LORA, MLA_HEADS * HEAD_PAD)
    wuv = jnp.concatenate([ukv[..., QK_NOPE:], zpad], axis=-1).reshape(DEPTH, KV_LORA, MLA_HEADS * HEAD_PAD)
    wr = jnp.concatenate([w_router_g, w_router_e,
                          jnp.zeros((DEPTH, D_MODEL, ROUTER_PAD - N_GROUPS - N_EXPERTS), F32)], axis=-1)
    bf = lambda a: a.astype(BF16)
    return dict(
        why=bf(w_in[:, :, :HY_COLS]), wgt=bf(w_in[:, :, MLA_END:]),
        wq=bf(w_in[:, :, c_q:c_kv]), wkv=bf(w_in[:, :, c_kv:c_kr]), wkr=bf(wkr),
        wuq=bf(wuq), wuqs=bf(wuqs), wuk=bf(wuk), wuv=bf(wuv),
        who=bf(w_hy_o), wmo=bf(w_mla_o), wout=bf(w_out), wr=wr,
    )


def _hyena(l, L, row0, u, fmat, gmat, zemb, deltas, flt, flt_bias):
    hsd, nyq = _filters(l, L, zemb, deltas, flt)
    spec = _spectrum(L, fmat, hsd, nyq)
    rb = row0 // L
    p = _hy_fwd(L, u, rb, 0, fmat, spec, 0)
    z1 = _hy_inv(l, L, p, gmat, u, row0, 1, u, row0, 0, flt_bias, 0)
    p = _hy_fwd(L, z1, 0, 0, fmat, spec, 1)
    return _hy_inv(l, L, p, gmat, u, row0, 2, z1, 0, 0, flt_bias, 1)


def kernel(x, c, ctx, c_ctx, w_ada, b_ada, norm1_g, w_in, q_norm_g, kv_norm_g, w_uq, w_ukv,
           sc_w, sc_b, flt_w1, flt_b1, flt_freq, flt_w2, flt_b2, flt_w3, flt_bias,
           w_hy_o, w_mla_o, w_out, norm2_g, w_router_g, b_router_g, w_router_e, b_router_e,
           w_gate_e, w_up_e, w_down_e, final_g):
    wts = _prep_weights(w_in, w_uq, w_ukv, w_hy_o, w_mla_o, w_out, w_router_g, w_router_e)
    tabs = _rope_tables()
    f_lat, g_lat = _dft_matrices(SEQ)
    f_ctx, g_ctx = _dft_matrices(CTX_LEN)
    z_lat, z_ctx = _filter_embedding(SEQ), _filter_embedding(CTX_LEN)
    deltas = jnp.linspace(math.log(HY_TARGET) / HY_FAST_DECAY, math.log(HY_TARGET) / HY_SLOW_DECAY,
                          HY_WIDTH, dtype=F32)[None, :]

    cc = jnp.concatenate([c, c_ctx[None, :], jnp.zeros((3, D_MODEL), F32)], axis=0)
    mod = _ada(cc, w_ada, b_ada).reshape(DEPTH, 8, N_MOD, D_MODEL)
    mods = jnp.stack([mod[:, :BATCH], jnp.broadcast_to(mod[:, BATCH:BATCH + 1], (DEPTH, BATCH, N_MOD, D_MODEL))],
                     axis=2)

    r3 = lambda a: a.reshape(DEPTH, 1, a.shape[-1])
    norm1, norm2, qg, kvg = r3(norm1_g), r3(norm2_g), r3(q_norm_g), r3(kv_norm_g)
    scb = r3(sc_b)
    fb1, ffr, fb2 = r3(flt_b1), r3(flt_freq), r3(flt_b2)
    fbias = flt_bias.reshape(DEPTH, HY_ORDER, 1, HY_WIDTH)
    w1_pad = jnp.concatenate([flt_w1, jnp.zeros((DEPTH, LANES - HY_EMB, HY_FFN), F32)], axis=1)
    flt = (w1_pad, fb1, ffr, flt_w2, fb2, flt_w3)

    h = jnp.concatenate([x, ctx], axis=1)
    for l in range(DEPTH):
        last = l == DEPTH - 1
        n_tiles = N_LAT_TILES if last else N_ALL_TILES
        p_hy, sg, q, k, v = _inproj(l, h, mods, norm1, tabs, qg, kvg, wts)

        u = _short_conv(l, p_hy, sc_w, scb)
        att_lat = _attention(q, k, v, latent=True)
        hy_lat = _hyena(l, SEQ, 0, u, f_lat, g_lat, z_lat, deltas, flt, fbias)
        if last:
            att_ctx, hy_ctx = att_lat, hy_lat
        else:
            att_ctx = _attention(q, k, v, latent=False)
            hy_ctx = _hyena(l, CTX_LEN, SEQ, u, f_ctx, g_ctx, z_ctx, deltas, flt, fbias)

        h1, u2, logits = _merge(l, n_tiles, h, sg, hy_lat, hy_ctx, att_lat, att_ctx, mods, norm2,
                                wts["who"], wts["wmo"], wts["wout"], wts["wr"])

        T = BATCH * n_tiles * TM
        expert, weight = _route(logits.reshape(T, ROUTER_PAD), b_router_g[l], b_router_e[l])
        slot_assign, slot_weight, block_expert, block_rows = _dispatch(expert, weight)
        y = _experts(l, block_expert, block_rows, slot_assign, u2.reshape(T, D_MODEL), slot_weight[:, None],
                     w_gate_e, w_up_e, w_down_e)
        h = _combine(l, n_tiles, h1, y.reshape(BATCH, n_tiles * TM, TOP_K * D_MODEL), mods,
                     final_g if last else None)
    return h
```

```python
import functools
import math

import jax
import jax.numpy as jnp
import numpy as np
from jax import lax
from jax.experimental import pallas as pl
from jax.experimental.pallas import tpu as pltpu

F32 = jnp.float32
BF16 = jnp.bfloat16

D_MODEL = 1024
BATCH = 4
SEQ = 4096
DEPTH = 4
GRID_W = 64
CTX_LEN = 256
S_ALL = SEQ + CTX_LEN
EPS = 1e-6
N_MOD = 6

HY_WIDTH = 512
HY_ORDER = 2
HY_BANDS = 16
HY_EMB = 1 + 2 * HY_BANDS
HY_FFN = 64
SHORT_K = 3
HY_FAST_DECAY = 0.3
HY_SLOW_DECAY = 1.5
HY_TARGET = 1e-2

MLA_HEADS = 8
QK_NOPE = 64
QK_ROPE = 32
V_DIM = 64
Q_LORA = 384
KV_LORA = 256
ROPE_PAIRS = QK_ROPE // 4
ROPE_BASE = 10000.0
ATTN_SCALE = (QK_NOPE + QK_ROPE) ** -0.5
Q_SCALE = ATTN_SCALE * math.log2(math.e)

N_GROUPS = 8
EXP_PER_GROUP = 8
N_EXPERTS = N_GROUPS * EXP_PER_GROUP
TOP_K = 2
D_EXPERT = 256

HY_COLS = (HY_ORDER + 1) * HY_WIDTH
MLA_END = HY_COLS + Q_LORA + KV_LORA + QK_ROPE

LANES = 128
HEAD_PAD = LANES
VMEM_LIMIT = 56 << 20

TM = 256
N_LAT_TILES = SEQ // TM
N_ALL_TILES = S_ALL // TM
TQ = 1024
TK = 512
FB = 256
MOE_BM = 256
DMA_UNROLL = 8
ROUTER_PAD = LANES


def _cp(*sem):
    return pltpu.CompilerParams(dimension_semantics=sem, vmem_limit_bytes=VMEM_LIMIT)


def _dot(a, b):
    return jnp.dot(a, b, preferred_element_type=F32)


def _split(a):
    hi = a.astype(BF16)
    lo = (a - hi.astype(F32)).astype(BF16)
    return hi, lo


def _dot3(a, b):
    ah, al = _split(a)
    bh, bl = _split(b)
    return _dot(ah, bh) + (_dot(ah, bl) + _dot(al, bh))


def _rms(x, g):
    return x * lax.rsqrt(jnp.mean(x * x, axis=-1, keepdims=True) + EPS) * g


def _ada_kernel(c_ref, w_ref, b_ref, o_ref):
    cc = c_ref[...]
    s = cc * jax.nn.sigmoid(cc)
    o_ref[0] = _dot3(s, w_ref[0]) + b_ref[0]


def _ada(cc, w_ada, b_ada):
    tn = 1536
    n = N_MOD * D_MODEL
    return pl.pallas_call(
        _ada_kernel,
        out_shape=jax.ShapeDtypeStruct((DEPTH, 8, n), F32),
        grid=(DEPTH, n // tn),
        in_specs=[
            pl.BlockSpec((8, D_MODEL), lambda l, j: (0, 0)),
            pl.BlockSpec((1, D_MODEL, tn), lambda l, j: (l, 0, j)),
            pl.BlockSpec((1, 1, tn), lambda l, j: (l, 0, j)),
        ],
        out_specs=pl.BlockSpec((1, 8, tn), lambda l, j: (l, 0, j)),
        compiler_params=_cp("parallel", "parallel"),
        name="ada",
    )(cc, w_ada, b_ada.reshape(DEPTH, 1, n))


def _inproj_kernel(h_ref, mod_ref, g_ref, tab_ref, qg_ref, kvg_ref,
                   why_ref, wgt_ref, wq_ref, wkv_ref, wkr_ref, wuq_ref, wuqs_ref, wuk_ref, wuv_ref,
                   phy_ref, sg_ref, q_ref, k_ref, v_ref):
    h = h_ref[0]
    shift = mod_ref[0, 0, 0, 0:1, :]
    scale = mod_ref[0, 0, 0, 1:2, :]
    u = (_rms(h, g_ref[0]) * (1.0 + scale) + shift).astype(BF16)

    phy_ref[0] = _dot(u, why_ref[0])
    sg_ref[0] = jax.nn.sigmoid(_dot(u, wgt_ref[0])).astype(BF16)

    cosq, sinq, cosk, sink = tab_ref[0], tab_ref[1], tab_ref[2], tab_ref[3]
    cq = _rms(_dot(u, wq_ref[0]), qg_ref[0]).astype(BF16)
    qa = _dot(cq, wuq_ref[0])
    qs = _dot(cq, wuqs_ref[0])
    ckv = _rms(_dot(u, wkv_ref[0]), kvg_ref[0]).astype(BF16)
    ka = _dot(ckv, wuk_ref[0])
    va = _dot(ckv, wuv_ref[0])
    kr = _dot(u, wkr_ref[0])
    krk = kr[:, :HEAD_PAD] * cosk + kr[:, HEAD_PAD:] * sink
    ones_col = (lax.broadcasted_iota(jnp.int32, (1, HEAD_PAD), 1) == V_DIM).astype(F32)
    for hh in range(MLA_HEADS):
        sl = slice(hh * HEAD_PAD, (hh + 1) * HEAD_PAD)
        q_ref[0, hh] = (qa[:, sl] * cosq + qs[:, sl] * sinq).astype(BF16)
        k_ref[0, hh] = (ka[:, sl] + krk).astype(BF16)
        v_ref[0, hh] = (va[:, sl] + ones_col).astype(BF16)


def _inproj(l, h, mods, norm1_g, tabs, q_norm_g, kv_norm_g, wts):
    names = ("why", "wgt", "wq", "wkv", "wkr", "wuq", "wuqs", "wuk", "wuv")
    w_specs = [pl.BlockSpec((1,) + wts[n].shape[1:], lambda b, i: (l, 0, 0)) for n in names]
    hd = MLA_HEADS * HEAD_PAD
    qkv_shape = jax.ShapeDtypeStruct((BATCH, MLA_HEADS, S_ALL, HEAD_PAD), BF16)
    qkv_spec = pl.BlockSpec((1, MLA_HEADS, TM, HEAD_PAD), lambda b, i: (b, 0, i, 0))
    del hd
    return pl.pallas_call(
        _inproj_kernel,
        out_shape=(
            jax.ShapeDtypeStruct((BATCH, S_ALL, HY_COLS), F32),
            jax.ShapeDtypeStruct((BATCH, S_ALL, 2 * D_MODEL), BF16),
            qkv_shape, qkv_shape, qkv_shape,
        ),
        grid=(BATCH, N_ALL_TILES),
        in_specs=[
            pl.BlockSpec((1, TM, D_MODEL), lambda b, i: (b, i, 0)),
            pl.BlockSpec((1, 1, 1, N_MOD, D_MODEL), lambda b, i: (l, b, i // N_LAT_TILES, 0, 0)),
            pl.BlockSpec((1, 1, D_MODEL), lambda b, i: (l, 0, 0)),
            pl.BlockSpec((4, TM, HEAD_PAD), lambda b, i: (0, i, 0)),
            pl.BlockSpec((1, 1, Q_LORA), lambda b, i: (l, 0, 0)),
            pl.BlockSpec((1, 1, KV_LORA), lambda b, i: (l, 0, 0)),
        ] + w_specs,
        out_specs=(
            pl.BlockSpec((1, TM, HY_COLS), lambda b, i: (b, i, 0)),
            pl.BlockSpec((1, TM, 2 * D_MODEL), lambda b, i: (b, i, 0)),
            qkv_spec, qkv_spec, qkv_spec,
        ),
        compiler_params=_cp("parallel", "parallel"),
        name="inproj",
    )(h, mods, norm1_g, tabs, q_norm_g, kv_norm_g, *[wts[n] for n in names])


def _flash_step(q_ref, k_ref, v_ref, rows, carry):
    out = []
    for hh in range(2):
        m, acc = carry[hh]
        s = lax.dot_general(q_ref[0, hh], k_ref[0, hh, rows, :], (((1,), (1,)), ((), ())),
                            preferred_element_type=F32)
        m_new = jnp.maximum(m, jnp.max(s, axis=-1, keepdims=True))
        p = jnp.exp2(s - m_new)
        acc = acc * jnp.exp2(m - m_new) + _dot(p.astype(BF16), v_ref[0, hh, rows, :])
        out.append((m_new, acc))
    return tuple(out)


def _attn_kernel(q_ref, k_ref, v_ref, o_ref, *, n_full, tk, tail):
    tq = q_ref.shape[2]
    init = (jnp.full((tq, 1), -1e30, F32), jnp.zeros((tq, HEAD_PAD), F32))
    carry = (init, init)
    if n_full:
        def body(j, carry):
            return _flash_step(q_ref, k_ref, v_ref, pl.ds(pl.multiple_of(j * tk, tk), tk), carry)
        carry = lax.fori_loop(0, n_full, body, carry)
    if tail:
        carry = _flash_step(q_ref, k_ref, v_ref, pl.ds(n_full * tk, tail), carry)
    outs = [acc[:, :V_DIM] / acc[:, V_DIM:V_DIM + 1] for _, acc in carry]
    o_ref[0] = jnp.concatenate(outs, axis=-1).astype(BF16)


def _attention(q, k, v, latent):
    if latent:
        tq, nq, q0, kv_rows, kv_blk = TQ, SEQ // TQ, 0, S_ALL, 0
        kern = functools.partial(_attn_kernel, n_full=SEQ // TK, tk=TK, tail=CTX_LEN)
    else:
        tq, nq, q0, kv_rows, kv_blk = CTX_LEN, 1, SEQ // CTX_LEN, CTX_LEN, SEQ // CTX_LEN
        kern = functools.partial(_attn_kernel, n_full=0, tk=TK, tail=CTX_LEN)
    return pl.pallas_call(
        kern,
        out_shape=jax.ShapeDtypeStruct((BATCH, nq * tq, MLA_HEADS * V_DIM), BF16),
        grid=(BATCH, MLA_HEADS // 2, nq),
        in_specs=[
            pl.BlockSpec((1, 2, tq, HEAD_PAD), lambda b, hp, i: (b, hp, q0 + i, 0)),
            pl.BlockSpec((1, 2, kv_rows, HEAD_PAD), lambda b, hp, i: (b, hp, kv_blk, 0)),
            pl.BlockSpec((1, 2, kv_rows, HEAD_PAD), lambda b, hp, i: (b, hp, kv_blk, 0)),
        ],
        out_specs=pl.BlockSpec((1, tq, 2 * V_DIM), lambda b, hp, i: (b, i, hp)),
        compiler_params=_cp("parallel", "parallel", "parallel"),
        name="attn_lat" if latent else "attn_ctx",
    )(q, k, v)


def _short_conv_kernel(p_ref, w_ref, b_ref, o_ref):
    x = p_ref[0]
    rows = lax.broadcasted_iota(jnp.int32, (S_ALL, 1), 0)
    first = (rows == 0) | (rows == SEQ)
    last = (rows == SEQ - 1) | (rows == S_ALL - 1)
    prev = jnp.where(first, 0.0, pltpu.roll(x, 1, 0))
    nxt = jnp.where(last, 0.0, pltpu.roll(x, S_ALL - 1, 0))
    w = w_ref[0]
    o_ref[0] = b_ref[0] + prev * w[0:1] + x * w[1:2] + nxt * w[2:3]


def _short_conv(l, p_hy, sc_w, sc_b):
    cb = LANES
    return pl.pallas_call(
        _short_conv_kernel,
        out_shape=jax.ShapeDtypeStruct(p_hy.shape, F32),
        grid=(BATCH, HY_COLS // cb),
        in_specs=[
            pl.BlockSpec((1, S_ALL, cb), lambda b, j: (b, 0, j)),
            pl.BlockSpec((1, SHORT_K, cb), lambda b, j: (l, 0, j)),
            pl.BlockSpec((1, 1, cb), lambda b, j: (l, 0, j)),
        ],
        out_specs=pl.BlockSpec((1, S_ALL, cb), lambda b, j: (b, 0, j)),
        compiler_params=_cp("parallel", "parallel"),
        name="short_conv",
    )(p_hy, sc_w, sc_b)


def _filter_kernel(z_ref, w1_ref, b1_ref, fr_ref, w2_ref, b2_ref, w3f_ref, w3b_ref, dl_ref, hsd_ref, nyq_ref):
    z = z_ref[...]
    fr = fr_ref[0]
    a = jnp.sin(fr * (_dot3(z, w1_ref[0]) + b1_ref[0]))
    a = jnp.sin(fr * (_dot3(a, w2_ref[0]) + b2_ref[0]))
    decay = jnp.exp(-z[:, 0:1] * jnp.abs(dl_ref[...]))
    rows = lax.broadcasted_iota(jnp.int32, (z.shape[0], 1), 0)

    def one_direction(w3_ref):
        h = _dot3(a, w3_ref[0]) * decay
        return h * lax.rsqrt(jnp.sum(h * h, axis=0, keepdims=True) + EPS)

    hf = one_direction(w3f_ref)
    hb = jnp.where(rows == 0, 0.0, one_direction(w3b_ref))
    hs = hf + hb
    hsd_ref[0] = hs.astype(BF16)
    hsd_ref[1] = (hf - hb).astype(BF16)
    sign = jnp.where((rows & 1) == 0, 1.0, -1.0)
    nyq_ref[0] = jnp.sum(hs * sign, axis=0, keepdims=True)


def _filters(l, L, zemb, deltas, flt):
    w1, b1, fr, w2, b2, w3 = flt
    ncb = HY_WIDTH // LANES
    full = lambda *shape: pl.BlockSpec((1,) + shape, lambda o, j: (l,) + (0,) * len(shape))
    return pl.pallas_call(
        _filter_kernel,
        out_shape=(
            jax.ShapeDtypeStruct((2, L, HY_ORDER * HY_WIDTH), BF16),
            jax.ShapeDtypeStruct((HY_ORDER, 1, HY_WIDTH), F32),
        ),
        grid=(HY_ORDER, ncb),
        in_specs=[
            pl.BlockSpec((L, LANES), lambda o, j: (0, 0)),
            full(LANES, HY_FFN), full(1, HY_FFN), full(1, HY_FFN),
            full(HY_FFN, HY_FFN), full(1, HY_FFN),
            pl.BlockSpec((1, HY_FFN, LANES), lambda o, j: (l, 0, o * 2 * ncb + j)),
            pl.BlockSpec((1, HY_FFN, LANES), lambda o, j: (l, 0, o * 2 * ncb + ncb + j)),
            pl.BlockSpec((1, LANES), lambda o, j: (0, j)),
        ],
        out_specs=(
            pl.BlockSpec((2, L, LANES), lambda o, j: (0, 0, o * ncb + j)),
            pl.BlockSpec((1, 1, LANES), lambda o, j: (o, 0, j)),
        ),
        compiler_params=_cp("parallel", "parallel"),
        name="hy_filter",
    )(zemb, w1, b1, fr, w2, b2, w3, w3, deltas)


def _spec_kernel(f_ref, h_ref, nyq_ref, o_ref):
    i, ri = pl.program_id(0), pl.program_id(1)
    out = _dot(f_ref[0, 0], h_ref[0])
    rows = lax.broadcasted_iota(jnp.int32, (FB, 1), 0)
    packed = (rows == 0) & (i == 0) & (ri == 1)
    o_ref[0, 0] = jnp.where(packed, nyq_ref[0], out)


def _spectrum(L, fmat, hsd, nyq):
    nfb = L // FB
    ncol = HY_ORDER * HY_WIDTH
    return pl.pallas_call(
        _spec_kernel,
        out_shape=jax.ShapeDtypeStruct((nfb, 2, FB, ncol), F32),
        grid=(nfb, 2, HY_ORDER),
        in_specs=[
            pl.BlockSpec((1, 1, FB, L), lambda i, ri, o: (i, ri, 0, 0)),
            pl.BlockSpec((1, L, HY_WIDTH), lambda i, ri, o: (ri, 0, o)),
            pl.BlockSpec((1, 1, HY_WIDTH), lambda i, ri, o: (o, 0, 0)),
        ],
        out_specs=pl.BlockSpec((1, 1, FB, HY_WIDTH), lambda i, ri, o: (i, ri, 0, o)),
        compiler_params=_cp("parallel", "parallel", "parallel"),
        name="hy_spectrum",
    )(fmat, hsd, nyq)


def _fwd_kernel(z_ref, f_ref, h_ref, p_ref, zb_ref):
    i = pl.program_id(1)

    @pl.when(i == 0)
    def _():
        zb_ref[...] = z_ref[0].astype(BF16)

    zb = zb_ref[...]
    zr = _dot(f_ref[0, 0], zb)
    zi = _dot(f_ref[0, 1], zb)
    hr, hi = h_ref[0, 0], h_ref[0, 1]
    rows = lax.broadcasted_iota(jnp.int32, (FB, 1), 0)
    packed = (rows == 0) & (i == 0)
    zihi = zi * hi
    p_ref[0, 0, 0] = (zr * hr - jnp.where(packed, 0.0, zihi)).astype(BF16)
    p_ref[0, 0, 1] = jnp.where(packed, zihi, zr * hi + zi * hr).astype(BF16)


def _hy_fwd(L, z, z_rowblk, z_colblk, fmat, spec, order):
    nfb = L // FB
    return pl.pallas_call(
        _fwd_kernel,
        out_shape=jax.ShapeDtypeStruct((BATCH, nfb, 2, FB, HY_WIDTH), BF16),
        grid=(BATCH, nfb),
        in_specs=[
            pl.BlockSpec((1, L, HY_WIDTH), lambda b, i: (b, z_rowblk, z_colblk)),
            pl.BlockSpec((1, 2, FB, L), lambda b, i: (i, 0, 0, 0)),
            pl.BlockSpec((1, 2, FB, HY_WIDTH), lambda b, i: (i, 0, 0, order)),
        ],
        out_specs=pl.BlockSpec((1, 1, 2, FB, HY_WIDTH), lambda b, i: (b, i, 0, 0, 0)),
        scratch_shapes=[pltpu.VMEM((L, HY_WIDTH), BF16)],
        compiler_params=_cp("parallel", "arbitrary"),
        name="hy_fwd",
    )(z, fmat, spec)


def _inv_kernel(p_ref, g_ref, gate_ref, z_ref, bias_ref, o_ref):
    y = _dot(g_ref[...], p_ref[0])
    o_ref[0] = gate_ref[0] * (y + bias_ref[0, 0] * z_ref[0])


def _hy_inv(l, L, p, gmat, u, row0, gate_colblk, z, z_row0, z_colblk, flt_bias, order):
    tmi = TM
    r0, zr0 = row0 // tmi, z_row0 // tmi
    return pl.pallas_call(
        _inv_kernel,
        out_shape=jax.ShapeDtypeStruct((BATCH, L, HY_WIDTH), F32),
        grid=(BATCH, L // tmi),
        in_specs=[
            pl.BlockSpec((1, 2 * L, HY_WIDTH), lambda b, i: (b, 0, 0)),
            pl.BlockSpec((tmi, 2 * L), lambda b, i: (i, 0)),
            pl.BlockSpec((1, tmi, HY_WIDTH), lambda b, i: (b, r0 + i, gate_colblk)),
            pl.BlockSpec((1, tmi, HY_WIDTH), lambda b, i: (b, zr0 + i, z_colblk)),
            pl.BlockSpec((1, 1, 1, HY_WIDTH), lambda b, i: (l, order, 0, 0)),
        ],
        out_specs=pl.BlockSpec((1, tmi, HY_WIDTH), lambda b, i: (b, i, 0)),
        compiler_params=_cp("parallel", "parallel"),
        name="hy_inv",
    )(p.reshape(BATCH, 2 * L, HY_WIDTH), gmat, u, z, flt_bias)


def _route_tile(lg, bias):
    lane = lax.broadcasted_iota(jnp.int32, lg.shape, 1)
    lane_f = lane.astype(F32)
    neg = jnp.float32(-jnp.inf)
    big = jnp.float32(ROUTER_PAD)
    biased = lg + bias

    def first_argmax(v):
        m = jnp.max(v, axis=-1, keepdims=True)
        return jnp.min(jnp.where(v == m, lane_f, big), axis=-1, keepdims=True).astype(jnp.int32)

    def pick(v, idx):
        return jnp.sum(jnp.where(lane == idx, v, 0.0), axis=-1, keepdims=True)

    is_group = lane < N_GROUPS
    g_sel = first_argmax(jnp.where(is_group, biased, neg))
    raw_g = jnp.where(is_group, lg, neg)
    e_g = jnp.exp(raw_g - jnp.max(raw_g, axis=-1, keepdims=True))
    p_g = pick(e_g, g_sel) / jnp.sum(e_g, axis=-1, keepdims=True)

    lo = N_GROUPS + g_sel * EXP_PER_GROUP
    cand = jnp.where((lane >= lo) & (lane < lo + EXP_PER_GROUP), biased, neg)
    i1 = first_argmax(cand)
    i2 = first_argmax(jnp.where(lane == i1, neg, cand))
    l1, l2 = pick(lg, i1), pick(lg, i2)
    top = jnp.maximum(l1, l2)
    e1, e2 = jnp.exp(l1 - top), jnp.exp(l2 - top)
    inv = p_g / (e1 + e2)
    return (i1 - N_GROUPS, i2 - N_GROUPS), (e1 * inv, e2 * inv)


def _merge_kernel(h_ref, sg_ref, hyl_ref, hyc_ref, attl_ref, attc_ref, mod_ref, g_ref,
                  who_ref, wmo_ref, wout_ref, wr_ref, br_ref, h1_ref, u2_ref, eid_ref, ew_ref):
    is_ctx = pl.program_id(1) >= N_LAT_TILES
    hy = jnp.where(is_ctx, hyc_ref[0], hyl_ref[0]).astype(BF16)
    att = jnp.where(is_ctx, attc_ref[0], attl_ref[0])
    sg = sg_ref[0].astype(F32)
    m = sg[:, :D_MODEL] * _dot(hy, who_ref[0]) + sg[:, D_MODEL:] * _dot(att, wmo_ref[0])
    y = _dot(m.astype(BF16), wout_ref[0])
    g1 = mod_ref[0, 0, 0, 2:3, :]
    sh2 = mod_ref[0, 0, 0, 3:4, :]
    sc2 = mod_ref[0, 0, 0, 4:5, :]
    h1 = h_ref[0] + g1 * y
    h1_ref[0] = h1
    u2 = _rms(h1, g_ref[0]) * (1.0 + sc2) + sh2
    u2_ref[0] = u2
    ids, ws = _route_tile(_dot3(u2, wr_ref[0]), br_ref[0])
    for c in range(TOP_K):
        eid_ref[0, :, c:c + 1] = ids[c]
        ew_ref[0, :, c:c + 1] = ws[c]


def _merge(l, n_tiles, h, sg, hy_lat, hy_ctx, att_lat, att_ctx, mods, norm2_g, who, wmo, wout, wr, br):
    rows = n_tiles * TM
    tile = lambda w: pl.BlockSpec((1, TM, w), lambda b, i: (b, i, 0))
    lat = lambda w: pl.BlockSpec((1, TM, w), lambda b, i: (b, jnp.minimum(i, N_LAT_TILES - 1), 0))
    ctx = lambda w: pl.BlockSpec((1, TM, w), lambda b, i: (b, 0, 0))
    wfull = lambda a: pl.BlockSpec((1,) + a.shape[1:], lambda b, i: (l, 0, 0))
    return pl.pallas_call(
        _merge_kernel,
        out_shape=(
            jax.ShapeDtypeStruct((BATCH, rows, D_MODEL), F32),
            jax.ShapeDtypeStruct((BATCH, rows, D_MODEL), F32),
            jax.ShapeDtypeStruct((BATCH, rows, TOP_K), jnp.int32),
            jax.ShapeDtypeStruct((BATCH, rows, TOP_K), F32),
        ),
        grid=(BATCH, n_tiles),
        in_specs=[
            tile(D_MODEL), tile(2 * D_MODEL),
            lat(HY_WIDTH), ctx(HY_WIDTH), lat(MLA_HEADS * V_DIM), ctx(MLA_HEADS * V_DIM),
            pl.BlockSpec((1, 1, 1, N_MOD, D_MODEL), lambda b, i: (l, b, i // N_LAT_TILES, 0, 0)),
            pl.BlockSpec((1, 1, D_MODEL), lambda b, i: (l, 0, 0)),
            wfull(who), wfull(wmo), wfull(wout), wfull(wr), wfull(br),
        ],
        out_specs=(tile(D_MODEL), tile(D_MODEL), tile(TOP_K), tile(TOP_K)),
        compiler_params=_cp("parallel", "parallel"),
        name="merge",
    )(h, sg, hy_lat, hy_ctx, att_lat, att_ctx, mods, norm2_g, who, wmo, wout, wr, br)


def _expert_kernel(be_ref, b0_ref, nr_ref, ord_ref, u_hbm, wg_ref, wu_ref, wd_ref, y_hbm,
                   xbuf, ybuf, gsem, ssem):
    i = pl.program_id(0)
    nb = pl.num_programs(0)
    slot = i % 2

    def in_copy(blk, sl, r):
        tok = lax.shift_right_logical(ord_ref[b0_ref[blk] + r], 1)
        return pltpu.make_async_copy(u_hbm.at[pl.ds(tok, 1)], xbuf.at[sl, pl.ds(r, 1)], gsem.at[sl])

    def out_copy(blk, sl, r):
        return pltpu.make_async_copy(ybuf.at[sl, pl.ds(r, 1)], y_hbm.at[pl.ds(ord_ref[b0_ref[blk] + r], 1)],
                                     ssem.at[sl])

    def for_rows(blk, fn):
        n = nr_ref[blk]
        n_groups = n // DMA_UNROLL

        def group(g, carry):
            for j in range(DMA_UNROLL):
                fn(g * DMA_UNROLL + j)
            return carry

        def single(r, carry):
            fn(r)
            return carry

        lax.fori_loop(0, n_groups, group, 0)
        lax.fori_loop(n_groups * DMA_UNROLL, n, single, 0)

    @pl.when(i == 0)
    def _():
        xbuf[...] = jnp.zeros_like(xbuf)
        for_rows(0, lambda r: in_copy(0, 0, r).start())

    @pl.when(i + 1 < nb)
    def _():
        for_rows(i + 1, lambda r: in_copy(i + 1, 1 - slot, r).start())

    for_rows(i, lambda r: in_copy(i, slot, r).wait())

    @pl.when(i >= 2)
    def _():
        for_rows(i - 2, lambda r: out_copy(i - 2, slot, r).wait())

    @pl.when(nr_ref[i] > 0)
    def _():
        x = xbuf[slot].astype(BF16)
        hg = _dot(x, wg_ref[0, 0].astype(BF16))
        hu = _dot(x, wu_ref[0, 0].astype(BF16))
        hb = (hg * jax.nn.sigmoid(hg) * hu).astype(BF16)
        ybuf[slot] = _dot(hb, wd_ref[0, 0].astype(BF16))

    for_rows(i, lambda r: out_copy(i, slot, r).start())

    @pl.when(i == nb - 1)
    def _():
        for_rows(i - 1, lambda r: out_copy(i - 1, 1 - slot, r).wait())
        for_rows(i, lambda r: out_copy(i, slot, r).wait())


def _experts(l, block_expert, block_row0, block_rows, order, u, w_gate_e, w_up_e, w_down_e):
    n_blocks = block_expert.shape[0]
    wspec = lambda r, c: pl.BlockSpec((1, 1, r, c), lambda i, be, b0, nr, od: (l, be[i], 0, 0))
    grid_spec = pltpu.PrefetchScalarGridSpec(
        num_scalar_prefetch=4,
        grid=(n_blocks,),
        in_specs=[
            pl.BlockSpec(memory_space=pl.ANY),
            wspec(D_MODEL, D_EXPERT), wspec(D_MODEL, D_EXPERT), wspec(D_EXPERT, D_MODEL),
        ],
        out_specs=pl.BlockSpec(memory_space=pl.ANY),
        scratch_shapes=[
            pltpu.VMEM((2, MOE_BM, D_MODEL), F32),
            pltpu.VMEM((2, MOE_BM, D_MODEL), F32),
            pltpu.SemaphoreType.DMA((2,)),
            pltpu.SemaphoreType.DMA((2,)),
        ],
    )
    return pl.pallas_call(
        _expert_kernel,
        out_shape=jax.ShapeDtypeStruct((TOP_K * u.shape[0], D_MODEL), F32),
        grid_spec=grid_spec,
        compiler_params=_cp("arbitrary"),
        name="experts",
    )(block_expert, block_row0, block_rows, order, u, w_gate_e, w_up_e, w_down_e)


def _dispatch(expert):
    A = expert.shape[0]
    e_sorted, order = lax.sort((expert, jnp.arange(A, dtype=jnp.int32)), num_keys=1)
    start = jnp.searchsorted(e_sorted, jnp.arange(N_EXPERTS + 1, dtype=jnp.int32), side="left").astype(jnp.int32)
    counts = start[1:] - start[:-1]
    n_blk = (counts + MOE_BM - 1) // MOE_BM
    blk_end = jnp.cumsum(n_blk)
    n_blocks = -(-A // MOE_BM) + N_EXPERTS
    b = jnp.arange(n_blocks, dtype=jnp.int32)
    block_expert = jnp.minimum(jnp.searchsorted(blk_end, b, side="right"), N_EXPERTS - 1).astype(jnp.int32)
    row_in_expert = (b - (blk_end[block_expert] - n_blk[block_expert])) * MOE_BM
    block_rows = jnp.clip(counts[block_expert] - row_in_expert, 0, MOE_BM).astype(jnp.int32)
    block_row0 = jnp.where(block_rows > 0, start[block_expert] + row_in_expert, 0).astype(jnp.int32)
    return order, block_expert, block_row0, block_rows


def _moe_sum(y_ref, w_ref):
    y, w = y_ref[0], w_ref[0]
    return w[:, 0:1] * y[:, :D_MODEL] + w[:, 1:2] * y[:, D_MODEL:]


def _combine_kernel(h_ref, y_ref, w_ref, mod_ref, o_ref):
    o_ref[0] = h_ref[0] + mod_ref[0, 0, 0, 5:6, :] * _moe_sum(y_ref, w_ref)


def _final_kernel(h_ref, y_ref, w_ref, mod_ref, g_ref, o_ref):
    h2 = h_ref[0] + mod_ref[0, 0, 0, 5:6, :] * _moe_sum(y_ref, w_ref)
    o_ref[0] = _rms(h2, g_ref[...])


def _combine(l, n_tiles, h1, y, w, mods, final_g=None):
    rows = n_tiles * TM
    tile = pl.BlockSpec((1, TM, D_MODEL), lambda b, i: (b, i, 0))
    in_specs = [tile, pl.BlockSpec((1, TM, TOP_K * D_MODEL), lambda b, i: (b, i, 0)),
                pl.BlockSpec((1, TM, TOP_K), lambda b, i: (b, i, 0)),
                pl.BlockSpec((1, 1, 1, N_MOD, D_MODEL), lambda b, i: (l, b, i // N_LAT_TILES, 0, 0))]
    args = [h1, y, w, mods]
    kern = _combine_kernel
    if final_g is not None:
        in_specs.append(pl.BlockSpec((1, D_MODEL), lambda b, i: (0, 0)))
        args.append(final_g.reshape(1, D_MODEL))
        kern = _final_kernel
    return pl.pallas_call(
        kern,
        out_shape=jax.ShapeDtypeStruct((BATCH, rows, D_MODEL), F32),
        grid=(BATCH, n_tiles),
        in_specs=in_specs,
        out_specs=tile,
        compiler_params=_cp("parallel", "parallel"),
        name="combine",
    )(*args)


def _rope_tables():
    n = jnp.arange(SEQ)
    pos = jnp.stack([n // GRID_W, n % GRID_W], axis=-1).astype(F32)
    inv = ROPE_BASE ** (-jnp.arange(ROPE_PAIRS, dtype=F32) / ROPE_PAIRS)
    ang = pos[:, :, None] * inv
    cos, sin = jnp.cos(ang), jnp.sin(ang)
    cos32 = jnp.stack([cos, cos], axis=2).reshape(SEQ, QK_ROPE)
    sin32 = jnp.stack([-sin, sin], axis=2).reshape(SEQ, QK_ROPE)
    one = jnp.ones((SEQ, QK_ROPE), F32)
    zero = jnp.zeros((SEQ, QK_ROPE), F32)
    cosq = jnp.concatenate([jnp.ones((SEQ, QK_NOPE), F32), cos32, one], axis=1) * Q_SCALE
    sinq = jnp.concatenate([jnp.zeros((SEQ, QK_NOPE), F32), sin32, zero], axis=1) * Q_SCALE
    cosk = jnp.concatenate([jnp.zeros((SEQ, QK_NOPE), F32), cos32, zero], axis=1)
    sink = jnp.concatenate([jnp.zeros((SEQ, QK_NOPE), F32), sin32, zero], axis=1)
    lat = jnp.stack([cosq, sinq, cosk, sink])
    ctx_row = jnp.ones((HEAD_PAD,), F32) * Q_SCALE
    ctx_cosk = jnp.concatenate([jnp.zeros((QK_NOPE + QK_ROPE,), F32), jnp.ones((QK_ROPE,), F32)])
    ctx = jnp.stack([ctx_row, jnp.zeros_like(ctx_row), ctx_cosk, jnp.zeros_like(ctx_row)])
    ctx = jnp.broadcast_to(ctx[:, None, :], (4, CTX_LEN, HEAD_PAD))
    return jnp.concatenate([lat, ctx], axis=1)


def _dft_matrices(L):
    N = 2 * L
    nfb = L // FB
    t = jnp.arange(L, dtype=jnp.int32)

    def tables(n_rows, step):
        ang = ((jnp.arange(n_rows, dtype=jnp.int32)[:, None] * step * t[None, :]) % N).astype(F32)
        ang = ang * (2.0 * math.pi / N)
        return jnp.cos(ang), jnp.sin(ang)

    ca, sa = tables(nfb, FB)
    cb, sb = tables(FB, 1)
    alt = jnp.where((t % 2) == 0, 1.0, -1.0).astype(F32)
    i_idx = jnp.arange(nfb)[:, None, None]
    r_idx = jnp.arange(FB)[None, :, None]
    dc = (i_idx == 0) & (r_idx == 0)
    c = ca[:, None, :] * cb[None] - sa[:, None, :] * sb[None]
    s = jnp.where(dc, alt, -(sa[:, None, :] * cb[None] + ca[:, None, :] * sb[None]))
    f = jnp.stack([c, s], axis=1).astype(BF16)
    wk = jnp.where(dc, 1.0 / N, 2.0 / N).astype(F32)[..., 0]
    cat, sat, cbt, sbt = ca.T[:, :, None], sa.T[:, :, None], cb.T[:, None, :], sb.T[:, None, :]
    ct = (cat * cbt - sat * sbt) * wk
    st = jnp.where(dc[..., 0], alt[:, None, None], -(sat * cbt + cat * sbt)) * wk
    g = jnp.stack([ct, st], axis=2).reshape(L, 2 * L).astype(BF16)
    return f, g


def _filter_embedding(L):
    t = jnp.linspace(0.0, 1.0, L, dtype=F32)[:, None]
    w = 2.0 * math.pi * jnp.arange(L, dtype=F32)[:, None] / L
    bands = jnp.linspace(1e-4, HY_BANDS - 1, HY_BANDS, dtype=F32)[None, :]
    return jnp.concatenate([t, jnp.cos(bands * w), -jnp.sin(bands * w),
                            jnp.zeros((L, LANES - HY_EMB), F32)], axis=-1)


def _rope_swap_perm():
    idx = np.arange(QK_ROPE)
    axis, half, pair = idx // (2 * ROPE_PAIRS), (idx // ROPE_PAIRS) % 2, idx % ROPE_PAIRS
    return axis * 2 * ROPE_PAIRS + (1 - half) * ROPE_PAIRS + pair


def _prep_weights(w_in, w_uq, w_ukv, w_hy_o, w_mla_o, w_out, w_router_g, w_router_e):
    perm = _rope_swap_perm()
    c_q = HY_COLS
    c_kv = HY_COLS + Q_LORA
    c_kr = c_kv + KV_LORA
    w_kr = w_in[:, :, c_kr:MLA_END]
    zeros = lambda n: jnp.zeros((DEPTH, D_MODEL, n), F32)
    wkr = jnp.concatenate([zeros(QK_NOPE), w_kr, w_kr,
                           zeros(QK_NOPE), w_kr[:, :, perm], zeros(QK_ROPE)], axis=-1)
    uq = w_uq.reshape(DEPTH, Q_LORA, MLA_HEADS, QK_NOPE + QK_ROPE)
    uq_r = uq[..., QK_NOPE:]
    wuq = jnp.concatenate([uq, uq_r], axis=-1).reshape(DEPTH, Q_LORA, MLA_HEADS * HEAD_PAD)
    wuqs = jnp.concatenate([jnp.zeros_like(uq[..., :QK_NOPE]), uq_r[..., perm], jnp.zeros_like(uq_r)],
                           axis=-1).reshape(DEPTH, Q_LORA, MLA_HEADS * HEAD_PAD)
    ukv = w_ukv.reshape(DEPTH, KV_LORA, MLA_HEADS, QK_NOPE + V_DIM)
    zpad = jnp.zeros_like(ukv[..., :HEAD_PAD - QK_NOPE])
    wuk = jnp.concatenate([ukv[..., :QK_NOPE], zpad], axis=-1).reshape(DEPTH, KV_LORA, MLA_HEADS * HEAD_PAD)
    wuv = jnp.concatenate([ukv[..., QK_NOPE:], zpad], axis=-1).reshape(DEPTH, KV_LORA, MLA_HEADS * HEAD_PAD)
    wr = jnp.concatenate([w_router_g, w_router_e,
                          jnp.zeros((DEPTH, D_MODEL, ROUTER_PAD - N_GROUPS - N_EXPERTS), F32)], axis=-1)
    bf = lambda a: a.astype(BF16)
    return dict(
        why=bf(w_in[:, :, :HY_COLS]), wgt=bf(w_in[:, :, MLA_END:]),
        wq=bf(w_in[:, :, c_q:c_kv]), wkv=bf(w_in[:, :, c_kv:c_kr]), wkr=bf(wkr),
        wuq=bf(wuq), wuqs=bf(wuqs), wuk=bf(wuk), wuv=bf(wuv),
        who=bf(w_hy_o), wmo=bf(w_mla_o), wout=bf(w_out), wr=wr,
    )


def _hyena(l, L, row0, u, fmat, gmat, zemb, deltas, flt, flt_bias):
    hsd, nyq = _filters(l, L, zemb, deltas, flt)
    spec = _spectrum(L, fmat, hsd, nyq)
    rb = row0 // L
    p = _hy_fwd(L, u, rb, 0, fmat, spec, 0)
    z1 = _hy_inv(l, L, p, gmat, u, row0, 1, u, row0, 0, flt_bias, 0)
    p = _hy_fwd(L, z1, 0, 0, fmat, spec, 1)
    return _hy_inv(l, L, p, gmat, u, row0, 2, z1, 0, 0, flt_bias, 1)


def kernel(x, c, ctx, c_ctx, w_ada, b_ada, norm1_g, w_in, q_norm_g, kv_norm_g, w_uq, w_ukv,
           sc_w, sc_b, flt_w1, flt_b1, flt_freq, flt_w2, flt_b2, flt_w3, flt_bias,
           w_hy_o, w_mla_o, w_out, norm2_g, w_router_g, b_router_g, w_router_e, b_router_e,
           w_gate_e, w_up_e, w_down_e, final_g):
    wts = _prep_weights(w_in, w_uq, w_ukv, w_hy_o, w_mla_o, w_out, w_router_g, w_router_e)
    tabs = _rope_tables()
    f_lat, g_lat = _dft_matrices(SEQ)
    f_ctx, g_ctx = _dft_matrices(CTX_LEN)
    z_lat, z_ctx = _filter_embedding(SEQ), _filter_embedding(CTX_LEN)
    deltas = jnp.linspace(math.log(HY_TARGET) / HY_FAST_DECAY, math.log(HY_TARGET) / HY_SLOW_DECAY,
                          HY_WIDTH, dtype=F32)[None, :]

    cc = jnp.concatenate([c, c_ctx[None, :], jnp.zeros((3, D_MODEL), F32)], axis=0)
    mod = _ada(cc, w_ada, b_ada).reshape(DEPTH, 8, N_MOD, D_MODEL)
    mods = jnp.stack([mod[:, :BATCH], jnp.broadcast_to(mod[:, BATCH:BATCH + 1], (DEPTH, BATCH, N_MOD, D_MODEL))],
                     axis=2)

    r3 = lambda a: a.reshape(DEPTH, 1, a.shape[-1])
    norm1, norm2, qg, kvg = r3(norm1_g), r3(norm2_g), r3(q_norm_g), r3(kv_norm_g)
    scb = r3(sc_b)
    fb1, ffr, fb2 = r3(flt_b1), r3(flt_freq), r3(flt_b2)
    fbias = flt_bias.reshape(DEPTH, HY_ORDER, 1, HY_WIDTH)
    w1_pad = jnp.concatenate([flt_w1, jnp.zeros((DEPTH, LANES - HY_EMB, HY_FFN), F32)], axis=1)
    flt = (w1_pad, fb1, ffr, flt_w2, fb2, flt_w3)

    b_router = jnp.concatenate([b_router_g, b_router_e,
                                jnp.zeros((DEPTH, ROUTER_PAD - N_GROUPS - N_EXPERTS), F32)], axis=-1)
    b_router = b_router.reshape(DEPTH, 1, ROUTER_PAD)

    h = jnp.concatenate([x, ctx], axis=1)
    for l in range(DEPTH):
        last = l == DEPTH - 1
        n_tiles = N_LAT_TILES if last else N_ALL_TILES
        p_hy, sg, q, k, v = _inproj(l, h, mods, norm1, tabs, qg, kvg, wts)

        u = _short_conv(l, p_hy, sc_w, scb)
        att_lat = _attention(q, k, v, latent=True)
        hy_lat = _hyena(l, SEQ, 0, u, f_lat, g_lat, z_lat, deltas, flt, fbias)
        if last:
            att_ctx, hy_ctx = att_lat, hy_lat
        else:
            att_ctx = _attention(q, k, v, latent=False)
            hy_ctx = _hyena(l, CTX_LEN, SEQ, u, f_ctx, g_ctx, z_ctx, deltas, flt, fbias)

        h1, u2, eid, ew = _merge(l, n_tiles, h, sg, hy_lat, hy_ctx, att_lat, att_ctx, mods, norm2,
                                 wts["who"], wts["wmo"], wts["wout"], wts["wr"], b_router)

        T = BATCH * n_tiles * TM
        order, block_expert, block_row0, block_rows = _dispatch(eid.reshape(T * TOP_K))
        y = _experts(l, block_expert, block_row0, block_rows, order, u2.reshape(T, D_MODEL),
                     w_gate_e, w_up_e, w_down_e)
        h = _combine(l, n_tiles, h1, y.reshape(BATCH, n_tiles * TM, TOP_K * D_MODEL), ew, mods,
                     final_g if last else None)
    return h
```

```python
import functools
import math

import jax
import jax.numpy as jnp
import numpy as np
from jax import lax
from jax.experimental import pallas as pl
from jax.experimental.pallas import tpu as pltpu

F32 = jnp.float32
BF16 = jnp.bfloat16

D_MODEL = 1024
BATCH = 4
SEQ = 4096
DEPTH = 4
GRID_W = 64
CTX_LEN = 256
S_ALL = SEQ + CTX_LEN
EPS = 1e-6
N_MOD = 6

HY_WIDTH = 512
HY_ORDER = 2
HY_BANDS = 16
HY_EMB = 1 + 2 * HY_BANDS
HY_FFN = 64
SHORT_K = 3
HY_FAST_DECAY = 0.3
HY_SLOW_DECAY = 1.5
HY_TARGET = 1e-2

MLA_HEADS = 8
QK_NOPE = 64
QK_ROPE = 32
V_DIM = 64
Q_LORA = 384
KV_LORA = 256
ROPE_PAIRS = QK_ROPE // 4
ROPE_BASE = 10000.0
ATTN_SCALE = (QK_NOPE + QK_ROPE) ** -0.5
Q_SCALE = ATTN_SCALE * math.log2(math.e)

N_GROUPS = 8
EXP_PER_GROUP = 8
N_EXPERTS = N_GROUPS * EXP_PER_GROUP
TOP_K = 2
D_EXPERT = 256

HY_COLS = (HY_ORDER + 1) * HY_WIDTH
MLA_END = HY_COLS + Q_LORA + KV_LORA + QK_ROPE

LANES = 128
HEAD_PAD = LANES
VMEM_LIMIT = 56 << 20

TM = 256
N_LAT_TILES = SEQ // TM
N_ALL_TILES = S_ALL // TM
TQ = 1024
TK = 512
FB = 256
HY_SEG = 1024
MOE_BM = 256
DMA_UNROLL = 8
ROUTER_PAD = LANES


def _cp(*sem):
    return pltpu.CompilerParams(dimension_semantics=sem, vmem_limit_bytes=VMEM_LIMIT)


def _dot(a, b):
    return jnp.dot(a, b, preferred_element_type=F32)


def _split(a):
    hi = a.astype(BF16)
    lo = (a - hi.astype(F32)).astype(BF16)
    return hi, lo


def _dot3(a, b):
    ah, al = _split(a)
    bh, bl = _split(b)
    return _dot(ah, bh) + (_dot(ah, bl) + _dot(al, bh))


def _rms(x, g):
    return x * lax.rsqrt(jnp.mean(x * x, axis=-1, keepdims=True) + EPS) * g


def _ada_kernel(c_ref, w_ref, b_ref, o_ref):
    cc = c_ref[...]
    s = cc * jax.nn.sigmoid(cc)
    o_ref[0] = _dot3(s, w_ref[0]) + b_ref[0]


def _ada(cc, w_ada, b_ada):
    tn = 1536
    n = N_MOD * D_MODEL
    return pl.pallas_call(
        _ada_kernel,
        out_shape=jax.ShapeDtypeStruct((DEPTH, 8, n), F32),
        grid=(DEPTH, n // tn),
        in_specs=[
            pl.BlockSpec((8, D_MODEL), lambda l, j: (0, 0)),
            pl.BlockSpec((1, D_MODEL, tn), lambda l, j: (l, 0, j)),
            pl.BlockSpec((1, 1, tn), lambda l, j: (l, 0, j)),
        ],
        out_specs=pl.BlockSpec((1, 8, tn), lambda l, j: (l, 0, j)),
        compiler_params=_cp("parallel", "parallel"),
        name="ada",
    )(cc, w_ada, b_ada.reshape(DEPTH, 1, n))


def _inproj_kernel(h_ref, mod_ref, g_ref, tab_ref, qg_ref, kvg_ref,
                   why_ref, wgt_ref, wq_ref, wkv_ref, wkr_ref, wuq_ref, wuqs_ref, wuk_ref, wuv_ref,
                   phy_ref, sg_ref, q_ref, k_ref, v_ref):
    h = h_ref[0]
    shift = mod_ref[0, 0, 0, 0:1, :]
    scale = mod_ref[0, 0, 0, 1:2, :]
    u = (_rms(h, g_ref[0]) * (1.0 + scale) + shift).astype(BF16)

    phy_ref[0] = _dot(u, why_ref[0])
    sg_ref[0] = jax.nn.sigmoid(_dot(u, wgt_ref[0])).astype(BF16)

    cosq, sinq, cosk, sink = tab_ref[0], tab_ref[1], tab_ref[2], tab_ref[3]
    cq = _rms(_dot(u, wq_ref[0]), qg_ref[0]).astype(BF16)
    qa = _dot(cq, wuq_ref[0])
    qs = _dot(cq, wuqs_ref[0])
    ckv = _rms(_dot(u, wkv_ref[0]), kvg_ref[0]).astype(BF16)
    ka = _dot(ckv, wuk_ref[0])
    va = _dot(ckv, wuv_ref[0])
    kr = _dot(u, wkr_ref[0])
    krk = kr[:, :HEAD_PAD] * cosk + kr[:, HEAD_PAD:] * sink
    ones_col = (lax.broadcasted_iota(jnp.int32, (1, HEAD_PAD), 1) == V_DIM).astype(F32)
    for hh in range(MLA_HEADS):
        sl = slice(hh * HEAD_PAD, (hh + 1) * HEAD_PAD)
        q_ref[0, hh] = (qa[:, sl] * cosq + qs[:, sl] * sinq).astype(BF16)
        k_ref[0, hh] = (ka[:, sl] + krk).astype(BF16)
        v_ref[0, hh] = (va[:, sl] + ones_col).astype(BF16)


def _inproj(l, h, mods, norm1_g, tabs, q_norm_g, kv_norm_g, wts):
    names = ("why", "wgt", "wq", "wkv", "wkr", "wuq", "wuqs", "wuk", "wuv")
    w_specs = [pl.BlockSpec((1,) + wts[n].shape[1:], lambda b, i: (l, 0, 0)) for n in names]
    hd = MLA_HEADS * HEAD_PAD
    qkv_shape = jax.ShapeDtypeStruct((BATCH, MLA_HEADS, S_ALL, HEAD_PAD), BF16)
    qkv_spec = pl.BlockSpec((1, MLA_HEADS, TM, HEAD_PAD), lambda b, i: (b, 0, i, 0))
    del hd
    return pl.pallas_call(
        _inproj_kernel,
        out_shape=(
            jax.ShapeDtypeStruct((BATCH, S_ALL, HY_COLS), F32),
            jax.ShapeDtypeStruct((BATCH, S_ALL, 2 * D_MODEL), BF16),
            qkv_shape, qkv_shape, qkv_shape,
        ),
        grid=(BATCH, N_ALL_TILES),
        in_specs=[
            pl.BlockSpec((1, TM, D_MODEL), lambda b, i: (b, i, 0)),
            pl.BlockSpec((1, 1, 1, N_MOD, D_MODEL), lambda b, i: (l, b, i // N_LAT_TILES, 0, 0)),
            pl.BlockSpec((1, 1, D_MODEL), lambda b, i: (l, 0, 0)),
            pl.BlockSpec((4, TM, HEAD_PAD), lambda b, i: (0, i, 0)),
            pl.BlockSpec((1, 1, Q_LORA), lambda b, i: (l, 0, 0)),
            pl.BlockSpec((1, 1, KV_LORA), lambda b, i: (l, 0, 0)),
        ] + w_specs,
        out_specs=(
            pl.BlockSpec((1, TM, HY_COLS), lambda b, i: (b, i, 0)),
            pl.BlockSpec((1, TM, 2 * D_MODEL), lambda b, i: (b, i, 0)),
            qkv_spec, qkv_spec, qkv_spec,
        ),
        compiler_params=_cp("parallel", "parallel"),
        name="inproj",
    )(h, mods, norm1_g, tabs, q_norm_g, kv_norm_g, *[wts[n] for n in names])


def _flash_step(q_ref, k_ref, v_ref, rows, carry):
    out = []
    for hh in range(2):
        m, acc = carry[hh]
        s = lax.dot_general(q_ref[0, hh], k_ref[0, hh, rows, :], (((1,), (1,)), ((), ())),
                            preferred_element_type=F32)
        m_new = jnp.maximum(m, jnp.max(s, axis=-1, keepdims=True))
        p = jnp.exp2(s - m_new)
        acc = acc * jnp.exp2(m - m_new) + _dot(p.astype(BF16), v_ref[0, hh, rows, :])
        out.append((m_new, acc))
    return tuple(out)


def _attn_kernel(q_ref, k_ref, v_ref, o_ref, *, n_full, tk, tail):
    tq = q_ref.shape[2]
    init = (jnp.full((tq, 1), -1e30, F32), jnp.zeros((tq, HEAD_PAD), F32))
    carry = (init, init)
    if n_full:
        def body(j, carry):
            return _flash_step(q_ref, k_ref, v_ref, pl.ds(pl.multiple_of(j * tk, tk), tk), carry)
        carry = lax.fori_loop(0, n_full, body, carry)
    if tail:
        carry = _flash_step(q_ref, k_ref, v_ref, pl.ds(n_full * tk, tail), carry)
    outs = [acc[:, :V_DIM] / acc[:, V_DIM:V_DIM + 1] for _, acc in carry]
    o_ref[0] = jnp.concatenate(outs, axis=-1).astype(BF16)


def _attention(q, k, v, latent):
    if latent:
        tq, nq, q0, kv_rows, kv_blk = TQ, SEQ // TQ, 0, S_ALL, 0
        kern = functools.partial(_attn_kernel, n_full=SEQ // TK, tk=TK, tail=CTX_LEN)
    else:
        tq, nq, q0, kv_rows, kv_blk = CTX_LEN, 1, SEQ // CTX_LEN, CTX_LEN, SEQ // CTX_LEN
        kern = functools.partial(_attn_kernel, n_full=0, tk=TK, tail=CTX_LEN)
    return pl.pallas_call(
        kern,
        out_shape=jax.ShapeDtypeStruct((BATCH, nq * tq, MLA_HEADS * V_DIM), BF16),
        grid=(BATCH, MLA_HEADS // 2, nq),
        in_specs=[
            pl.BlockSpec((1, 2, tq, HEAD_PAD), lambda b, hp, i: (b, hp, q0 + i, 0)),
            pl.BlockSpec((1, 2, kv_rows, HEAD_PAD), lambda b, hp, i: (b, hp, kv_blk, 0)),
            pl.BlockSpec((1, 2, kv_rows, HEAD_PAD), lambda b, hp, i: (b, hp, kv_blk, 0)),
        ],
        out_specs=pl.BlockSpec((1, tq, 2 * V_DIM), lambda b, hp, i: (b, i, hp)),
        compiler_params=_cp("parallel", "parallel", "parallel"),
        name="attn_lat" if latent else "attn_ctx",
    )(q, k, v)


def _short_conv_kernel(p_ref, w_ref, b_ref, o_ref):
    x = p_ref[0]
    rows = lax.broadcasted_iota(jnp.int32, (S_ALL, 1), 0)
    first = (rows == 0) | (rows == SEQ)
    last = (rows == SEQ - 1) | (rows == S_ALL - 1)
    prev = jnp.where(first, 0.0, pltpu.roll(x, 1, 0))
    nxt = jnp.where(last, 0.0, pltpu.roll(x, S_ALL - 1, 0))
    w = w_ref[0]
    o_ref[0] = b_ref[0] + prev * w[0:1] + x * w[1:2] + nxt * w[2:3]


def _short_conv(l, p_hy, sc_w, sc_b):
    cb = LANES
    return pl.pallas_call(
        _short_conv_kernel,
        out_shape=jax.ShapeDtypeStruct(p_hy.shape, F32),
        grid=(BATCH, HY_COLS // cb),
        in_specs=[
            pl.BlockSpec((1, S_ALL, cb), lambda b, j: (b, 0, j)),
            pl.BlockSpec((1, SHORT_K, cb), lambda b, j: (l, 0, j)),
            pl.BlockSpec((1, 1, cb), lambda b, j: (l, 0, j)),
        ],
        out_specs=pl.BlockSpec((1, S_ALL, cb), lambda b, j: (b, 0, j)),
        compiler_params=_cp("parallel", "parallel"),
        name="short_conv",
    )(p_hy, sc_w, sc_b)


def _filter_kernel(z_ref, w1_ref, b1_ref, fr_ref, w2_ref, b2_ref, w3f_ref, w3b_ref, dl_ref, hfb_ref, a_ref):
    z = z_ref[...]

    @pl.when((pl.program_id(0) == 0) & (pl.program_id(1) == 0))
    def _():
        fr = fr_ref[0]
        a1 = jnp.sin(fr * (_dot3(z, w1_ref[0]) + b1_ref[0]))
        a_ref[...] = jnp.sin(fr * (_dot3(a1, w2_ref[0]) + b2_ref[0]))

    a = a_ref[...]
    decay = jnp.exp(-z[:, 0:1] * jnp.abs(dl_ref[...]))
    rows = lax.broadcasted_iota(jnp.int32, (z.shape[0], 1), 0)

    def one_direction(w3_ref):
        h = _dot3(a, w3_ref[0]) * decay
        return h * lax.rsqrt(jnp.sum(h * h, axis=0, keepdims=True) + EPS)

    hfb_ref[0] = one_direction(w3f_ref).astype(BF16)
    hfb_ref[1] = jnp.where(rows == 0, 0.0, one_direction(w3b_ref)).astype(BF16)


def _filters(l, L, zemb, deltas, flt):
    w1, b1, fr, w2, b2, w3 = flt
    ncb = HY_WIDTH // LANES
    full = lambda *shape: pl.BlockSpec((1,) + shape, lambda o, j: (l,) + (0,) * len(shape))
    return pl.pallas_call(
        _filter_kernel,
        out_shape=jax.ShapeDtypeStruct((2, L, HY_ORDER * HY_WIDTH), BF16),
        grid=(HY_ORDER, ncb),
        in_specs=[
            pl.BlockSpec((L, LANES), lambda o, j: (0, 0)),
            full(LANES, HY_FFN), full(1, HY_FFN), full(1, HY_FFN),
            full(HY_FFN, HY_FFN), full(1, HY_FFN),
            pl.BlockSpec((1, HY_FFN, LANES), lambda o, j: (l, 0, o * 2 * ncb + j)),
            pl.BlockSpec((1, HY_FFN, LANES), lambda o, j: (l, 0, o * 2 * ncb + ncb + j)),
            pl.BlockSpec((1, LANES), lambda o, j: (0, j)),
        ],
        out_specs=pl.BlockSpec((2, L, LANES), lambda o, j: (0, 0, o * ncb + j)),
        scratch_shapes=[pltpu.VMEM((L, HY_FFN), F32)],
        compiler_params=_cp("arbitrary", "arbitrary"),
        name="hy_filter",
    )(zemb, w1, b1, fr, w2, b2, w3, w3, deltas)


def _packed_row0(block):
    return (lax.broadcasted_iota(jnp.int32, (FB, 1), 0) == 0) & (block == 0)


def _packed_mac(acc, c, z):
    (ar, ai), (cr, ci), (zr, zi) = acc, c, z
    return ar + (zr * cr - zi * ci), ai + (zr * ci + zi * cr)


def _spec_kernel(f_ref, h_ref, o_ref, *, seg, n_seg):
    fblk = pl.program_id(1)
    fr, fi = f_ref[0, 0], f_ref[0, 1]
    row0 = _packed_row0(fblk)
    sign = jnp.where((lax.broadcasted_iota(jnp.int32, (FB, 1), 0) & 1) == 0, 1.0, -1.0)

    def first_half(d, m):
        taps = h_ref[d, m * seg:(m + 1) * seg, :]
        return _dot(fr, taps), _dot(fi, taps), taps[0:1, :].astype(F32)

    def causal_block(d, m, halves):
        re, im, _ = halves[m]
        if m >= 1:
            pre, pim, tap0 = halves[m - 1]
            re = re + sign * (pre - tap0)
            im = im + sign * (pim - jnp.where(row0, tap0, 0.0))
        return re, im

    def conj(re, im):
        return re, jnp.where(row0, im, -im)

    halves = [[first_half(d, m) for m in range(n_seg)] for d in range(2)]
    for li in range(2 * n_seg - 1):
        lag = li - (n_seg - 1)
        if lag > 0:
            re, im = causal_block(0, lag, halves[0])
        elif lag < 0:
            re, im = conj(*causal_block(1, -lag, halves[1]))
        else:
            fre, fim = causal_block(0, 0, halves[0])
            gre, gim = conj(*causal_block(1, 0, halves[1]))
            re, im = fre + gre, fim + gim
        o_ref[0, 0, li, 0] = re
        o_ref[0, 0, li, 1] = im


def _spectrum(L, seg, fmat, hfb):
    n_seg, nfb = L // seg, seg // FB
    n_lag = 2 * n_seg - 1
    return pl.pallas_call(
        functools.partial(_spec_kernel, seg=seg, n_seg=n_seg),
        out_shape=jax.ShapeDtypeStruct((HY_ORDER, nfb, n_lag, 2, FB, HY_WIDTH), F32),
        grid=(HY_ORDER, nfb),
        in_specs=[
            pl.BlockSpec((1, 2, FB, seg), lambda o, f: (f, 0, 0, 0)),
            pl.BlockSpec((2, L, HY_WIDTH), lambda o, f: (0, 0, o)),
        ],
        out_specs=pl.BlockSpec((1, 1, n_lag, 2, FB, HY_WIDTH), lambda o, f: (o, f, 0, 0, 0, 0)),
        compiler_params=_cp("parallel", "parallel"),
        name="hy_spectrum",
    )(fmat, hfb)


def _fwd_kernel(z_ref, f_ref, c_ref, y_ref, zb_ref, *, seg, n_seg):
    fblk = pl.program_id(1)

    @pl.when(fblk == 0)
    def _():
        zb_ref[...] = z_ref[0].astype(BF16)

    fr, fi = f_ref[0, 0], f_ref[0, 1]
    row0 = _packed_row0(fblk)
    acc = [None] * n_seg
    edge = [None] * n_seg
    for j in range(n_seg):
        zj = zb_ref[j * seg:(j + 1) * seg, :]
        z = (_dot(fr, zj), _dot(fi, zj))
        for i in range(n_seg):
            c = (c_ref[0, 0, i - j + n_seg - 1, 0], c_ref[0, 0, i - j + n_seg - 1, 1])
            e = (z[0][0:1] * c[0][0:1], z[1][0:1] * c[1][0:1])
            if acc[i] is None:
                acc[i] = (z[0] * c[0] - z[1] * c[1], z[0] * c[1] + z[1] * c[0])
                edge[i] = e
            else:
                acc[i] = _packed_mac(acc[i], c, z)
                edge[i] = (edge[i][0] + e[0], edge[i][1] + e[1])
    for i in range(n_seg):
        y_ref[0, i, 0:FB] = jnp.where(row0, edge[i][0], acc[i][0]).astype(BF16)
        y_ref[0, i, FB:2 * FB] = jnp.where(row0, edge[i][1], acc[i][1]).astype(BF16)


def _hy_fwd(L, seg, z, z_rowblk, z_colblk, fmat, spec, order):
    n_seg, nfb = L // seg, seg // FB
    n_lag = 2 * n_seg - 1
    return pl.pallas_call(
        functools.partial(_fwd_kernel, seg=seg, n_seg=n_seg),
        out_shape=jax.ShapeDtypeStruct((BATCH, n_seg, 2 * seg, HY_WIDTH), BF16),
        grid=(BATCH, nfb),
        in_specs=[
            pl.BlockSpec((1, L, HY_WIDTH), lambda b, f: (b, z_rowblk, z_colblk)),
            pl.BlockSpec((1, 2, FB, seg), lambda b, f: (f, 0, 0, 0)),
            pl.BlockSpec((1, 1, n_lag, 2, FB, HY_WIDTH), lambda b, f: (order, f, 0, 0, 0, 0)),
        ],
        out_specs=pl.BlockSpec((1, n_seg, 2 * FB, HY_WIDTH), lambda b, f: (b, 0, f, 0)),
        scratch_shapes=[pltpu.VMEM((L, HY_WIDTH), BF16)],
        compiler_params=_cp("parallel", "arbitrary"),
        name="hy_fwd",
    )(z, fmat, spec)


def _inv_kernel(y_ref, g_ref, gate_ref, z_ref, bias_ref, o_ref):
    y = _dot(g_ref[...], y_ref[0, 0])
    o_ref[0] = gate_ref[0] * (y + bias_ref[0, 0] * z_ref[0])


def _hy_inv(l, L, seg, yf, gmat, u, row0, gate_colblk, z, z_row0, z_colblk, flt_bias, order):
    tmi = min(seg, 2 * TM)
    per_seg = seg // tmi
    r0, zr0 = row0 // tmi, z_row0 // tmi
    return pl.pallas_call(
        _inv_kernel,
        out_shape=jax.ShapeDtypeStruct((BATCH, L, HY_WIDTH), F32),
        grid=(BATCH, L // tmi),
        in_specs=[
            pl.BlockSpec((1, 1, 2 * seg, HY_WIDTH), lambda b, i: (b, i // per_seg, 0, 0)),
            pl.BlockSpec((tmi, 2 * seg), lambda b, i: (i % per_seg, 0)),
            pl.BlockSpec((1, tmi, HY_WIDTH), lambda b, i: (b, r0 + i, gate_colblk)),
            pl.BlockSpec((1, tmi, HY_WIDTH), lambda b, i: (b, zr0 + i, z_colblk)),
            pl.BlockSpec((1, 1, 1, HY_WIDTH), lambda b, i: (l, order, 0, 0)),
        ],
        out_specs=pl.BlockSpec((1, tmi, HY_WIDTH), lambda b, i: (b, i, 0)),
        compiler_params=_cp("parallel", "parallel"),
        name="hy_inv",
    )(yf, gmat, u, z, flt_bias)


def _route_tile(lg, bias):
    lane = lax.broadcasted_iota(jnp.int32, lg.shape, 1)
    lane_f = lane.astype(F32)
    neg = jnp.float32(-jnp.inf)
    big = jnp.float32(ROUTER_PAD)
    biased = lg + bias

    def first_argmax(v):
        m = jnp.max(v, axis=-1, keepdims=True)
        return jnp.min(jnp.where(v == m, lane_f, big), axis=-1, keepdims=True).astype(jnp.int32)

    def pick(v, idx):
        return jnp.sum(jnp.where(lane == idx, v, 0.0), axis=-1, keepdims=True)

    is_group = lane < N_GROUPS
    g_sel = first_argmax(jnp.where(is_group, biased, neg))
    raw_g = jnp.where(is_group, lg, neg)
    e_g = jnp.exp(raw_g - jnp.max(raw_g, axis=-1, keepdims=True))
    p_g = pick(e_g, g_sel) / jnp.sum(e_g, axis=-1, keepdims=True)

    lo = N_GROUPS + g_sel * EXP_PER_GROUP
    cand = jnp.where((lane >= lo) & (lane < lo + EXP_PER_GROUP), biased, neg)
    i1 = first_argmax(cand)
    i2 = first_argmax(jnp.where(lane == i1, neg, cand))
    l1, l2 = pick(lg, i1), pick(lg, i2)
    top = jnp.maximum(l1, l2)
    e1, e2 = jnp.exp(l1 - top), jnp.exp(l2 - top)
    inv = p_g / (e1 + e2)
    return (i1 - N_GROUPS, i2 - N_GROUPS), (e1 * inv, e2 * inv)


def _merge_kernel(h_ref, sg_ref, hyl_ref, hyc_ref, attl_ref, attc_ref, mod_ref, g_ref,
                  who_ref, wmo_ref, wout_ref, wr_ref, br_ref, h1_ref, u2_ref, eid_ref, ew_ref):
    is_ctx = pl.program_id(1) >= N_LAT_TILES
    hy = jnp.where(is_ctx, hyc_ref[0], hyl_ref[0]).astype(BF16)
    att = jnp.where(is_ctx, attc_ref[0], attl_ref[0])
    sg = sg_ref[0].astype(F32)
    m = sg[:, :D_MODEL] * _dot(hy, who_ref[0]) + sg[:, D_MODEL:] * _dot(att, wmo_ref[0])
    y = _dot(m.astype(BF16), wout_ref[0])
    g1 = mod_ref[0, 0, 0, 2:3, :]
    sh2 = mod_ref[0, 0, 0, 3:4, :]
    sc2 = mod_ref[0, 0, 0, 4:5, :]
    h1 = h_ref[0] + g1 * y
    h1_ref[0] = h1
    u2 = _rms(h1, g_ref[0]) * (1.0 + sc2) + sh2
    u2_ref[0] = u2
    ids, ws = _route_tile(_dot3(u2, wr_ref[0]), br_ref[0])
    for c in range(TOP_K):
        eid_ref[0, :, c:c + 1] = ids[c]
        ew_ref[0, :, c:c + 1] = ws[c]


def _merge(l, n_tiles, h, sg, hy_lat, hy_ctx, att_lat, att_ctx, mods, norm2_g, who, wmo, wout, wr, br):
    rows = n_tiles * TM
    tile = lambda w: pl.BlockSpec((1, TM, w), lambda b, i: (b, i, 0))
    lat = lambda w: pl.BlockSpec((1, TM, w), lambda b, i: (b, jnp.minimum(i, N_LAT_TILES - 1), 0))
    ctx = lambda w: pl.BlockSpec((1, TM, w), lambda b, i: (b, 0, 0))
    wfull = lambda a: pl.BlockSpec((1,) + a.shape[1:], lambda b, i: (l, 0, 0))
    return pl.pallas_call(
        _merge_kernel,
        out_shape=(
            jax.ShapeDtypeStruct((BATCH, rows, D_MODEL), F32),
            jax.ShapeDtypeStruct((BATCH, rows, D_MODEL), F32),
            jax.ShapeDtypeStruct((BATCH, rows, TOP_K), jnp.int32),
            jax.ShapeDtypeStruct((BATCH, rows, TOP_K), F32),
        ),
        grid=(BATCH, n_tiles),
        in_specs=[
            tile(D_MODEL), tile(2 * D_MODEL),
            lat(HY_WIDTH), ctx(HY_WIDTH), lat(MLA_HEADS * V_DIM), ctx(MLA_HEADS * V_DIM),
            pl.BlockSpec((1, 1, 1, N_MOD, D_MODEL), lambda b, i: (l, b, i // N_LAT_TILES, 0, 0)),
            pl.BlockSpec((1, 1, D_MODEL), lambda b, i: (l, 0, 0)),
            wfull(who), wfull(wmo), wfull(wout), wfull(wr), wfull(br),
        ],
        out_specs=(tile(D_MODEL), tile(D_MODEL), tile(TOP_K), tile(TOP_K)),
        compiler_params=_cp("parallel", "parallel"),
        name="merge",
    )(h, sg, hy_lat, hy_ctx, att_lat, att_ctx, mods, norm2_g, who, wmo, wout, wr, br)


def _expert_kernel(be_ref, b0_ref, nr_ref, ord_ref, u_hbm, wg_ref, wu_ref, wd_ref, y_hbm,
                   xbuf, ybuf, gsem, ssem):
    i = pl.program_id(0)
    nb = pl.num_programs(0)
    slot = i % 2

    def in_copy(blk, sl, r):
        tok = lax.shift_right_logical(ord_ref[b0_ref[blk] + r], 1)
        return pltpu.make_async_copy(u_hbm.at[pl.ds(tok, 1)], xbuf.at[sl, pl.ds(r, 1)], gsem.at[sl])

    def out_copy(blk, sl, r):
        a = ord_ref[b0_ref[blk] + r]
        col = pl.multiple_of((a & 1) * D_MODEL, D_MODEL)
        dst = y_hbm.at[pl.ds(lax.shift_right_logical(a, 1), 1), pl.ds(col, D_MODEL)]
        return pltpu.make_async_copy(ybuf.at[sl, pl.ds(r, 1)], dst, ssem.at[sl])

    def for_rows(blk, fn):
        n = nr_ref[blk]
        n_groups = n // DMA_UNROLL

        def group(g, carry):
            for j in range(DMA_UNROLL):
                fn(g * DMA_UNROLL + j)
            return carry

        def single(r, carry):
            fn(r)
            return carry

        lax.fori_loop(0, n_groups, group, 0)
        lax.fori_loop(n_groups * DMA_UNROLL, n, single, 0)

    @pl.when(i == 0)
    def _():
        xbuf[...] = jnp.zeros_like(xbuf)
        for_rows(0, lambda r: in_copy(0, 0, r).start())

    @pl.when(i + 1 < nb)
    def _():
        for_rows(i + 1, lambda r: in_copy(i + 1, 1 - slot, r).start())

    for_rows(i, lambda r: in_copy(i, slot, r).wait())

    @pl.when(i >= 2)
    def _():
        for_rows(i - 2, lambda r: out_copy(i - 2, slot, r).wait())

    @pl.when(nr_ref[i] > 0)
    def _():
        x = xbuf[slot].astype(BF16)
        hg = _dot(x, wg_ref[0, 0].astype(BF16))
        hu = _dot(x, wu_ref[0, 0].astype(BF16))
        hb = (hg * jax.nn.sigmoid(hg) * hu).astype(BF16)
        ybuf[slot] = _dot(hb, wd_ref[0, 0].astype(BF16))

    for_rows(i, lambda r: out_copy(i, slot, r).start())

    @pl.when(i == nb - 1)
    def _():
        for_rows(i - 1, lambda r: out_copy(i - 1, 1 - slot, r).wait())
        for_rows(i, lambda r: out_copy(i, slot, r).wait())


def _experts(l, block_expert, block_row0, block_rows, order, u, w_gate_e, w_up_e, w_down_e):
    n_blocks = block_expert.shape[0]
    wspec = lambda r, c: pl.BlockSpec((1, 1, r, c), lambda i, be, b0, nr, od: (l, be[i], 0, 0))
    grid_spec = pltpu.PrefetchScalarGridSpec(
        num_scalar_prefetch=4,
        grid=(n_blocks,),
        in_specs=[
            pl.BlockSpec(memory_space=pl.ANY),
            wspec(D_MODEL, D_EXPERT), wspec(D_MODEL, D_EXPERT), wspec(D_EXPERT, D_MODEL),
        ],
        out_specs=pl.BlockSpec(memory_space=pl.ANY),
        scratch_shapes=[
            pltpu.VMEM((2, MOE_BM, D_MODEL), F32),
            pltpu.VMEM((2, MOE_BM, D_MODEL), F32),
            pltpu.SemaphoreType.DMA((2,)),
            pltpu.SemaphoreType.DMA((2,)),
        ],
    )
    return pl.pallas_call(
        _expert_kernel,
        out_shape=jax.ShapeDtypeStruct((u.shape[0], TOP_K * D_MODEL), F32),
        grid_spec=grid_spec,
        compiler_params=_cp("arbitrary"),
        name="experts",
    )(block_expert, block_row0, block_rows, order, u, w_gate_e, w_up_e, w_down_e)


def _dispatch(expert):
    A = expert.shape[0]
    e_sorted, order = lax.sort((expert, jnp.arange(A, dtype=jnp.int32)), num_keys=1)
    start = jnp.searchsorted(e_sorted, jnp.arange(N_EXPERTS + 1, dtype=jnp.int32), side="left").astype(jnp.int32)
    counts = start[1:] - start[:-1]
    n_blk = (counts + MOE_BM - 1) // MOE_BM
    blk_end = jnp.cumsum(n_blk)
    n_blocks = -(-A // MOE_BM) + N_EXPERTS
    b = jnp.arange(n_blocks, dtype=jnp.int32)
    block_expert = jnp.minimum(jnp.searchsorted(blk_end, b, side="right"), N_EXPERTS - 1).astype(jnp.int32)
    row_in_expert = (b - (blk_end[block_expert] - n_blk[block_expert])) * MOE_BM
    block_rows = jnp.clip(counts[block_expert] - row_in_expert, 0, MOE_BM).astype(jnp.int32)
    block_row0 = jnp.where(block_rows > 0, start[block_expert] + row_in_expert, 0).astype(jnp.int32)
    return order, block_expert, block_row0, block_rows


def _moe_sum(y_ref, w_ref):
    y, w = y_ref[0], w_ref[0]
    return w[:, 0:1] * y[:, :D_MODEL] + w[:, 1:2] * y[:, D_MODEL:]


def _combine_kernel(h_ref, y_ref, w_ref, mod_ref, o_ref):
    o_ref[0] = h_ref[0] + mod_ref[0, 0, 0, 5:6, :] * _moe_sum(y_ref, w_ref)


def _final_kernel(h_ref, y_ref, w_ref, mod_ref, g_ref, o_ref):
    h2 = h_ref[0] + mod_ref[0, 0, 0, 5:6, :] * _moe_sum(y_ref, w_ref)
    o_ref[0] = _rms(h2, g_ref[...])


def _combine(l, n_tiles, h1, y, w, mods, final_g=None):
    rows = n_tiles * TM
    tile = pl.BlockSpec((1, TM, D_MODEL), lambda b, i: (b, i, 0))
    in_specs = [tile, pl.BlockSpec((1, TM, TOP_K * D_MODEL), lambda b, i: (b, i, 0)),
                pl.BlockSpec((1, TM, TOP_K), lambda b, i: (b, i, 0)),
                pl.BlockSpec((1, 1, 1, N_MOD, D_MODEL), lambda b, i: (l, b, i // N_LAT_TILES, 0, 0))]
    args = [h1, y, w, mods]
    kern = _combine_kernel
    if final_g is not None:
        in_specs.append(pl.BlockSpec((1, D_MODEL), lambda b, i: (0, 0)))
        args.append(final_g.reshape(1, D_MODEL))
        kern = _final_kernel
    return pl.pallas_call(
        kern,
        out_shape=jax.ShapeDtypeStruct((BATCH, rows, D_MODEL), F32),
        grid=(BATCH, n_tiles),
        in_specs=in_specs,
        out_specs=tile,
        compiler_params=_cp("parallel", "parallel"),
        name="combine",
    )(*args)


def _rope_tables():
    n = jnp.arange(SEQ)
    pos = jnp.stack([n // GRID_W, n % GRID_W], axis=-1).astype(F32)
    inv = ROPE_BASE ** (-jnp.arange(ROPE_PAIRS, dtype=F32) / ROPE_PAIRS)
    ang = pos[:, :, None] * inv
    cos, sin = jnp.cos(ang), jnp.sin(ang)
    cos32 = jnp.stack([cos, cos], axis=2).reshape(SEQ, QK_ROPE)
    sin32 = jnp.stack([-sin, sin], axis=2).reshape(SEQ, QK_ROPE)
    one = jnp.ones((SEQ, QK_ROPE), F32)
    zero = jnp.zeros((SEQ, QK_ROPE), F32)
    cosq = jnp.concatenate([jnp.ones((SEQ, QK_NOPE), F32), cos32, one], axis=1) * Q_SCALE
    sinq = jnp.concatenate([jnp.zeros((SEQ, QK_NOPE), F32), sin32, zero], axis=1) * Q_SCALE
    cosk = jnp.concatenate([jnp.zeros((SEQ, QK_NOPE), F32), cos32, zero], axis=1)
    sink = jnp.concatenate([jnp.zeros((SEQ, QK_NOPE), F32), sin32, zero], axis=1)
    lat = jnp.stack([cosq, sinq, cosk, sink])
    ctx_row = jnp.ones((HEAD_PAD,), F32) * Q_SCALE
    ctx_cosk = jnp.concatenate([jnp.zeros((QK_NOPE + QK_ROPE,), F32), jnp.ones((QK_ROPE,), F32)])
    ctx = jnp.stack([ctx_row, jnp.zeros_like(ctx_row), ctx_cosk, jnp.zeros_like(ctx_row)])
    ctx = jnp.broadcast_to(ctx[:, None, :], (4, CTX_LEN, HEAD_PAD))
    return jnp.concatenate([lat, ctx], axis=1)


def _dft_matrices(L):
    N = 2 * L
    nfb = L // FB
    t = jnp.arange(L, dtype=jnp.int32)

    def tables(n_rows, step):
        ang = ((jnp.arange(n_rows, dtype=jnp.int32)[:, None] * step * t[None, :]) % N).astype(F32)
        ang = ang * (2.0 * math.pi / N)
        return jnp.cos(ang), jnp.sin(ang)

    ca, sa = tables(nfb, FB)
    cb, sb = tables(FB, 1)
    alt = jnp.where((t % 2) == 0, 1.0, -1.0).astype(F32)
    i_idx = jnp.arange(nfb)[:, None, None]
    r_idx = jnp.arange(FB)[None, :, None]
    dc = (i_idx == 0) & (r_idx == 0)
    c = ca[:, None, :] * cb[None] - sa[:, None, :] * sb[None]
    s = jnp.where(dc, alt, -(sa[:, None, :] * cb[None] + ca[:, None, :] * sb[None]))
    f = jnp.stack([c, s], axis=1).astype(BF16)
    wk = jnp.where(dc, 1.0 / N, 2.0 / N).astype(F32)[..., 0]
    cat, sat, cbt, sbt = ca.T[:, :, None], sa.T[:, :, None], cb.T[:, None, :], sb.T[:, None, :]
    ct = (cat * cbt - sat * sbt) * wk
    st = jnp.where(dc[..., 0], alt[:, None, None], -(sat * cbt + cat * sbt)) * wk
    g = jnp.stack([ct, st], axis=2).reshape(L, 2 * L).astype(BF16)
    return f, g


def _filter_embedding(L):
    t = jnp.linspace(0.0, 1.0, L, dtype=F32)[:, None]
    w = 2.0 * math.pi * jnp.arange(L, dtype=F32)[:, None] / L
    bands = jnp.linspace(1e-4, HY_BANDS - 1, HY_BANDS, dtype=F32)[None, :]
    return jnp.concatenate([t, jnp.cos(bands * w), -jnp.sin(bands * w),
                            jnp.zeros((L, LANES - HY_EMB), F32)], axis=-1)


def _rope_swap_perm():
    idx = np.arange(QK_ROPE)
    axis, half, pair = idx // (2 * ROPE_PAIRS), (idx // ROPE_PAIRS) % 2, idx % ROPE_PAIRS
    return axis * 2 * ROPE_PAIRS + (1 - half) * ROPE_PAIRS + pair


def _prep_weights(w_in, w_uq, w_ukv, w_hy_o, w_mla_o, w_out, w_router_g, w_router_e):
    perm = _rope_swap_perm()
    c_q = HY_COLS
    c_kv = HY_COLS + Q_LORA
    c_kr = c_kv + KV_LORA
    w_kr = w_in[:, :, c_kr:MLA_END]
    zeros = lambda n: jnp.zeros((DEPTH, D_MODEL, n), F32)
    wkr = jnp.concatenate([zeros(QK_NOPE), w_kr, w_kr,
                           zeros(QK_NOPE), w_kr[:, :, perm], zeros(QK_ROPE)], axis=-1)
    uq = w_uq.reshape(DEPTH, Q_LORA, MLA_HEADS, QK_NOPE + QK_ROPE)
    uq_r = uq[..., QK_NOPE:]
    wuq = jnp.concatenate([uq, uq_r], axis=-1).reshape(DEPTH, Q_LORA, MLA_HEADS * HEAD_PAD)
    wuqs = jnp.concatenate([jnp.zeros_like(uq[..., :QK_NOPE]), uq_r[..., perm], jnp.zeros_like(uq_r)],
                           axis=-1).reshape(DEPTH, Q_LORA, MLA_HEADS * HEAD_PAD)
    ukv = w_ukv.reshape(DEPTH, KV_LORA, MLA_HEADS, QK_NOPE + V_DIM)
    zpad = jnp.zeros_like(ukv[..., :HEAD_PAD - QK_NOPE])
    wuk = jnp.concatenate([ukv[..., :QK_NOPE], zpad], axis=-1).reshape(DEPTH, KV_LORA, MLA_HEADS * HEAD_PAD)
    wuv = jnp.concatenate([ukv[..., QK_NOPE:], zpad], axis=-1).reshape(DEPTH, KV_LORA, MLA_HEADS * HEAD_PAD)
    wr = jnp.concatenate([w_router_g, w_router_e,
                          jnp.zeros((DEPTH, D_MODEL, ROUTER_PAD - N_GROUPS - N_EXPERTS), F32)], axis=-1)
    bf = lambda a: a.astype(BF16)
    return dict(
        why=bf(w_in[:, :, :HY_COLS]), wgt=bf(w_in[:, :, MLA_END:]),
        wq=bf(w_in[:, :, c_q:c_kv]), wkv=bf(w_in[:, :, c_kv:c_kr]), wkr=bf(wkr),
        wuq=bf(wuq), wuqs=bf(wuqs), wuk=bf(wuk), wuv=bf(wuv),
        who=bf(w_hy_o), wmo=bf(w_mla_o), wout=bf(w_out), wr=wr,
    )


def _hyena(l, L, row0, u, fmat, gmat, zemb, deltas, flt, flt_bias):
    seg = fmat.shape[-1]
    spec = _spectrum(L, seg, fmat, _filters(l, L, zemb, deltas, flt))
    rb = row0 // L
    yf = _hy_fwd(L, seg, u, rb, 0, fmat, spec, 0)
    z1 = _hy_inv(l, L, seg, yf, gmat, u, row0, 1, u, row0, 0, flt_bias, 0)
    yf = _hy_fwd(L, seg, z1, 0, 0, fmat, spec, 1)
    return _hy_inv(l, L, seg, yf, gmat, u, row0, 2, z1, 0, 0, flt_bias, 1)


def kernel(x, c, ctx, c_ctx, w_ada, b_ada, norm1_g, w_in, q_norm_g, kv_norm_g, w_uq, w_ukv,
           sc_w, sc_b, flt_w1, flt_b1, flt_freq, flt_w2, flt_b2, flt_w3, flt_bias,
           w_hy_o, w_mla_o, w_out, norm2_g, w_router_g, b_router_g, w_router_e, b_router_e,
           w_gate_e, w_up_e, w_down_e, final_g):
    wts = _prep_weights(w_in, w_uq, w_ukv, w_hy_o, w_mla_o, w_out, w_router_g, w_router_e)
    tabs = _rope_tables()
    f_lat, g_lat = _dft_matrices(min(SEQ, HY_SEG))
    f_ctx, g_ctx = _dft_matrices(min(CTX_LEN, HY_SEG))
    z_lat, z_ctx = _filter_embedding(SEQ), _filter_embedding(CTX_LEN)
    deltas = jnp.linspace(math.log(HY_TARGET) / HY_FAST_DECAY, math.log(HY_TARGET) / HY_SLOW_DECAY,
                          HY_WIDTH, dtype=F32)[None, :]

    cc = jnp.concatenate([c, c_ctx[None, :], jnp.zeros((3, D_MODEL), F32)], axis=0)
    mod = _ada(cc, w_ada, b_ada).reshape(DEPTH, 8, N_MOD, D_MODEL)
    mods = jnp.stack([mod[:, :BATCH], jnp.broadcast_to(mod[:, BATCH:BATCH + 1], (DEPTH, BATCH, N_MOD, D_MODEL))],
                     axis=2)

    r3 = lambda a: a.reshape(DEPTH, 1, a.shape[-1])
    norm1, norm2, qg, kvg = r3(norm1_g), r3(norm2_g), r3(q_norm_g), r3(kv_norm_g)
    scb = r3(sc_b)
    fb1, ffr, fb2 = r3(flt_b1), r3(flt_freq), r3(flt_b2)
    fbias = flt_bias.reshape(DEPTH, HY_ORDER, 1, HY_WIDTH)
    w1_pad = jnp.concatenate([flt_w1, jnp.zeros((DEPTH, LANES - HY_EMB, HY_FFN), F32)], axis=1)
    flt = (w1_pad, fb1, ffr, flt_w2, fb2, flt_w3)

    b_router = jnp.concatenate([b_router_g, b_router_e,
                                jnp.zeros((DEPTH, ROUTER_PAD - N_GROUPS - N_EXPERTS), F32)], axis=-1)
    b_router = b_router.reshape(DEPTH, 1, ROUTER_PAD)

    h = jnp.concatenate([x, ctx], axis=1)
    for l in range(DEPTH):
        last = l == DEPTH - 1
        n_tiles = N_LAT_TILES if last else N_ALL_TILES
        p_hy, sg, q, k, v = _inproj(l, h, mods, norm1, tabs, qg, kvg, wts)

        u = _short_conv(l, p_hy, sc_w, scb)
        att_lat = _attention(q, k, v, latent=True)
        hy_lat = _hyena(l, SEQ, 0, u, f_lat, g_lat, z_lat, deltas, flt, fbias)
        if last:
            att_ctx, hy_ctx = att_lat, hy_lat
        else:
            att_ctx = _attention(q, k, v, latent=False)
            hy_ctx = _hyena(l, CTX_LEN, SEQ, u, f_ctx, g_ctx, z_ctx, deltas, flt, fbias)

        h1, u2, eid, ew = _merge(l, n_tiles, h, sg, hy_lat, hy_ctx, att_lat, att_ctx, mods, norm2,
                                 wts["who"], wts["wmo"], wts["wout"], wts["wr"], b_router)

        T = BATCH * n_tiles * TM
        order, block_expert, block_row0, block_rows = _dispatch(eid.reshape(T * TOP_K))
        y = _experts(l, block_expert, block_row0, block_rows, order, u2.reshape(T, D_MODEL),
                     w_gate_e, w_up_e, w_down_e)
        h = _combine(l, n_tiles, h1, y.reshape(BATCH, n_tiles * TM, TOP_K * D_MODEL), ew, mods,
                     final_g if last else None)
    return h
```

```python
import functools
import math

import jax
import jax.numpy as jnp
import numpy as np
from jax import lax
from jax.experimental import pallas as pl
from jax.experimental.pallas import tpu as pltpu

F32 = jnp.float32
BF16 = jnp.bfloat16

D_MODEL = 1024
BATCH = 4
SEQ = 4096
DEPTH = 4
GRID_W = 64
CTX_LEN = 256
S_ALL = SEQ + CTX_LEN
EPS = 1e-6
N_MOD = 6

HY_WIDTH = 512
HY_ORDER = 2
HY_BANDS = 16
HY_EMB = 1 + 2 * HY_BANDS
HY_FFN = 64
SHORT_K = 3
HY_FAST_DECAY = 0.3
HY_SLOW_DECAY = 1.5
HY_TARGET = 1e-2

MLA_HEADS = 8
QK_NOPE = 64
QK_ROPE = 32
V_DIM = 64
Q_LORA = 384
KV_LORA = 256
ROPE_PAIRS = QK_ROPE // 4
ROPE_BASE = 10000.0
ATTN_SCALE = (QK_NOPE + QK_ROPE) ** -0.5
Q_SCALE = ATTN_SCALE * math.log2(math.e)

N_GROUPS = 8
EXP_PER_GROUP = 8
N_EXPERTS = N_GROUPS * EXP_PER_GROUP
TOP_K = 2
D_EXPERT = 256

HY_COLS = (HY_ORDER + 1) * HY_WIDTH
MLA_END = HY_COLS + Q_LORA + KV_LORA + QK_ROPE

LANES = 128
HEAD_PAD = LANES
VMEM_LIMIT = 56 << 20

TM = 256
N_LAT_TILES = SEQ // TM
N_ALL_TILES = S_ALL // TM
TQ = 512
TK = 2048
FB = 256
HY_SEG = 1024
MOE_BM = 256
DMA_UNROLL = 8
ROUTER_PAD = LANES


def _cp(*sem):
    return pltpu.CompilerParams(dimension_semantics=sem, vmem_limit_bytes=VMEM_LIMIT)


def _dot(a, b):
    return jnp.dot(a, b, preferred_element_type=F32)


def _split(a):
    hi = a.astype(BF16)
    lo = (a - hi.astype(F32)).astype(BF16)
    return hi, lo


def _dot3(a, b):
    ah, al = _split(a)
    bh, bl = _split(b)
    return _dot(ah, bh) + (_dot(ah, bl) + _dot(al, bh))


def _rms(x, g):
    return x * lax.rsqrt(jnp.mean(x * x, axis=-1, keepdims=True) + EPS) * g


def _ada_kernel(c_ref, w_ref, b_ref, o_ref):
    cc = c_ref[...]
    s = cc * jax.nn.sigmoid(cc)
    o_ref[0] = _dot3(s, w_ref[0]) + b_ref[0]


def _ada(cc, w_ada, b_ada):
    tn = 1536
    n = N_MOD * D_MODEL
    return pl.pallas_call(
        _ada_kernel,
        out_shape=jax.ShapeDtypeStruct((DEPTH, 8, n), F32),
        grid=(DEPTH, n // tn),
        in_specs=[
            pl.BlockSpec((8, D_MODEL), lambda l, j: (0, 0)),
            pl.BlockSpec((1, D_MODEL, tn), lambda l, j: (l, 0, j)),
            pl.BlockSpec((1, 1, tn), lambda l, j: (l, 0, j)),
        ],
        out_specs=pl.BlockSpec((1, 8, tn), lambda l, j: (l, 0, j)),
        compiler_params=_cp("parallel", "parallel"),
        name="ada",
    )(cc, w_ada, b_ada.reshape(DEPTH, 1, n))


def _inproj_kernel(h_ref, mod_ref, g_ref, tab_ref, qg_ref, kvg_ref,
                   why_ref, wgt_ref, wq_ref, wkv_ref, wkr_ref, wuq_ref, wuqs_ref, wuk_ref, wuv_ref,
                   phy_ref, sg_ref, q_ref, k_ref, v_ref):
    h = h_ref[0]
    shift = mod_ref[0, 0, 0, 0:1, :]
    scale = mod_ref[0, 0, 0, 1:2, :]
    u = (_rms(h, g_ref[0]) * (1.0 + scale) + shift).astype(BF16)

    phy_ref[0] = _dot(u, why_ref[0])
    sg_ref[0] = jax.nn.sigmoid(_dot(u, wgt_ref[0])).astype(BF16)

    cosq, sinq, cosk, sink = tab_ref[0], tab_ref[1], tab_ref[2], tab_ref[3]
    cq = _rms(_dot(u, wq_ref[0]), qg_ref[0]).astype(BF16)
    qa = _dot(cq, wuq_ref[0])
    qs = _dot(cq, wuqs_ref[0])
    ckv = _rms(_dot(u, wkv_ref[0]), kvg_ref[0]).astype(BF16)
    ka = _dot(ckv, wuk_ref[0])
    va = _dot(ckv, wuv_ref[0])
    kr = _dot(u, wkr_ref[0])
    krk = kr[:, :HEAD_PAD] * cosk + kr[:, HEAD_PAD:] * sink
    ones_col = (lax.broadcasted_iota(jnp.int32, (1, HEAD_PAD), 1) == V_DIM).astype(F32)
    for hh in range(MLA_HEADS):
        sl = slice(hh * HEAD_PAD, (hh + 1) * HEAD_PAD)
        q_ref[0, hh] = (qa[:, sl] * cosq + qs[:, sl] * sinq).astype(BF16)
        k_ref[0, hh] = (ka[:, sl] + krk).astype(BF16)
        v_ref[0, hh] = (va[:, sl] + ones_col).astype(BF16)


def _inproj(l, h, mods, norm1_g, tabs, q_norm_g, kv_norm_g, wts):
    names = ("why", "wgt", "wq", "wkv", "wkr", "wuq", "wuqs", "wuk", "wuv")
    w_specs = [pl.BlockSpec((1,) + wts[n].shape[1:], lambda b, i: (l, 0, 0)) for n in names]
    hd = MLA_HEADS * HEAD_PAD
    qkv_shape = jax.ShapeDtypeStruct((BATCH, MLA_HEADS, S_ALL, HEAD_PAD), BF16)
    qkv_spec = pl.BlockSpec((1, MLA_HEADS, TM, HEAD_PAD), lambda b, i: (b, 0, i, 0))
    del hd
    return pl.pallas_call(
        _inproj_kernel,
        out_shape=(
            jax.ShapeDtypeStruct((BATCH, S_ALL, HY_COLS), F32),
            jax.ShapeDtypeStruct((BATCH, S_ALL, 2 * D_MODEL), BF16),
            qkv_shape, qkv_shape, qkv_shape,
        ),
        grid=(BATCH, N_ALL_TILES),
        in_specs=[
            pl.BlockSpec((1, TM, D_MODEL), lambda b, i: (b, i, 0)),
            pl.BlockSpec((1, 1, 1, N_MOD, D_MODEL), lambda b, i: (l, b, i // N_LAT_TILES, 0, 0)),
            pl.BlockSpec((1, 1, D_MODEL), lambda b, i: (l, 0, 0)),
            pl.BlockSpec((4, TM, HEAD_PAD), lambda b, i: (0, i, 0)),
            pl.BlockSpec((1, 1, Q_LORA), lambda b, i: (l, 0, 0)),
            pl.BlockSpec((1, 1, KV_LORA), lambda b, i: (l, 0, 0)),
        ] + w_specs,
        out_specs=(
            pl.BlockSpec((1, TM, HY_COLS), lambda b, i: (b, i, 0)),
            pl.BlockSpec((1, TM, 2 * D_MODEL), lambda b, i: (b, i, 0)),
            qkv_spec, qkv_spec, qkv_spec,
        ),
        compiler_params=_cp("parallel", "parallel"),
        name="inproj",
    )(h, mods, norm1_g, tabs, q_norm_g, kv_norm_g, *[wts[n] for n in names])


def _flash_step(q_ref, k_ref, v_ref, rows, carry):
    out = []
    for hh in range(2):
        m, acc = carry[hh]
        s = lax.dot_general(q_ref[0, hh], k_ref[0, hh, rows, :], (((1,), (1,)), ((), ())),
                            preferred_element_type=F32)
        m_new = jnp.maximum(m, jnp.max(s, axis=-1, keepdims=True))
        p = jnp.exp2(s - m_new)
        acc = acc * jnp.exp2(m - m_new) + _dot(p.astype(BF16), v_ref[0, hh, rows, :])
        out.append((m_new, acc))
    return tuple(out)


def _attn_kernel(q_ref, k_ref, v_ref, o_ref, *, n_full, tk, tail):
    tq = q_ref.shape[2]
    init = (jnp.full((tq, 1), -1e30, F32), jnp.zeros((tq, HEAD_PAD), F32))
    carry = (init, init)
    if n_full:
        def body(j, carry):
            return _flash_step(q_ref, k_ref, v_ref, pl.ds(pl.multiple_of(j * tk, tk), tk), carry)
        carry = lax.fori_loop(0, n_full, body, carry)
    if tail:
        carry = _flash_step(q_ref, k_ref, v_ref, pl.ds(n_full * tk, tail), carry)
    outs = [acc[:, :V_DIM] / acc[:, V_DIM:V_DIM + 1] for _, acc in carry]
    o_ref[0] = jnp.concatenate(outs, axis=-1).astype(BF16)


def _attention(q, k, v, latent):
    if latent:
        tq, nq, q0, kv_rows, kv_blk = TQ, SEQ // TQ, 0, S_ALL, 0
        kern = functools.partial(_attn_kernel, n_full=S_ALL // TK, tk=TK, tail=S_ALL % TK)
    else:
        tq, nq, q0, kv_rows, kv_blk = CTX_LEN, 1, SEQ // CTX_LEN, CTX_LEN, SEQ // CTX_LEN
        kern = functools.partial(_attn_kernel, n_full=0, tk=TK, tail=CTX_LEN)
    return pl.pallas_call(
        kern,
        out_shape=jax.ShapeDtypeStruct((BATCH, nq * tq, MLA_HEADS * V_DIM), BF16),
        grid=(BATCH, MLA_HEADS // 2, nq),
        in_specs=[
            pl.BlockSpec((1, 2, tq, HEAD_PAD), lambda b, hp, i: (b, hp, q0 + i, 0)),
            pl.BlockSpec((1, 2, kv_rows, HEAD_PAD), lambda b, hp, i: (b, hp, kv_blk, 0)),
            pl.BlockSpec((1, 2, kv_rows, HEAD_PAD), lambda b, hp, i: (b, hp, kv_blk, 0)),
        ],
        out_specs=pl.BlockSpec((1, tq, 2 * V_DIM), lambda b, hp, i: (b, i, hp)),
        compiler_params=_cp("parallel", "parallel", "parallel"),
        name="attn_lat" if latent else "attn_ctx",
    )(q, k, v)


def _short_conv_kernel(p_ref, w_ref, b_ref, o_ref):
    x = p_ref[0]
    rows = lax.broadcasted_iota(jnp.int32, (S_ALL, 1), 0)
    first = (rows == 0) | (rows == SEQ)
    last = (rows == SEQ - 1) | (rows == S_ALL - 1)
    prev = jnp.where(first, 0.0, pltpu.roll(x, 1, 0))
    nxt = jnp.where(last, 0.0, pltpu.roll(x, S_ALL - 1, 0))
    w = w_ref[0]
    o_ref[0] = b_ref[0] + prev * w[0:1] + x * w[1:2] + nxt * w[2:3]


def _short_conv(l, p_hy, sc_w, sc_b):
    cb = LANES
    return pl.pallas_call(
        _short_conv_kernel,
        out_shape=jax.ShapeDtypeStruct(p_hy.shape, F32),
        grid=(BATCH, HY_COLS // cb),
        in_specs=[
            pl.BlockSpec((1, S_ALL, cb), lambda b, j: (b, 0, j)),
            pl.BlockSpec((1, SHORT_K, cb), lambda b, j: (l, 0, j)),
            pl.BlockSpec((1, 1, cb), lambda b, j: (l, 0, j)),
        ],
        out_specs=pl.BlockSpec((1, S_ALL, cb), lambda b, j: (b, 0, j)),
        compiler_params=_cp("parallel", "parallel"),
        name="short_conv",
    )(p_hy, sc_w, sc_b)


def _filter_kernel(z_ref, w1_ref, b1_ref, fr_ref, w2_ref, b2_ref, w3f_ref, w3b_ref, dl_ref, hfb_ref, a_ref):
    z = z_ref[...]

    @pl.when((pl.program_id(0) == 0) & (pl.program_id(1) == 0))
    def _():
        fr = fr_ref[0]
        a1 = jnp.sin(fr * (_dot3(z, w1_ref[0]) + b1_ref[0]))
        a_ref[...] = jnp.sin(fr * (_dot3(a1, w2_ref[0]) + b2_ref[0]))

    a = a_ref[...]
    decay = jnp.exp(-z[:, 0:1] * jnp.abs(dl_ref[...]))
    rows = lax.broadcasted_iota(jnp.int32, (z.shape[0], 1), 0)

    def one_direction(w3_ref):
        h = _dot3(a, w3_ref[0]) * decay
        return h * lax.rsqrt(jnp.sum(h * h, axis=0, keepdims=True) + EPS)

    hfb_ref[0] = one_direction(w3f_ref).astype(BF16)
    hfb_ref[1] = jnp.where(rows == 0, 0.0, one_direction(w3b_ref)).astype(BF16)


def _filters(l, L, zemb, deltas, flt):
    w1, b1, fr, w2, b2, w3 = flt
    ncb = HY_WIDTH // LANES
    full = lambda *shape: pl.BlockSpec((1,) + shape, lambda o, j: (l,) + (0,) * len(shape))
    return pl.pallas_call(
        _filter_kernel,
        out_shape=jax.ShapeDtypeStruct((2, L, HY_ORDER * HY_WIDTH), BF16),
        grid=(HY_ORDER, ncb),
        in_specs=[
            pl.BlockSpec((L, LANES), lambda o, j: (0, 0)),
            full(LANES, HY_FFN), full(1, HY_FFN), full(1, HY_FFN),
            full(HY_FFN, HY_FFN), full(1, HY_FFN),
            pl.BlockSpec((1, HY_FFN, LANES), lambda o, j: (l, 0, o * 2 * ncb + j)),
            pl.BlockSpec((1, HY_FFN, LANES), lambda o, j: (l, 0, o * 2 * ncb + ncb + j)),
            pl.BlockSpec((1, LANES), lambda o, j: (0, j)),
        ],
        out_specs=pl.BlockSpec((2, L, LANES), lambda o, j: (0, 0, o * ncb + j)),
        scratch_shapes=[pltpu.VMEM((L, HY_FFN), F32)],
        compiler_params=_cp("arbitrary", "arbitrary"),
        name="hy_filter",
    )(zemb, w1, b1, fr, w2, b2, w3, w3, deltas)


def _packed_row0(block):
    return (lax.broadcasted_iota(jnp.int32, (FB, 1), 0) == 0) & (block == 0)


def _packed_mac(acc, c, z):
    (ar, ai), (cr, ci), (zr, zi) = acc, c, z
    return ar + (zr * cr - zi * ci), ai + (zr * ci + zi * cr)


def _spec_kernel(f_ref, h_ref, o_ref, *, seg, n_seg):
    fblk = pl.program_id(1)
    fr, fi = f_ref[0, 0], f_ref[0, 1]
    row0 = _packed_row0(fblk)
    sign = jnp.where((lax.broadcasted_iota(jnp.int32, (FB, 1), 0) & 1) == 0, 1.0, -1.0)

    def first_half(d, m):
        taps = h_ref[d, m * seg:(m + 1) * seg, :]
        return _dot(fr, taps), _dot(fi, taps), taps[0:1, :].astype(F32)

    def causal_block(d, m, halves):
        re, im, _ = halves[m]
        if m >= 1:
            pre, pim, tap0 = halves[m - 1]
            re = re + sign * (pre - tap0)
            im = im + sign * (pim - jnp.where(row0, tap0, 0.0))
        return re, im

    def conj(re, im):
        return re, jnp.where(row0, im, -im)

    halves = [[first_half(d, m) for m in range(n_seg)] for d in range(2)]
    for li in range(2 * n_seg - 1):
        lag = li - (n_seg - 1)
        if lag > 0:
            re, im = causal_block(0, lag, halves[0])
        elif lag < 0:
            re, im = conj(*causal_block(1, -lag, halves[1]))
        else:
            fre, fim = causal_block(0, 0, halves[0])
            gre, gim = conj(*causal_block(1, 0, halves[1]))
            re, im = fre + gre, fim + gim
        o_ref[0, 0, li, 0] = re
        o_ref[0, 0, li, 1] = im


def _spectrum(L, seg, fmat, hfb):
    n_seg, nfb = L // seg, seg // FB
    n_lag = 2 * n_seg - 1
    return pl.pallas_call(
        functools.partial(_spec_kernel, seg=seg, n_seg=n_seg),
        out_shape=jax.ShapeDtypeStruct((HY_ORDER, nfb, n_lag, 2, FB, HY_WIDTH), F32),
        grid=(HY_ORDER, nfb),
        in_specs=[
            pl.BlockSpec((1, 2, FB, seg), lambda o, f: (f, 0, 0, 0)),
            pl.BlockSpec((2, L, HY_WIDTH), lambda o, f: (0, 0, o)),
        ],
        out_specs=pl.BlockSpec((1, 1, n_lag, 2, FB, HY_WIDTH), lambda o, f: (o, f, 0, 0, 0, 0)),
        compiler_params=_cp("parallel", "parallel"),
        name="hy_spectrum",
    )(fmat, hfb)


def _fwd_kernel(z_ref, f_ref, c_ref, y_ref, zb_ref, *, seg, n_seg):
    fblk = pl.program_id(1)

    @pl.when(fblk == 0)
    def _():
        zb_ref[...] = z_ref[0].astype(BF16)

    fr, fi = f_ref[0, 0], f_ref[0, 1]
    row0 = _packed_row0(fblk)
    acc = [None] * n_seg
    edge = [None] * n_seg
    for j in range(n_seg):
        zj = zb_ref[j * seg:(j + 1) * seg, :]
        z = (_dot(fr, zj), _dot(fi, zj))
        for i in range(n_seg):
            c = (c_ref[0, 0, i - j + n_seg - 1, 0], c_ref[0, 0, i - j + n_seg - 1, 1])
            e = (z[0][0:1] * c[0][0:1], z[1][0:1] * c[1][0:1])
            if acc[i] is None:
                acc[i] = (z[0] * c[0] - z[1] * c[1], z[0] * c[1] + z[1] * c[0])
                edge[i] = e
            else:
                acc[i] = _packed_mac(acc[i], c, z)
                edge[i] = (edge[i][0] + e[0], edge[i][1] + e[1])
    for i in range(n_seg):
        y_ref[0, i, 0:FB] = jnp.where(row0, edge[i][0], acc[i][0]).astype(BF16)
        y_ref[0, i, FB:2 * FB] = jnp.where(row0, edge[i][1], acc[i][1]).astype(BF16)


def _hy_fwd(L, seg, z, z_rowblk, z_colblk, fmat, spec, order):
    n_seg, nfb = L // seg, seg // FB
    n_lag = 2 * n_seg - 1
    return pl.pallas_call(
        functools.partial(_fwd_kernel, seg=seg, n_seg=n_seg),
        out_shape=jax.ShapeDtypeStruct((BATCH, n_seg, 2 * seg, HY_WIDTH), BF16),
        grid=(BATCH, nfb),
        in_specs=[
            pl.BlockSpec((1, L, HY_WIDTH), lambda b, f: (b, z_rowblk, z_colblk)),
            pl.BlockSpec((1, 2, FB, seg), lambda b, f: (f, 0, 0, 0)),
            pl.BlockSpec((1, 1, n_lag, 2, FB, HY_WIDTH), lambda b, f: (order, f, 0, 0, 0, 0)),
        ],
        out_specs=pl.BlockSpec((1, n_seg, 2 * FB, HY_WIDTH), lambda b, f: (b, 0, f, 0)),
        scratch_shapes=[pltpu.VMEM((L, HY_WIDTH), BF16)],
        compiler_params=_cp("parallel", "arbitrary"),
        name="hy_fwd",
    )(z, fmat, spec)


def _inv_kernel(y_ref, g_ref, gate_ref, z_ref, bias_ref, o_ref):
    y = _dot(g_ref[...], y_ref[0, 0])
    o_ref[0] = gate_ref[0] * (y + bias_ref[0, 0] * z_ref[0])


def _hy_inv(l, L, seg, yf, gmat, u, row0, gate_colblk, z, z_row0, z_colblk, flt_bias, order):
    tmi = min(seg, 2 * TM)
    per_seg = seg // tmi
    r0, zr0 = row0 // tmi, z_row0 // tmi
    return pl.pallas_call(
        _inv_kernel,
        out_shape=jax.ShapeDtypeStruct((BATCH, L, HY_WIDTH), F32),
        grid=(BATCH, L // tmi),
        in_specs=[
            pl.BlockSpec((1, 1, 2 * seg, HY_WIDTH), lambda b, i: (b, i // per_seg, 0, 0)),
            pl.BlockSpec((tmi, 2 * seg), lambda b, i: (i % per_seg, 0)),
            pl.BlockSpec((1, tmi, HY_WIDTH), lambda b, i: (b, r0 + i, gate_colblk)),
            pl.BlockSpec((1, tmi, HY_WIDTH), lambda b, i: (b, zr0 + i, z_colblk)),
            pl.BlockSpec((1, 1, 1, HY_WIDTH), lambda b, i: (l, order, 0, 0)),
        ],
        out_specs=pl.BlockSpec((1, tmi, HY_WIDTH), lambda b, i: (b, i, 0)),
        compiler_params=_cp("parallel", "parallel"),
        name="hy_inv",
    )(yf, gmat, u, z, flt_bias)


def _route_tile(lg, bias):
    lane = lax.broadcasted_iota(jnp.int32, lg.shape, 1)
    lane_f = lane.astype(F32)
    neg = jnp.float32(-jnp.inf)
    big = jnp.float32(ROUTER_PAD)
    biased = lg + bias

    def first_argmax(v):
        m = jnp.max(v, axis=-1, keepdims=True)
        return jnp.min(jnp.where(v == m, lane_f, big), axis=-1, keepdims=True).astype(jnp.int32)

    def pick(v, idx):
        return jnp.sum(jnp.where(lane == idx, v, 0.0), axis=-1, keepdims=True)

    is_group = lane < N_GROUPS
    g_sel = first_argmax(jnp.where(is_group, biased, neg))
    raw_g = jnp.where(is_group, lg, neg)
    e_g = jnp.exp(raw_g - jnp.max(raw_g, axis=-1, keepdims=True))
    p_g = pick(e_g, g_sel) / jnp.sum(e_g, axis=-1, keepdims=True)

    lo = N_GROUPS + g_sel * EXP_PER_GROUP
    cand = jnp.where((lane >= lo) & (lane < lo + EXP_PER_GROUP), biased, neg)
    i1 = first_argmax(cand)
    i2 = first_argmax(jnp.where(lane == i1, neg, cand))
    l1, l2 = pick(lg, i1), pick(lg, i2)
    top = jnp.maximum(l1, l2)
    e1, e2 = jnp.exp(l1 - top), jnp.exp(l2 - top)
    inv = p_g / (e1 + e2)
    return (i1 - N_GROUPS, i2 - N_GROUPS), (e1 * inv, e2 * inv)


def _merge_kernel(h_ref, sg_ref, hyl_ref, hyc_ref, attl_ref, attc_ref, mod_ref, g_ref,
                  who_ref, wmo_ref, wout_ref, wr_ref, br_ref, h1_ref, u2_ref, eid_ref, ew_ref, cnt_ref):
    is_ctx = pl.program_id(1) >= N_LAT_TILES
    hy = jnp.where(is_ctx, hyc_ref[0], hyl_ref[0]).astype(BF16)
    att = jnp.where(is_ctx, attc_ref[0], attl_ref[0])
    sg = sg_ref[0].astype(F32)
    m = sg[:, :D_MODEL] * _dot(hy, who_ref[0]) + sg[:, D_MODEL:] * _dot(att, wmo_ref[0])
    y = _dot(m.astype(BF16), wout_ref[0])
    g1 = mod_ref[0, 0, 0, 2:3, :]
    sh2 = mod_ref[0, 0, 0, 3:4, :]
    sc2 = mod_ref[0, 0, 0, 4:5, :]
    h1 = h_ref[0] + g1 * y
    h1_ref[0] = h1
    u2 = _rms(h1, g_ref[0]) * (1.0 + sc2) + sh2
    u2_ref[0] = u2
    ids, ws = _route_tile(_dot3(u2, wr_ref[0]), br_ref[0])
    lane = lax.broadcasted_iota(jnp.int32, (TM, ROUTER_PAD), 1)
    hist = jnp.zeros((1, ROUTER_PAD), F32)
    for c in range(TOP_K):
        eid_ref[0, :, c:c + 1] = ids[c]
        ew_ref[0, :, c:c + 1] = ws[c]
        hist = hist + jnp.sum(jnp.where(lane == ids[c], 1.0, 0.0), axis=0, keepdims=True)
    cnt_ref[0, 0] = hist


def _merge(l, n_tiles, h, sg, hy_lat, hy_ctx, att_lat, att_ctx, mods, norm2_g, who, wmo, wout, wr, br):
    rows = n_tiles * TM
    tile = lambda w: pl.BlockSpec((1, TM, w), lambda b, i: (b, i, 0))
    lat = lambda w: pl.BlockSpec((1, TM, w), lambda b, i: (b, jnp.minimum(i, N_LAT_TILES - 1), 0))
    ctx = lambda w: pl.BlockSpec((1, TM, w), lambda b, i: (b, 0, 0))
    wfull = lambda a: pl.BlockSpec((1,) + a.shape[1:], lambda b, i: (l, 0, 0))
    return pl.pallas_call(
        _merge_kernel,
        out_shape=(
            jax.ShapeDtypeStruct((BATCH, rows, D_MODEL), F32),
            jax.ShapeDtypeStruct((BATCH, rows, D_MODEL), F32),
            jax.ShapeDtypeStruct((BATCH, rows, TOP_K), jnp.int32),
            jax.ShapeDtypeStruct((BATCH, rows, TOP_K), F32),
            jax.ShapeDtypeStruct((BATCH, n_tiles, 1, ROUTER_PAD), F32),
        ),
        grid=(BATCH, n_tiles),
        in_specs=[
            tile(D_MODEL), tile(2 * D_MODEL),
            lat(HY_WIDTH), ctx(HY_WIDTH), lat(MLA_HEADS * V_DIM), ctx(MLA_HEADS * V_DIM),
            pl.BlockSpec((1, 1, 1, N_MOD, D_MODEL), lambda b, i: (l, b, i // N_LAT_TILES, 0, 0)),
            pl.BlockSpec((1, 1, D_MODEL), lambda b, i: (l, 0, 0)),
            wfull(who), wfull(wmo), wfull(wout), wfull(wr), wfull(br),
        ],
        out_specs=(tile(D_MODEL), tile(D_MODEL), tile(TOP_K), tile(TOP_K),
                   pl.BlockSpec((1, 1, 1, ROUTER_PAD), lambda b, i: (b, i, 0, 0))),
        compiler_params=_cp("parallel", "parallel"),
        name="merge",
    )(h, sg, hy_lat, hy_ctx, att_lat, att_ctx, mods, norm2_g, who, wmo, wout, wr, br)


def _expert_kernel(be_ref, b0_ref, nr_ref, ord_ref, u_hbm, wg_ref, wu_ref, wd_ref, y_hbm,
                   xbuf, ybuf, gsem, ssem):
    i = pl.program_id(0)
    nb = pl.num_programs(0)
    slot = i % 2

    def in_copy(blk, sl, r):
        tok = lax.shift_right_logical(ord_ref[b0_ref[blk] + r], 1)
        return pltpu.make_async_copy(u_hbm.at[pl.ds(tok, 1)], xbuf.at[sl, pl.ds(r, 1)], gsem.at[sl])

    def out_copy(blk, sl, r):
        a = ord_ref[b0_ref[blk] + r]
        col = pl.multiple_of((a & 1) * D_MODEL, D_MODEL)
        dst = y_hbm.at[pl.ds(lax.shift_right_logical(a, 1), 1), pl.ds(col, D_MODEL)]
        return pltpu.make_async_copy(ybuf.at[sl, pl.ds(r, 1)], dst, ssem.at[sl])

    def for_rows(blk, fn):
        n = nr_ref[blk]
        n_groups = n // DMA_UNROLL

        def group(g, carry):
            for j in range(DMA_UNROLL):
                fn(g * DMA_UNROLL + j)
            return carry

        def single(r, carry):
            fn(r)
            return carry

        lax.fori_loop(0, n_groups, group, 0)
        lax.fori_loop(n_groups * DMA_UNROLL, n, single, 0)

    @pl.when(i == 0)
    def _():
        xbuf[...] = jnp.zeros_like(xbuf)
        for_rows(0, lambda r: in_copy(0, 0, r).start())

    @pl.when(i + 1 < nb)
    def _():
        for_rows(i + 1, lambda r: in_copy(i + 1, 1 - slot, r).start())

    for_rows(i, lambda r: in_copy(i, slot, r).wait())

    @pl.when(i >= 2)
    def _():
        for_rows(i - 2, lambda r: out_copy(i - 2, slot, r).wait())

    @pl.when(nr_ref[i] > 0)
    def _():
        x = xbuf[slot].astype(BF16)
        hg = _dot(x, wg_ref[0, 0].astype(BF16))
        hu = _dot(x, wu_ref[0, 0].astype(BF16))
        hb = (hg * jax.nn.sigmoid(hg) * hu).astype(BF16)
        ybuf[slot] = _dot(hb, wd_ref[0, 0].astype(BF16))

    for_rows(i, lambda r: out_copy(i, slot, r).start())

    @pl.when(i == nb - 1)
    def _():
        for_rows(i - 1, lambda r: out_copy(i - 1, 1 - slot, r).wait())
        for_rows(i, lambda r: out_copy(i, slot, r).wait())


def _experts(l, block_expert, block_row0, block_rows, order, u, w_gate_e, w_up_e, w_down_e):
    n_blocks = block_expert.shape[0]
    wspec = lambda r, c: pl.BlockSpec((1, 1, r, c), lambda i, be, b0, nr, od: (l, be[i], 0, 0))
    grid_spec = pltpu.PrefetchScalarGridSpec(
        num_scalar_prefetch=4,
        grid=(n_blocks,),
        in_specs=[
            pl.BlockSpec(memory_space=pl.ANY),
            wspec(D_MODEL, D_EXPERT), wspec(D_MODEL, D_EXPERT), wspec(D_EXPERT, D_MODEL),
        ],
        out_specs=pl.BlockSpec(memory_space=pl.ANY),
        scratch_shapes=[
            pltpu.VMEM((2, MOE_BM, D_MODEL), F32),
            pltpu.VMEM((2, MOE_BM, D_MODEL), F32),
            pltpu.SemaphoreType.DMA((2,)),
            pltpu.SemaphoreType.DMA((2,)),
        ],
    )
    return pl.pallas_call(
        _expert_kernel,
        out_shape=jax.ShapeDtypeStruct((u.shape[0], TOP_K * D_MODEL), F32),
        grid_spec=grid_spec,
        compiler_params=_cp("arbitrary"),
        name="experts",
    )(block_expert, block_row0, block_rows, order, u, w_gate_e, w_up_e, w_down_e)


def _dispatch(expert, counts):
    A = expert.shape[0]
    _, order = lax.sort((expert, jnp.arange(A, dtype=jnp.int32)), num_keys=1)
    start = jnp.cumsum(counts) - counts
    n_blk = (counts + MOE_BM - 1) // MOE_BM
    blk_end = jnp.cumsum(n_blk)
    blk_start = blk_end - n_blk
    n_blocks = -(-A // MOE_BM) + N_EXPERTS
    b = jnp.arange(n_blocks, dtype=jnp.int32)[:, None]
    mine = (b >= blk_start[None, :]) & (b < blk_end[None, :])
    used = jnp.any(mine, axis=1)
    pick = lambda table: jnp.sum(jnp.where(mine, table[None, :], 0), axis=1)
    row_in_expert = (b[:, 0] - pick(blk_start)) * MOE_BM
    block_expert = jnp.where(used, pick(jnp.arange(N_EXPERTS, dtype=jnp.int32)), N_EXPERTS - 1)
    block_rows = jnp.where(used, jnp.clip(pick(counts) - row_in_expert, 0, MOE_BM), 0)
    block_row0 = jnp.where(used, pick(start) + row_in_expert, 0)
    return order, block_expert.astype(jnp.int32), block_row0.astype(jnp.int32), block_rows.astype(jnp.int32)


def _moe_sum(y_ref, w_ref):
    y, w = y_ref[0], w_ref[0]
    return w[:, 0:1] * y[:, :D_MODEL] + w[:, 1:2] * y[:, D_MODEL:]


def _combine_kernel(h_ref, y_ref, w_ref, mod_ref, o_ref):
    o_ref[0] = h_ref[0] + mod_ref[0, 0, 0, 5:6, :] * _moe_sum(y_ref, w_ref)


def _final_kernel(h_ref, y_ref, w_ref, mod_ref, g_ref, o_ref):
    h2 = h_ref[0] + mod_ref[0, 0, 0, 5:6, :] * _moe_sum(y_ref, w_ref)
    o_ref[0] = _rms(h2, g_ref[...])


def _combine(l, n_tiles, h1, y, w, mods, final_g=None):
    rows = n_tiles * TM
    tile = pl.BlockSpec((1, TM, D_MODEL), lambda b, i: (b, i, 0))
    in_specs = [tile, pl.BlockSpec((1, TM, TOP_K * D_MODEL), lambda b, i: (b, i, 0)),
                pl.BlockSpec((1, TM, TOP_K), lambda b, i: (b, i, 0)),
                pl.BlockSpec((1, 1, 1, N_MOD, D_MODEL), lambda b, i: (l, b, i // N_LAT_TILES, 0, 0))]
    args = [h1, y, w, mods]
    kern = _combine_kernel
    if final_g is not None:
        in_specs.append(pl.BlockSpec((1, D_MODEL), lambda b, i: (0, 0)))
        args.append(final_g.reshape(1, D_MODEL))
        kern = _final_kernel
    return pl.pallas_call(
        kern,
        out_shape=jax.ShapeDtypeStruct((BATCH, rows, D_MODEL), F32),
        grid=(BATCH, n_tiles),
        in_specs=in_specs,
        out_specs=tile,
        compiler_params=_cp("parallel", "parallel"),
        name="combine",
    )(*args)


def _rope_tables():
    n = jnp.arange(SEQ)
    pos = jnp.stack([n // GRID_W, n % GRID_W], axis=-1).astype(F32)
    inv = ROPE_BASE ** (-jnp.arange(ROPE_PAIRS, dtype=F32) / ROPE_PAIRS)
    ang = pos[:, :, None] * inv
    cos, sin = jnp.cos(ang), jnp.sin(ang)
    cos32 = jnp.stack([cos, cos], axis=2).reshape(SEQ, QK_ROPE)
    sin32 = jnp.stack([-sin, sin], axis=2).reshape(SEQ, QK_ROPE)
    one = jnp.ones((SEQ, QK_ROPE), F32)
    zero = jnp.zeros((SEQ, QK_ROPE), F32)
    cosq = jnp.concatenate([jnp.ones((SEQ, QK_NOPE), F32), cos32, one], axis=1) * Q_SCALE
    sinq = jnp.concatenate([jnp.zeros((SEQ, QK_NOPE), F32), sin32, zero], axis=1) * Q_SCALE
    cosk = jnp.concatenate([jnp.zeros((SEQ, QK_NOPE), F32), cos32, zero], axis=1)
    sink = jnp.concatenate([jnp.zeros((SEQ, QK_NOPE), F32), sin32, zero], axis=1)
    lat = jnp.stack([cosq, sinq, cosk, sink])
    ctx_row = jnp.ones((HEAD_PAD,), F32) * Q_SCALE
    ctx_cosk = jnp.concatenate([jnp.zeros((QK_NOPE + QK_ROPE,), F32), jnp.ones((QK_ROPE,), F32)])
    ctx = jnp.stack([ctx_row, jnp.zeros_like(ctx_row), ctx_cosk, jnp.zeros_like(ctx_row)])
    ctx = jnp.broadcast_to(ctx[:, None, :], (4, CTX_LEN, HEAD_PAD))
    return jnp.concatenate([lat, ctx], axis=1)


def _dft_matrices(L):
    N = 2 * L
    nfb = L // FB
    t = jnp.arange(L, dtype=jnp.int32)

    def tables(n_rows, step):
        ang = ((jnp.arange(n_rows, dtype=jnp.int32)[:, None] * step * t[None, :]) % N).astype(F32)
        ang = ang * (2.0 * math.pi / N)
        return jnp.cos(ang), jnp.sin(ang)

    ca, sa = tables(nfb, FB)
    cb, sb = tables(FB, 1)
    alt = jnp.where((t % 2) == 0, 1.0, -1.0).astype(F32)
    i_idx = jnp.arange(nfb)[:, None, None]
    r_idx = jnp.arange(FB)[None, :, None]
    dc = (i_idx == 0) & (r_idx == 0)
    c = ca[:, None, :] * cb[None] - sa[:, None, :] * sb[None]
    s = jnp.where(dc, alt, -(sa[:, None, :] * cb[None] + ca[:, None, :] * sb[None]))
    f = jnp.stack([c, s], axis=1).astype(BF16)
    wk = jnp.where(dc, 1.0 / N, 2.0 / N).astype(F32)[..., 0]
    cat, sat, cbt, sbt = ca.T[:, :, None], sa.T[:, :, None], cb.T[:, None, :], sb.T[:, None, :]
    ct = (cat * cbt - sat * sbt) * wk
    st = jnp.where(dc[..., 0], alt[:, None, None], -(sat * cbt + cat * sbt)) * wk
    g = jnp.stack([ct, st], axis=2).reshape(L, 2 * L).astype(BF16)
    return f, g


def _filter_embedding(L):
    t = jnp.linspace(0.0, 1.0, L, dtype=F32)[:, None]
    w = 2.0 * math.pi * jnp.arange(L, dtype=F32)[:, None] / L
    bands = jnp.linspace(1e-4, HY_BANDS - 1, HY_BANDS, dtype=F32)[None, :]
    return jnp.concatenate([t, jnp.cos(bands * w), -jnp.sin(bands * w),
                            jnp.zeros((L, LANES - HY_EMB), F32)], axis=-1)


def _rope_swap_perm():
    idx = np.arange(QK_ROPE)
    axis, half, pair = idx // (2 * ROPE_PAIRS), (idx // ROPE_PAIRS) % 2, idx % ROPE_PAIRS
    return axis * 2 * ROPE_PAIRS + (1 - half) * ROPE_PAIRS + pair


def _prep_weights(w_in, w_uq, w_ukv, w_hy_o, w_mla_o, w_out, w_router_g, w_router_e):
    perm = _rope_swap_perm()
    c_q = HY_COLS
    c_kv = HY_COLS + Q_LORA
    c_kr = c_kv + KV_LORA
    w_kr = w_in[:, :, c_kr:MLA_END]
    zeros = lambda n: jnp.zeros((DEPTH, D_MODEL, n), F32)
    wkr = jnp.concatenate([zeros(QK_NOPE), w_kr, w_kr,
                           zeros(QK_NOPE), w_kr[:, :, perm], zeros(QK_ROPE)], axis=-1)
    uq = w_uq.reshape(DEPTH, Q_LORA, MLA_HEADS, QK_NOPE + QK_ROPE)
    uq_r = uq[..., QK_NOPE:]
    wuq = jnp.concatenate([uq, uq_r], axis=-1).reshape(DEPTH, Q_LORA, MLA_HEADS * HEAD_PAD)
    wuqs = jnp.concatenate([jnp.zeros_like(uq[..., :QK_NOPE]), uq_r[..., perm], jnp.zeros_like(uq_r)],
                           axis=-1).reshape(DEPTH, Q_LORA, MLA_HEADS * HEAD_PAD)
    ukv = w_ukv.reshape(DEPTH, KV_LORA, MLA_HEADS, QK_NOPE + V_DIM)
    zpad = jnp.zeros_like(ukv[..., :HEAD_PAD - QK_NOPE])
    wuk = jnp.concatenate([ukv[..., :QK_NOPE], zpad], axis=-1).reshape(DEPTH, KV_LORA, MLA_HEADS * HEAD_PAD)
    wuv = jnp.concatenate([ukv[..., QK_NOPE:], zpad], axis=-1).reshape(DEPTH, KV_LORA, MLA_HEADS * HEAD_PAD)
    wr = jnp.concatenate([w_router_g, w_router_e,
                          jnp.zeros((DEPTH, D_MODEL, ROUTER_PAD - N_GROUPS - N_EXPERTS), F32)], axis=-1)
    bf = lambda a: a.astype(BF16)
    return dict(
        why=bf(w_in[:, :, :HY_COLS]), wgt=bf(w_in[:, :, MLA_END:]),
        wq=bf(w_in[:, :, c_q:c_kv]), wkv=bf(w_in[:, :, c_kv:c_kr]), wkr=bf(wkr),
        wuq=bf(wuq), wuqs=bf(wuqs), wuk=bf(wuk), wuv=bf(wuv),
        who=bf(w_hy_o), wmo=bf(w_mla_o), wout=bf(w_out), wr=wr,
    )


def _hyena(l, L, row0, u, fmat, gmat, zemb, deltas, flt, flt_bias):
    seg = fmat.shape[-1]
    spec = _spectrum(L, seg, fmat, _filters(l, L, zemb, deltas, flt))
    rb = row0 // L
    yf = _hy_fwd(L, seg, u, rb, 0, fmat, spec, 0)
    z1 = _hy_inv(l, L, seg, yf, gmat, u, row0, 1, u, row0, 0, flt_bias, 0)
    yf = _hy_fwd(L, seg, z1, 0, 0, fmat, spec, 1)
    return _hy_inv(l, L, seg, yf, gmat, u, row0, 2, z1, 0, 0, flt_bias, 1)


def kernel(x, c, ctx, c_ctx, w_ada, b_ada, norm1_g, w_in, q_norm_g, kv_norm_g, w_uq, w_ukv,
           sc_w, sc_b, flt_w1, flt_b1, flt_freq, flt_w2, flt_b2, flt_w3, flt_bias,
           w_hy_o, w_mla_o, w_out, norm2_g, w_router_g, b_router_g, w_router_e, b_router_e,
           w_gate_e, w_up_e, w_down_e, final_g):
    wts = _prep_weights(w_in, w_uq, w_ukv, w_hy_o, w_mla_o, w_out, w_router_g, w_router_e)
    tabs = _rope_tables()
    f_lat, g_lat = _dft_matrices(min(SEQ, HY_SEG))
    f_ctx, g_ctx = _dft_matrices(min(CTX_LEN, HY_SEG))
    z_lat, z_ctx = _filter_embedding(SEQ), _filter_embedding(CTX_LEN)
    deltas = jnp.linspace(math.log(HY_TARGET) / HY_FAST_DECAY, math.log(HY_TARGET) / HY_SLOW_DECAY,
                          HY_WIDTH, dtype=F32)[None, :]

    cc = jnp.concatenate([c, c_ctx[None, :], jnp.zeros((3, D_MODEL), F32)], axis=0)
    mod = _ada(cc, w_ada, b_ada).reshape(DEPTH, 8, N_MOD, D_MODEL)
    mods = jnp.stack([mod[:, :BATCH], jnp.broadcast_to(mod[:, BATCH:BATCH + 1], (DEPTH, BATCH, N_MOD, D_MODEL))],
                     axis=2)

    r3 = lambda a: a.reshape(DEPTH, 1, a.shape[-1])
    norm1, norm2, qg, kvg = r3(norm1_g), r3(norm2_g), r3(q_norm_g), r3(kv_norm_g)
    scb = r3(sc_b)
    fb1, ffr, fb2 = r3(flt_b1), r3(flt_freq), r3(flt_b2)
    fbias = flt_bias.reshape(DEPTH, HY_ORDER, 1, HY_WIDTH)
    w1_pad = jnp.concatenate([flt_w1, jnp.zeros((DEPTH, LANES - HY_EMB, HY_FFN), F32)], axis=1)
    flt = (w1_pad, fb1, ffr, flt_w2, fb2, flt_w3)

    b_router = jnp.concatenate([b_router_g, b_router_e,
                                jnp.zeros((DEPTH, ROUTER_PAD - N_GROUPS - N_EXPERTS), F32)], axis=-1)
    b_router = b_router.reshape(DEPTH, 1, ROUTER_PAD)

    h = jnp.concatenate([x, ctx], axis=1)
    for l in range(DEPTH):
        last = l == DEPTH - 1
        n_tiles = N_LAT_TILES if last else N_ALL_TILES
        p_hy, sg, q, k, v = _inproj(l, h, mods, norm1, tabs, qg, kvg, wts)

        u = _short_conv(l, p_hy, sc_w, scb)
        att_lat = _attention(q, k, v, latent=True)
        hy_lat = _hyena(l, SEQ, 0, u, f_lat, g_lat, z_lat, deltas, flt, fbias)
        if last:
            att_ctx, hy_ctx = att_lat, hy_lat
        else:
            att_ctx = _attention(q, k, v, latent=False)
            hy_ctx = _hyena(l, CTX_LEN, SEQ, u, f_ctx, g_ctx, z_ctx, deltas, flt, fbias)

        h1, u2, eid, ew, hist = _merge(l, n_tiles, h, sg, hy_lat, hy_ctx, att_lat, att_ctx, mods, norm2,
                                       wts["who"], wts["wmo"], wts["wout"], wts["wr"], b_router)
        counts = jnp.sum(hist, axis=(0, 1, 2))[:N_EXPERTS].astype(jnp.int32)

        T = BATCH * n_tiles * TM
        order, block_expert, block_row0, block_rows = _dispatch(eid.reshape(T * TOP_K), counts)
        y = _experts(l, block_expert, block_row0, block_rows, order, u2.reshape(T, D_MODEL),
                     w_gate_e, w_up_e, w_down_e)
        h = _combine(l, n_tiles, h1, y.reshape(BATCH, n_tiles * TM, TOP_K * D_MODEL), ew, mods,
                     final_g if last else None)
    return h
```

```python
import functools
import math

import jax
import jax.numpy as jnp
import numpy as np
from jax import lax
from jax.experimental import pallas as pl
from jax.experimental.pallas import tpu as pltpu

F32 = jnp.float32
BF16 = jnp.bfloat16

D_MODEL = 1024
BATCH = 4
SEQ = 4096
DEPTH = 4
GRID_W = 64
CTX_LEN = 256
S_ALL = SEQ + CTX_LEN
EPS = 1e-6
N_MOD = 6

HY_WIDTH = 512
HY_ORDER = 2
HY_BANDS = 16
HY_EMB = 1 + 2 * HY_BANDS
HY_FFN = 64
SHORT_K = 3
HY_FAST_DECAY = 0.3
HY_SLOW_DECAY = 1.5
HY_TARGET = 1e-2

MLA_HEADS = 8
QK_NOPE = 64
QK_ROPE = 32
V_DIM = 64
Q_LORA = 384
KV_LORA = 256
ROPE_PAIRS = QK_ROPE // 4
ROPE_BASE = 10000.0
ATTN_SCALE = (QK_NOPE + QK_ROPE) ** -0.5
Q_SCALE = ATTN_SCALE * math.log2(math.e)

N_GROUPS = 8
EXP_PER_GROUP = 8
N_EXPERTS = N_GROUPS * EXP_PER_GROUP
TOP_K = 2
D_EXPERT = 256

HY_COLS = (HY_ORDER + 1) * HY_WIDTH
MLA_END = HY_COLS + Q_LORA + KV_LORA + QK_ROPE

LANES = 128
HEAD_PAD = LANES
VMEM_LIMIT = 56 << 20

TM = 256
N_LAT_TILES = SEQ // TM
N_ALL_TILES = S_ALL // TM
TQ = 512
TK = 2048
FB = 256
HY_SEG = 1024
MOE_BM = 256
SUBLANES = 8
ROUTER_PAD = LANES


def _cp(*sem):
    return pltpu.CompilerParams(dimension_semantics=sem, vmem_limit_bytes=VMEM_LIMIT)


def _dot(a, b):
    return jnp.dot(a, b, preferred_element_type=F32)


def _split(a):
    hi = a.astype(BF16)
    lo = (a - hi.astype(F32)).astype(BF16)
    return hi, lo


def _dot3(a, b):
    ah, al = _split(a)
    bh, bl = _split(b)
    return _dot(ah, bh) + (_dot(ah, bl) + _dot(al, bh))


def _rms(x, g):
    return x * lax.rsqrt(jnp.mean(x * x, axis=-1, keepdims=True) + EPS) * g


def _ada_kernel(c_ref, w_ref, b_ref, o_ref):
    cc = c_ref[...]
    s = cc * jax.nn.sigmoid(cc)
    o_ref[0] = _dot3(s, w_ref[0]) + b_ref[0]


def _ada(cc, w_ada, b_ada):
    tn = 1536
    n = N_MOD * D_MODEL
    return pl.pallas_call(
        _ada_kernel,
        out_shape=jax.ShapeDtypeStruct((DEPTH, 8, n), F32),
        grid=(DEPTH, n // tn),
        in_specs=[
            pl.BlockSpec((8, D_MODEL), lambda l, j: (0, 0)),
            pl.BlockSpec((1, D_MODEL, tn), lambda l, j: (l, 0, j)),
            pl.BlockSpec((1, 1, tn), lambda l, j: (l, 0, j)),
        ],
        out_specs=pl.BlockSpec((1, 8, tn), lambda l, j: (l, 0, j)),
        compiler_params=_cp("parallel", "parallel"),
        name="ada",
    )(cc, w_ada, b_ada.reshape(DEPTH, 1, n))


def _inproj_kernel(h_ref, mod_ref, g_ref, tab_ref, qg_ref, kvg_ref,
                   why_ref, wgt_ref, wq_ref, wkv_ref, wkr_ref, wuq_ref, wuqs_ref, wuk_ref, wuv_ref,
                   phy_ref, sg_ref, q_ref, k_ref, v_ref):
    h = h_ref[0]
    shift = mod_ref[0, 0, 0, 0:1, :]
    scale = mod_ref[0, 0, 0, 1:2, :]
    u = (_rms(h, g_ref[0]) * (1.0 + scale) + shift).astype(BF16)

    phy_ref[0] = _dot(u, why_ref[0])
    sg_ref[0] = jax.nn.sigmoid(_dot(u, wgt_ref[0])).astype(BF16)

    cosq, sinq, cosk, sink = tab_ref[0], tab_ref[1], tab_ref[2], tab_ref[3]
    cq = _rms(_dot(u, wq_ref[0]), qg_ref[0]).astype(BF16)
    qa = _dot(cq, wuq_ref[0])
    qs = _dot(cq, wuqs_ref[0])
    ckv = _rms(_dot(u, wkv_ref[0]), kvg_ref[0]).astype(BF16)
    ka = _dot(ckv, wuk_ref[0])
    va = _dot(ckv, wuv_ref[0])
    kr = _dot(u, wkr_ref[0])
    krk = kr[:, :HEAD_PAD] * cosk + kr[:, HEAD_PAD:] * sink
    ones_col = (lax.broadcasted_iota(jnp.int32, (1, HEAD_PAD), 1) == V_DIM).astype(F32)
    for hh in range(MLA_HEADS):
        sl = slice(hh * HEAD_PAD, (hh + 1) * HEAD_PAD)
        q_ref[0, hh] = (qa[:, sl] * cosq + qs[:, sl] * sinq).astype(BF16)
        k_ref[0, hh] = (ka[:, sl] + krk).astype(BF16)
        v_ref[0, hh] = (va[:, sl] + ones_col).astype(BF16)


def _inproj(l, h, mods, norm1_g, tabs, q_norm_g, kv_norm_g, wts):
    names = ("why", "wgt", "wq", "wkv", "wkr", "wuq", "wuqs", "wuk", "wuv")
    w_specs = [pl.BlockSpec((1,) + wts[n].shape[1:], lambda b, i: (l, 0, 0)) for n in names]
    hd = MLA_HEADS * HEAD_PAD
    qkv_shape = jax.ShapeDtypeStruct((BATCH, MLA_HEADS, S_ALL, HEAD_PAD), BF16)
    qkv_spec = pl.BlockSpec((1, MLA_HEADS, TM, HEAD_PAD), lambda b, i: (b, 0, i, 0))
    del hd
    return pl.pallas_call(
        _inproj_kernel,
        out_shape=(
            jax.ShapeDtypeStruct((BATCH, S_ALL, HY_COLS), F32),
            jax.ShapeDtypeStruct((BATCH, S_ALL, 2 * D_MODEL), BF16),
            qkv_shape, qkv_shape, qkv_shape,
        ),
        grid=(BATCH, N_ALL_TILES),
        in_specs=[
            pl.BlockSpec((1, TM, D_MODEL), lambda b, i: (b, i, 0)),
            pl.BlockSpec((1, 1, 1, N_MOD, D_MODEL), lambda b, i: (l, b, i // N_LAT_TILES, 0, 0)),
            pl.BlockSpec((1, 1, D_MODEL), lambda b, i: (l, 0, 0)),
            pl.BlockSpec((4, TM, HEAD_PAD), lambda b, i: (0, i, 0)),
            pl.BlockSpec((1, 1, Q_LORA), lambda b, i: (l, 0, 0)),
            pl.BlockSpec((1, 1, KV_LORA), lambda b, i: (l, 0, 0)),
        ] + w_specs,
        out_specs=(
            pl.BlockSpec((1, TM, HY_COLS), lambda b, i: (b, i, 0)),
            pl.BlockSpec((1, TM, 2 * D_MODEL), lambda b, i: (b, i, 0)),
            qkv_spec, qkv_spec, qkv_spec,
        ),
        compiler_params=_cp("parallel", "parallel"),
        name="inproj",
    )(h, mods, norm1_g, tabs, q_norm_g, kv_norm_g, *[wts[n] for n in names])


def _flash_step(q_ref, k_ref, v_ref, rows, carry):
    out = []
    for hh in range(2):
        m, acc = carry[hh]
        s = lax.dot_general(q_ref[0, hh], k_ref[0, hh, rows, :], (((1,), (1,)), ((), ())),
                            preferred_element_type=F32)
        m_new = jnp.maximum(m, jnp.max(s, axis=-1, keepdims=True))
        p = jnp.exp2(s - m_new)
        acc = acc * jnp.exp2(m - m_new) + _dot(p.astype(BF16), v_ref[0, hh, rows, :])
        out.append((m_new, acc))
    return tuple(out)


def _attn_kernel(q_ref, k_ref, v_ref, o_ref, *, chunks):
    tq = q_ref.shape[2]
    init = (jnp.full((tq, 1), -1e30, F32), jnp.zeros((tq, HEAD_PAD), F32))
    carry = (init, init)
    start = 0
    for size in chunks:
        carry = _flash_step(q_ref, k_ref, v_ref, pl.ds(start, size), carry)
        start += size
    outs = [acc[:, :V_DIM] / acc[:, V_DIM:V_DIM + 1] for _, acc in carry]
    o_ref[0] = jnp.concatenate(outs, axis=-1).astype(BF16)


def _attention(q, k, v, latent):
    if latent:
        tq, nq, q0, kv_rows, kv_blk = TQ, SEQ // TQ, 0, S_ALL, 0
        kern = functools.partial(_attn_kernel, chunks=(TK,) * (SEQ // TK - 1) + (TK + CTX_LEN,))
    else:
        tq, nq, q0, kv_rows, kv_blk = CTX_LEN, 1, SEQ // CTX_LEN, CTX_LEN, SEQ // CTX_LEN
        kern = functools.partial(_attn_kernel, chunks=(CTX_LEN,))
    return pl.pallas_call(
        kern,
        out_shape=jax.ShapeDtypeStruct((BATCH, nq * tq, MLA_HEADS * V_DIM), BF16),
        grid=(BATCH, MLA_HEADS // 2, nq),
        in_specs=[
            pl.BlockSpec((1, 2, tq, HEAD_PAD), lambda b, hp, i: (b, hp, q0 + i, 0)),
            pl.BlockSpec((1, 2, kv_rows, HEAD_PAD), lambda b, hp, i: (b, hp, kv_blk, 0)),
            pl.BlockSpec((1, 2, kv_rows, HEAD_PAD), lambda b, hp, i: (b, hp, kv_blk, 0)),
        ],
        out_specs=pl.BlockSpec((1, tq, 2 * V_DIM), lambda b, hp, i: (b, i, hp)),
        compiler_params=_cp("parallel", "parallel", "parallel"),
        name="attn_lat" if latent else "attn_ctx",
    )(q, k, v)


def _short_conv_kernel(p_ref, w_ref, b_ref, o_ref):
    x = p_ref[0]
    rows = lax.broadcasted_iota(jnp.int32, (S_ALL, 1), 0)
    first = (rows == 0) | (rows == SEQ)
    last = (rows == SEQ - 1) | (rows == S_ALL - 1)
    prev = jnp.where(first, 0.0, pltpu.roll(x, 1, 0))
    nxt = jnp.where(last, 0.0, pltpu.roll(x, S_ALL - 1, 0))
    w = w_ref[0]
    o_ref[0] = b_ref[0] + prev * w[0:1] + x * w[1:2] + nxt * w[2:3]


def _short_conv(l, p_hy, sc_w, sc_b):
    cb = LANES
    return pl.pallas_call(
        _short_conv_kernel,
        out_shape=jax.ShapeDtypeStruct(p_hy.shape, F32),
        grid=(BATCH, HY_COLS // cb),
        in_specs=[
            pl.BlockSpec((1, S_ALL, cb), lambda b, j: (b, 0, j)),
            pl.BlockSpec((1, SHORT_K, cb), lambda b, j: (l, 0, j)),
            pl.BlockSpec((1, 1, cb), lambda b, j: (l, 0, j)),
        ],
        out_specs=pl.BlockSpec((1, S_ALL, cb), lambda b, j: (b, 0, j)),
        compiler_params=_cp("parallel", "parallel"),
        name="short_conv",
    )(p_hy, sc_w, sc_b)


def _filter_kernel(z_ref, w1_ref, b1_ref, fr_ref, w2_ref, b2_ref, w3f_ref, w3b_ref, dl_ref, hfb_ref, a_ref):
    z = z_ref[...]

    @pl.when((pl.program_id(0) == 0) & (pl.program_id(1) == 0))
    def _():
        fr = fr_ref[0]
        a1 = jnp.sin(fr * (_dot3(z, w1_ref[0]) + b1_ref[0]))
        a_ref[...] = jnp.sin(fr * (_dot3(a1, w2_ref[0]) + b2_ref[0]))

    a = a_ref[...]
    decay = jnp.exp(-z[:, 0:1] * jnp.abs(dl_ref[...]))
    rows = lax.broadcasted_iota(jnp.int32, (z.shape[0], 1), 0)

    def one_direction(w3_ref):
        h = _dot3(a, w3_ref[0]) * decay
        return h * lax.rsqrt(jnp.sum(h * h, axis=0, keepdims=True) + EPS)

    hfb_ref[0] = one_direction(w3f_ref).astype(BF16)
    hfb_ref[1] = jnp.where(rows == 0, 0.0, one_direction(w3b_ref)).astype(BF16)


def _filters(l, L, zemb, deltas, flt):
    w1, b1, fr, w2, b2, w3 = flt
    ncb = HY_WIDTH // LANES
    full = lambda *shape: pl.BlockSpec((1,) + shape, lambda o, j: (l,) + (0,) * len(shape))
    return pl.pallas_call(
        _filter_kernel,
        out_shape=jax.ShapeDtypeStruct((2, L, HY_ORDER * HY_WIDTH), BF16),
        grid=(HY_ORDER, ncb),
        in_specs=[
            pl.BlockSpec((L, LANES), lambda o, j: (0, 0)),
            full(LANES, HY_FFN), full(1, HY_FFN), full(1, HY_FFN),
            full(HY_FFN, HY_FFN), full(1, HY_FFN),
            pl.BlockSpec((1, HY_FFN, LANES), lambda o, j: (l, 0, o * 2 * ncb + j)),
            pl.BlockSpec((1, HY_FFN, LANES), lambda o, j: (l, 0, o * 2 * ncb + ncb + j)),
            pl.BlockSpec((1, LANES), lambda o, j: (0, j)),
        ],
        out_specs=pl.BlockSpec((2, L, LANES), lambda o, j: (0, 0, o * ncb + j)),
        scratch_shapes=[pltpu.VMEM((L, HY_FFN), F32)],
        compiler_params=_cp("arbitrary", "arbitrary"),
        name="hy_filter",
    )(zemb, w1, b1, fr, w2, b2, w3, w3, deltas)


def _packed_row0(block):
    return (lax.broadcasted_iota(jnp.int32, (FB, 1), 0) == 0) & (block == 0)


def _packed_mac(acc, c, z):
    (ar, ai), (cr, ci), (zr, zi) = acc, c, z
    return ar + (zr * cr - zi * ci), ai + (zr * ci + zi * cr)


def _spec_kernel(f_ref, h_ref, o_ref, *, seg, n_seg):
    fblk = pl.program_id(1)
    fr, fi = f_ref[0, 0], f_ref[0, 1]
    row0 = _packed_row0(fblk)
    sign = jnp.where((lax.broadcasted_iota(jnp.int32, (FB, 1), 0) & 1) == 0, 1.0, -1.0)

    def first_half(d, m):
        taps = h_ref[d, m * seg:(m + 1) * seg, :]
        return _dot(fr, taps), _dot(fi, taps), taps[0:1, :].astype(F32)

    def causal_block(d, m, halves):
        re, im, _ = halves[m]
        if m >= 1:
            pre, pim, tap0 = halves[m - 1]
            re = re + sign * (pre - tap0)
            im = im + sign * (pim - jnp.where(row0, tap0, 0.0))
        return re, im

    def conj(re, im):
        return re, jnp.where(row0, im, -im)

    halves = [[first_half(d, m) for m in range(n_seg)] for d in range(2)]
    for li in range(2 * n_seg - 1):
        lag = li - (n_seg - 1)
        if lag > 0:
            re, im = causal_block(0, lag, halves[0])
        elif lag < 0:
            re, im = conj(*causal_block(1, -lag, halves[1]))
        else:
            fre, fim = causal_block(0, 0, halves[0])
            gre, gim = conj(*causal_block(1, 0, halves[1]))
            re, im = fre + gre, fim + gim
        o_ref[0, 0, li, 0] = re
        o_ref[0, 0, li, 1] = im


def _spectrum(L, seg, fmat, hfb):
    n_seg, nfb = L // seg, seg // FB
    n_lag = 2 * n_seg - 1
    return pl.pallas_call(
        functools.partial(_spec_kernel, seg=seg, n_seg=n_seg),
        out_shape=jax.ShapeDtypeStruct((HY_ORDER, nfb, n_lag, 2, FB, HY_WIDTH), F32),
        grid=(HY_ORDER, nfb),
        in_specs=[
            pl.BlockSpec((1, 2, FB, seg), lambda o, f: (f, 0, 0, 0)),
            pl.BlockSpec((2, L, HY_WIDTH), lambda o, f: (0, 0, o)),
        ],
        out_specs=pl.BlockSpec((1, 1, n_lag, 2, FB, HY_WIDTH), lambda o, f: (o, f, 0, 0, 0, 0)),
        compiler_params=_cp("parallel", "parallel"),
        name="hy_spectrum",
    )(fmat, hfb)


def _fwd_kernel(z_ref, f_ref, c_ref, y_ref, zb_ref, *, seg, n_seg):
    fblk = pl.program_id(1)

    @pl.when(fblk == 0)
    def _():
        zb_ref[...] = z_ref[0].astype(BF16)

    fr, fi = f_ref[0, 0], f_ref[0, 1]
    row0 = _packed_row0(fblk)
    acc = [None] * n_seg
    edge = [None] * n_seg
    for j in range(n_seg):
        zj = zb_ref[j * seg:(j + 1) * seg, :]
        z = (_dot(fr, zj), _dot(fi, zj))
        for i in range(n_seg):
            c = (c_ref[0, 0, i - j + n_seg - 1, 0], c_ref[0, 0, i - j + n_seg - 1, 1])
            e = (z[0][0:1] * c[0][0:1], z[1][0:1] * c[1][0:1])
            if acc[i] is None:
                acc[i] = (z[0] * c[0] - z[1] * c[1], z[0] * c[1] + z[1] * c[0])
                edge[i] = e
            else:
                acc[i] = _packed_mac(acc[i], c, z)
                edge[i] = (edge[i][0] + e[0], edge[i][1] + e[1])
    for i in range(n_seg):
        y_ref[0, i, 0:FB] = jnp.where(row0, edge[i][0], acc[i][0]).astype(BF16)
        y_ref[0, i, FB:2 * FB] = jnp.where(row0, edge[i][1], acc[i][1]).astype(BF16)


def _hy_fwd(L, seg, z, z_rowblk, z_colblk, fmat, spec, order):
    n_seg, nfb = L // seg, seg // FB
    n_lag = 2 * n_seg - 1
    return pl.pallas_call(
        functools.partial(_fwd_kernel, seg=seg, n_seg=n_seg),
        out_shape=jax.ShapeDtypeStruct((BATCH, n_seg, 2 * seg, HY_WIDTH), BF16),
        grid=(BATCH, nfb),
        in_specs=[
            pl.BlockSpec((1, L, HY_WIDTH), lambda b, f: (b, z_rowblk, z_colblk)),
            pl.BlockSpec((1, 2, FB, seg), lambda b, f: (f, 0, 0, 0)),
            pl.BlockSpec((1, 1, n_lag, 2, FB, HY_WIDTH), lambda b, f: (order, f, 0, 0, 0, 0)),
        ],
        out_specs=pl.BlockSpec((1, n_seg, 2 * FB, HY_WIDTH), lambda b, f: (b, 0, f, 0)),
        scratch_shapes=[pltpu.VMEM((L, HY_WIDTH), BF16)],
        compiler_params=_cp("parallel", "arbitrary"),
        name="hy_fwd",
    )(z, fmat, spec)


def _inv_kernel(y_ref, g_ref, gate_ref, z_ref, bias_ref, o_ref):
    y = _dot(g_ref[...], y_ref[0, 0])
    o_ref[0] = gate_ref[0] * (y + bias_ref[0, 0] * z_ref[0])


def _hy_inv(l, L, seg, yf, gmat, u, row0, gate_colblk, z, z_row0, z_colblk, flt_bias, order):
    tmi = min(seg, 2 * TM)
    per_seg = seg // tmi
    r0, zr0 = row0 // tmi, z_row0 // tmi
    return pl.pallas_call(
        _inv_kernel,
        out_shape=jax.ShapeDtypeStruct((BATCH, L, HY_WIDTH), F32),
        grid=(BATCH, L // tmi),
        in_specs=[
            pl.BlockSpec((1, 1, 2 * seg, HY_WIDTH), lambda b, i: (b, i // per_seg, 0, 0)),
            pl.BlockSpec((tmi, 2 * seg), lambda b, i: (i % per_seg, 0)),
            pl.BlockSpec((1, tmi, HY_WIDTH), lambda b, i: (b, r0 + i, gate_colblk)),
            pl.BlockSpec((1, tmi, HY_WIDTH), lambda b, i: (b, zr0 + i, z_colblk)),
            pl.BlockSpec((1, 1, 1, HY_WIDTH), lambda b, i: (l, order, 0, 0)),
        ],
        out_specs=pl.BlockSpec((1, tmi, HY_WIDTH), lambda b, i: (b, i, 0)),
        compiler_params=_cp("parallel", "parallel"),
        name="hy_inv",
    )(yf, gmat, u, z, flt_bias)


def _route_tile(lg, bias):
    lane = lax.broadcasted_iota(jnp.int32, lg.shape, 1)
    lane_f = lane.astype(F32)
    neg = jnp.float32(-jnp.inf)
    big = jnp.float32(ROUTER_PAD)
    biased = lg + bias

    def first_argmax(v):
        m = jnp.max(v, axis=-1, keepdims=True)
        return jnp.min(jnp.where(v == m, lane_f, big), axis=-1, keepdims=True).astype(jnp.int32)

    def pick(v, idx):
        return jnp.sum(jnp.where(lane == idx, v, 0.0), axis=-1, keepdims=True)

    is_group = lane < N_GROUPS
    g_sel = first_argmax(jnp.where(is_group, biased, neg))
    raw_g = jnp.where(is_group, lg, neg)
    e_g = jnp.exp(raw_g - jnp.max(raw_g, axis=-1, keepdims=True))
    p_g = pick(e_g, g_sel) / jnp.sum(e_g, axis=-1, keepdims=True)

    lo = N_GROUPS + g_sel * EXP_PER_GROUP
    cand = jnp.where((lane >= lo) & (lane < lo + EXP_PER_GROUP), biased, neg)
    i1 = first_argmax(cand)
    i2 = first_argmax(jnp.where(lane == i1, neg, cand))
    l1, l2 = pick(lg, i1), pick(lg, i2)
    top = jnp.maximum(l1, l2)
    e1, e2 = jnp.exp(l1 - top), jnp.exp(l2 - top)
    inv = p_g / (e1 + e2)
    return (i1 - N_GROUPS, i2 - N_GROUPS), (e1 * inv, e2 * inv)


def _merge_kernel(h_ref, sg_ref, hyl_ref, hyc_ref, attl_ref, attc_ref, mod_ref, g_ref,
                  who_ref, wmo_ref, wout_ref, wr_ref, br_ref, h1_ref, u2_ref, eid_ref, ew_ref, cnt_ref):
    is_ctx = pl.program_id(1) >= N_LAT_TILES
    hy = jnp.where(is_ctx, hyc_ref[0], hyl_ref[0]).astype(BF16)
    att = jnp.where(is_ctx, attc_ref[0], attl_ref[0])
    sg = sg_ref[0].astype(F32)
    m = sg[:, :D_MODEL] * _dot(hy, who_ref[0]) + sg[:, D_MODEL:] * _dot(att, wmo_ref[0])
    y = _dot(m.astype(BF16), wout_ref[0])
    g1 = mod_ref[0, 0, 0, 2:3, :]
    sh2 = mod_ref[0, 0, 0, 3:4, :]
    sc2 = mod_ref[0, 0, 0, 4:5, :]
    h1 = h_ref[0] + g1 * y
    h1_ref[0] = h1
    u2 = _rms(h1, g_ref[0]) * (1.0 + sc2) + sh2
    u2_ref[0] = u2
    ids, ws = _route_tile(_dot3(u2, wr_ref[0]), br_ref[0])
    lane = lax.broadcasted_iota(jnp.int32, (TM, ROUTER_PAD), 1)
    hist = jnp.zeros((1, ROUTER_PAD), F32)
    for c in range(TOP_K):
        eid_ref[0, :, c:c + 1] = ids[c]
        ew_ref[0, :, c:c + 1] = ws[c]
        hist = hist + jnp.sum(jnp.where(lane == ids[c], 1.0, 0.0), axis=0, keepdims=True)
    cnt_ref[0, 0] = hist


def _merge(l, n_tiles, h, sg, hy_lat, hy_ctx, att_lat, att_ctx, mods, norm2_g, who, wmo, wout, wr, br):
    rows = n_tiles * TM
    tile = lambda w: pl.BlockSpec((1, TM, w), lambda b, i: (b, i, 0))
    lat = lambda w: pl.BlockSpec((1, TM, w), lambda b, i: (b, jnp.minimum(i, N_LAT_TILES - 1), 0))
    ctx = lambda w: pl.BlockSpec((1, TM, w), lambda b, i: (b, 0, 0))
    wfull = lambda a: pl.BlockSpec((1,) + a.shape[1:], lambda b, i: (l, 0, 0))
    return pl.pallas_call(
        _merge_kernel,
        out_shape=(
            jax.ShapeDtypeStruct((BATCH, rows, D_MODEL), F32),
            jax.ShapeDtypeStruct((BATCH, rows, D_MODEL), F32),
            jax.ShapeDtypeStruct((BATCH, rows, TOP_K), jnp.int32),
            jax.ShapeDtypeStruct((BATCH, rows, TOP_K), F32),
            jax.ShapeDtypeStruct((BATCH, n_tiles, 1, ROUTER_PAD), F32),
        ),
        grid=(BATCH, n_tiles),
        in_specs=[
            tile(D_MODEL), tile(2 * D_MODEL),
            lat(HY_WIDTH), ctx(HY_WIDTH), lat(MLA_HEADS * V_DIM), ctx(MLA_HEADS * V_DIM),
            pl.BlockSpec((1, 1, 1, N_MOD, D_MODEL), lambda b, i: (l, b, i // N_LAT_TILES, 0, 0)),
            pl.BlockSpec((1, 1, D_MODEL), lambda b, i: (l, 0, 0)),
            wfull(who), wfull(wmo), wfull(wout), wfull(wr), wfull(br),
        ],
        out_specs=(tile(D_MODEL), tile(D_MODEL), tile(TOP_K), tile(TOP_K),
                   pl.BlockSpec((1, 1, 1, ROUTER_PAD), lambda b, i: (b, i, 0, 0))),
        compiler_params=_cp("parallel", "parallel"),
        name="merge",
    )(h, sg, hy_lat, hy_ctx, att_lat, att_ctx, mods, norm2_g, who, wmo, wout, wr, br)


def _expert_kernel(be_ref, b0_ref, nr_ref, src_ref, dst_ref, u_hbm, wg_ref, wu_ref, wd_ref, y_hbm,
                   xbuf, ybuf, gsem, ssem):
    i = pl.program_id(0)
    nb = pl.num_programs(0)
    slot = i % 2

    def in_copy(blk, sl, g, j):
        tok = src_ref[b0_ref[blk] + g * SUBLANES + j]
        return pltpu.make_async_copy(u_hbm.at[pl.ds(tok, 1)], xbuf.at[sl, g, pl.ds(j, 1)], gsem.at[sl])

    def out_copy(blk, sl, g, j):
        r = g * SUBLANES + j
        return pltpu.make_async_copy(ybuf.at[sl, pl.ds(r, 1)], y_hbm.at[pl.ds(dst_ref[b0_ref[blk] + r], 1)],
                                     ssem.at[sl])

    def for_rows(blk, fn):
        n = nr_ref[blk]
        n_groups = n // SUBLANES

        def group(g, carry):
            for j in range(SUBLANES):
                fn(g, j)
            return carry

        def single(r, carry):
            fn(n_groups, r)
            return carry

        lax.fori_loop(0, n_groups, group, 0)
        lax.fori_loop(0, n - n_groups * SUBLANES, single, 0)

    @pl.when(i == 0)
    def _():
        xbuf[...] = jnp.zeros_like(xbuf)
        for_rows(0, lambda g, j: in_copy(0, 0, g, j).start())

    @pl.when(i + 1 < nb)
    def _():
        for_rows(i + 1, lambda g, j: in_copy(i + 1, 1 - slot, g, j).start())

    for_rows(i, lambda g, j: in_copy(i, slot, g, j).wait())

    @pl.when(i >= 2)
    def _():
        for_rows(i - 2, lambda g, j: out_copy(i - 2, slot, g, j).wait())

    @pl.when(nr_ref[i] > 0)
    def _():
        x = xbuf[slot].reshape(MOE_BM, D_MODEL).astype(BF16)
        hg = _dot(x, wg_ref[0, 0].astype(BF16))
        hu = _dot(x, wu_ref[0, 0].astype(BF16))
        hb = (hg * jax.nn.sigmoid(hg) * hu).astype(BF16)
        ybuf[slot] = _dot(hb, wd_ref[0, 0].astype(BF16))

    for_rows(i, lambda g, j: out_copy(i, slot, g, j).start())

    @pl.when(i == nb - 1)
    def _():
        for_rows(i - 1, lambda g, j: out_copy(i - 1, 1 - slot, g, j).wait())
        for_rows(i, lambda g, j: out_copy(i, slot, g, j).wait())


def _experts(l, block_expert, block_row0, block_rows, order, u, w_gate_e, w_up_e, w_down_e):
    n_blocks = block_expert.shape[0]
    n_tok = u.shape[0]
    src = lax.shift_right_logical(order, 1)
    dst = (order & 1) * n_tok + src
    wspec = lambda r, c: pl.BlockSpec((1, 1, r, c), lambda i, be, b0, nr, sr, ds: (l, be[i], 0, 0))
    grid_spec = pltpu.PrefetchScalarGridSpec(
        num_scalar_prefetch=5,
        grid=(n_blocks,),
        in_specs=[
            pl.BlockSpec(memory_space=pl.ANY),
            wspec(D_MODEL, D_EXPERT), wspec(D_MODEL, D_EXPERT), wspec(D_EXPERT, D_MODEL),
        ],
        out_specs=pl.BlockSpec(memory_space=pl.ANY),
        scratch_shapes=[
            pltpu.VMEM((2, MOE_BM // SUBLANES, SUBLANES, D_MODEL), F32),
            pltpu.VMEM((2, MOE_BM, D_MODEL), F32),
            pltpu.SemaphoreType.DMA((2,)),
            pltpu.SemaphoreType.DMA((2,)),
        ],
    )
    return pl.pallas_call(
        _expert_kernel,
        out_shape=jax.ShapeDtypeStruct((TOP_K * u.shape[0], D_MODEL), F32),
        grid_spec=grid_spec,
        compiler_params=_cp("arbitrary"),
        name="experts",
    )(block_expert, block_row0, block_rows, src, dst, u, w_gate_e, w_up_e, w_down_e)


def _dispatch(expert, counts):
    A = expert.shape[0]
    _, order = lax.sort((expert, jnp.arange(A, dtype=jnp.int32)), num_keys=1)
    start = jnp.cumsum(counts) - counts
    n_blk = (counts + MOE_BM - 1) // MOE_BM
    blk_end = jnp.cumsum(n_blk)
    blk_start = blk_end - n_blk
    n_blocks = -(-A // MOE_BM) + N_EXPERTS
    b = jnp.arange(n_blocks, dtype=jnp.int32)[:, None]
    mine = (b >= blk_start[None, :]) & (b < blk_end[None, :])
    used = jnp.any(mine, axis=1)
    pick = lambda table: jnp.sum(jnp.where(mine, table[None, :], 0), axis=1)
    row_in_expert = (b[:, 0] - pick(blk_start)) * MOE_BM
    block_expert = jnp.where(used, pick(jnp.arange(N_EXPERTS, dtype=jnp.int32)), N_EXPERTS - 1)
    block_rows = jnp.where(used, jnp.clip(pick(counts) - row_in_expert, 0, MOE_BM), 0)
    block_row0 = jnp.where(used, pick(start) + row_in_expert, 0)
    return order, block_expert.astype(jnp.int32), block_row0.astype(jnp.int32), block_rows.astype(jnp.int32)


def _moe_sum(y_ref, w_ref):
    w = w_ref[0]
    return sum(w[:, c:c + 1] * y_ref[c, 0] for c in range(TOP_K))


def _combine_kernel(h_ref, y_ref, w_ref, mod_ref, o_ref):
    o_ref[0] = h_ref[0] + mod_ref[0, 0, 0, 5:6, :] * _moe_sum(y_ref, w_ref)


def _final_kernel(h_ref, y_ref, w_ref, mod_ref, g_ref, o_ref):
    h2 = h_ref[0] + mod_ref[0, 0, 0, 5:6, :] * _moe_sum(y_ref, w_ref)
    o_ref[0] = _rms(h2, g_ref[...])


def _combine(l, n_tiles, h1, y, w, mods, final_g=None):
    rows = n_tiles * TM
    tile = pl.BlockSpec((1, TM, D_MODEL), lambda b, i: (b, i, 0))
    in_specs = [tile, pl.BlockSpec((TOP_K, 1, TM, D_MODEL), lambda b, i: (0, b, i, 0)),
                pl.BlockSpec((1, TM, TOP_K), lambda b, i: (b, i, 0)),
                pl.BlockSpec((1, 1, 1, N_MOD, D_MODEL), lambda b, i: (l, b, i // N_LAT_TILES, 0, 0))]
    args = [h1, y, w, mods]
    kern = _combine_kernel
    if final_g is not None:
        in_specs.append(pl.BlockSpec((1, D_MODEL), lambda b, i: (0, 0)))
        args.append(final_g.reshape(1, D_MODEL))
        kern = _final_kernel
    return pl.pallas_call(
        kern,
        out_shape=jax.ShapeDtypeStruct((BATCH, rows, D_MODEL), F32),
        grid=(BATCH, n_tiles),
        in_specs=in_specs,
        out_specs=tile,
        compiler_params=_cp("parallel", "parallel"),
        name="combine",
    )(*args)


def _rope_tables():
    n = jnp.arange(SEQ)
    pos = jnp.stack([n // GRID_W, n % GRID_W], axis=-1).astype(F32)
    inv = ROPE_BASE ** (-jnp.arange(ROPE_PAIRS, dtype=F32) / ROPE_PAIRS)
    ang = pos[:, :, None] * inv
    cos, sin = jnp.cos(ang), jnp.sin(ang)
    cos32 = jnp.stack([cos, cos], axis=2).reshape(SEQ, QK_ROPE)
    sin32 = jnp.stack([-sin, sin], axis=2).reshape(SEQ, QK_ROPE)
    one = jnp.ones((SEQ, QK_ROPE), F32)
    zero = jnp.zeros((SEQ, QK_ROPE), F32)
    cosq = jnp.concatenate([jnp.ones((SEQ, QK_NOPE), F32), cos32, one], axis=1) * Q_SCALE
    sinq = jnp.concatenate([jnp.zeros((SEQ, QK_NOPE), F32), sin32, zero], axis=1) * Q_SCALE
    cosk = jnp.concatenate([jnp.zeros((SEQ, QK_NOPE), F32), cos32, zero], axis=1)
    sink = jnp.concatenate([jnp.zeros((SEQ, QK_NOPE), F32), sin32, zero], axis=1)
    lat = jnp.stack([cosq, sinq, cosk, sink])
    ctx_row = jnp.ones((HEAD_PAD,), F32) * Q_SCALE
    ctx_cosk = jnp.concatenate([jnp.zeros((QK_NOPE + QK_ROPE,), F32), jnp.ones((QK_ROPE,), F32)])
    ctx = jnp.stack([ctx_row, jnp.zeros_like(ctx_row), ctx_cosk, jnp.zeros_like(ctx_row)])
    ctx = jnp.broadcast_to(ctx[:, None, :], (4, CTX_LEN, HEAD_PAD))
    return jnp.concatenate([lat, ctx], axis=1)


def _dft_matrices(L):
    N = 2 * L
    nfb = L // FB
    t = jnp.arange(L, dtype=jnp.int32)

    def tables(n_rows, step):
        ang = ((jnp.arange(n_rows, dtype=jnp.int32)[:, None] * step * t[None, :]) % N).astype(F32)
        ang = ang * (2.0 * math.pi / N)
        return jnp.cos(ang), jnp.sin(ang)

    ca, sa = tables(nfb, FB)
    cb, sb = tables(FB, 1)
    alt = jnp.where((t % 2) == 0, 1.0, -1.0).astype(F32)
    i_idx = jnp.arange(nfb)[:, None, None]
    r_idx = jnp.arange(FB)[None, :, None]
    dc = (i_idx == 0) & (r_idx == 0)
    c = ca[:, None, :] * cb[None] - sa[:, None, :] * sb[None]
    s = jnp.where(dc, alt, -(sa[:, None, :] * cb[None] + ca[:, None, :] * sb[None]))
    f = jnp.stack([c, s], axis=1).astype(BF16)
    wk = jnp.where(dc, 1.0 / N, 2.0 / N).astype(F32)[..., 0]
    cat, sat, cbt, sbt = ca.T[:, :, None], sa.T[:, :, None], cb.T[:, None, :], sb.T[:, None, :]
    ct = (cat * cbt - sat * sbt) * wk
    st = jnp.where(dc[..., 0], alt[:, None, None], -(sat * cbt + cat * sbt)) * wk
    g = jnp.stack([ct, st], axis=2).reshape(L, 2 * L).astype(BF16)
    return f, g


def _filter_embedding(L):
    t = jnp.linspace(0.0, 1.0, L, dtype=F32)[:, None]
    w = 2.0 * math.pi * jnp.arange(L, dtype=F32)[:, None] / L
    bands = jnp.linspace(1e-4, HY_BANDS - 1, HY_BANDS, dtype=F32)[None, :]
    return jnp.concatenate([t, jnp.cos(bands * w), -jnp.sin(bands * w),
                            jnp.zeros((L, LANES - HY_EMB), F32)], axis=-1)


def _rope_swap_perm():
    idx = np.arange(QK_ROPE)
    axis, half, pair = idx // (2 * ROPE_PAIRS), (idx // ROPE_PAIRS) % 2, idx % ROPE_PAIRS
    return axis * 2 * ROPE_PAIRS + (1 - half) * ROPE_PAIRS + pair


def _prep_weights(w_in, w_uq, w_ukv, w_hy_o, w_mla_o, w_out, w_router_g, w_router_e):
    perm = _rope_swap_perm()
    c_q = HY_COLS
    c_kv = HY_COLS + Q_LORA
    c_kr = c_kv + KV_LORA
    w_kr = w_in[:, :, c_kr:MLA_END]
    zeros = lambda n: jnp.zeros((DEPTH, D_MODEL, n), F32)
    wkr = jnp.concatenate([zeros(QK_NOPE), w_kr, w_kr,
                           zeros(QK_NOPE), w_kr[:, :, perm], zeros(QK_ROPE)], axis=-1)
    uq = w_uq.reshape(DEPTH, Q_LORA, MLA_HEADS, QK_NOPE + QK_ROPE)
    uq_r = uq[..., QK_NOPE:]
    wuq = jnp.concatenate([uq, uq_r], axis=-1).reshape(DEPTH, Q_LORA, MLA_HEADS * HEAD_PAD)
    wuqs = jnp.concatenate([jnp.zeros_like(uq[..., :QK_NOPE]), uq_r[..., perm], jnp.zeros_like(uq_r)],
                           axis=-1).reshape(DEPTH, Q_LORA, MLA_HEADS * HEAD_PAD)
    ukv = w_ukv.reshape(DEPTH, KV_LORA, MLA_HEADS, QK_NOPE + V_DIM)
    zpad = jnp.zeros_like(ukv[..., :HEAD_PAD - QK_NOPE])
    wuk = jnp.concatenate([ukv[..., :QK_NOPE], zpad], axis=-1).reshape(DEPTH, KV_LORA, MLA_HEADS * HEAD_PAD)
    wuv = jnp.concatenate([ukv[..., QK_NOPE:], zpad], axis=-1).reshape(DEPTH, KV_LORA, MLA_HEADS * HEAD_PAD)
    wr = jnp.concatenate([w_router_g, w_router_e,
                          jnp.zeros((DEPTH, D_MODEL, ROUTER_PAD - N_GROUPS - N_EXPERTS), F32)], axis=-1)
    bf = lambda a: a.astype(BF16)
    return dict(
        why=bf(w_in[:, :, :HY_COLS]), wgt=bf(w_in[:, :, MLA_END:]),
        wq=bf(w_in[:, :, c_q:c_kv]), wkv=bf(w_in[:, :, c_kv:c_kr]), wkr=bf(wkr),
        wuq=bf(wuq), wuqs=bf(wuqs), wuk=bf(wuk), wuv=bf(wuv),
        who=bf(w_hy_o), wmo=bf(w_mla_o), wout=bf(w_out), wr=wr,
    )


def _hyena(l, L, row0, u, fmat, gmat, zemb, deltas, flt, flt_bias):
    seg = fmat.shape[-1]
    spec = _spectrum(L, seg, fmat, _filters(l, L, zemb, deltas, flt))
    rb = row0 // L
    yf = _hy_fwd(L, seg, u, rb, 0, fmat, spec, 0)
    z1 = _hy_inv(l, L, seg, yf, gmat, u, row0, 1, u, row0, 0, flt_bias, 0)
    yf = _hy_fwd(L, seg, z1, 0, 0, fmat, spec, 1)
    return _hy_inv(l, L, seg, yf, gmat, u, row0, 2, z1, 0, 0, flt_bias, 1)


def kernel(x, c, ctx, c_ctx, w_ada, b_ada, norm1_g, w_in, q_norm_g, kv_norm_g, w_uq, w_ukv,
           sc_w, sc_b, flt_w1, flt_b1, flt_freq, flt_w2, flt_b2, flt_w3, flt_bias,
           w_hy_o, w_mla_o, w_out, norm2_g, w_router_g, b_router_g, w_router_e, b_router_e,
           w_gate_e, w_up_e, w_down_e, final_g):
    wts = _prep_weights(w_in, w_uq, w_ukv, w_hy_o, w_mla_o, w_out, w_router_g, w_router_e)
    tabs = _rope_tables()
    f_lat, g_lat = _dft_matrices(min(SEQ, HY_SEG))
    f_ctx, g_ctx = _dft_matrices(min(CTX_LEN, HY_SEG))
    z_lat, z_ctx = _filter_embedding(SEQ), _filter_embedding(CTX_LEN)
    deltas = jnp.linspace(math.log(HY_TARGET) / HY_FAST_DECAY, math.log(HY_TARGET) / HY_SLOW_DECAY,
                          HY_WIDTH, dtype=F32)[None, :]

    cc = jnp.concatenate([c, c_ctx[None, :], jnp.zeros((3, D_MODEL), F32)], axis=0)
    mod = _ada(cc, w_ada, b_ada).reshape(DEPTH, 8, N_MOD, D_MODEL)
    mods = jnp.stack([mod[:, :BATCH], jnp.broadcast_to(mod[:, BATCH:BATCH + 1], (DEPTH, BATCH, N_MOD, D_MODEL))],
                     axis=2)

    r3 = lambda a: a.reshape(DEPTH, 1, a.shape[-1])
    norm1, norm2, qg, kvg = r3(norm1_g), r3(norm2_g), r3(q_norm_g), r3(kv_norm_g)
    scb = r3(sc_b)
    fb1, ffr, fb2 = r3(flt_b1), r3(flt_freq), r3(flt_b2)
    fbias = flt_bias.reshape(DEPTH, HY_ORDER, 1, HY_WIDTH)
    w1_pad = jnp.concatenate([flt_w1, jnp.zeros((DEPTH, LANES - HY_EMB, HY_FFN), F32)], axis=1)
    flt = (w1_pad, fb1, ffr, flt_w2, fb2, flt_w3)

    b_router = jnp.concatenate([b_router_g, b_router_e,
                                jnp.zeros((DEPTH, ROUTER_PAD - N_GROUPS - N_EXPERTS), F32)], axis=-1)
    b_router = b_router.reshape(DEPTH, 1, ROUTER_PAD)

    h = jnp.concatenate([x, ctx], axis=1)
    for l in range(DEPTH):
        last = l == DEPTH - 1
        n_tiles = N_LAT_TILES if last else N_ALL_TILES
        p_hy, sg, q, k, v = _inproj(l, h, mods, norm1, tabs, qg, kvg, wts)

        u = _short_conv(l, p_hy, sc_w, scb)
        att_lat = _attention(q, k, v, latent=True)
        hy_lat = _hyena(l, SEQ, 0, u, f_lat, g_lat, z_lat, deltas, flt, fbias)
        if last:
            att_ctx, hy_ctx = att_lat, hy_lat
        else:
            att_ctx = _attention(q, k, v, latent=False)
            hy_ctx = _hyena(l, CTX_LEN, SEQ, u, f_ctx, g_ctx, z_ctx, deltas, flt, fbias)

        h1, u2, eid, ew, hist = _merge(l, n_tiles, h, sg, hy_lat, hy_ctx, att_lat, att_ctx, mods, norm2,
                                       wts["who"], wts["wmo"], wts["wout"], wts["wr"], b_router)
        counts = jnp.sum(hist, axis=(0, 1, 2))[:N_EXPERTS].astype(jnp.int32)

        T = BATCH * n_tiles * TM
        order, block_expert, block_row0, block_rows = _dispatch(eid.reshape(T * TOP_K), counts)
        y = _experts(l, block_expert, block_row0, block_rows, order, u2.reshape(T, D_MODEL),
                     w_gate_e, w_up_e, w_down_e)
        h = _combine(l, n_tiles, h1, y.reshape(TOP_K, BATCH, n_tiles * TM, D_MODEL), ew, mods,
                     final_g if last else None)
    return h
```

```python
import functools
import math

import jax
import jax.numpy as jnp
import numpy as np
from jax import lax
from jax.experimental import pallas as pl
from jax.experimental.pallas import tpu as pltpu

F32 = jnp.float32
BF16 = jnp.bfloat16

D_MODEL = 1024
BATCH = 4
SEQ = 4096
DEPTH = 4
GRID_W = 64
CTX_LEN = 256
S_ALL = SEQ + CTX_LEN
EPS = 1e-6
N_MOD = 6

HY_WIDTH = 512
HY_ORDER = 2
HY_BANDS = 16
HY_EMB = 1 + 2 * HY_BANDS
HY_FFN = 64
SHORT_K = 3
HY_FAST_DECAY = 0.3
HY_SLOW_DECAY = 1.5
HY_TARGET = 1e-2

MLA_HEADS = 8
QK_NOPE = 64
QK_ROPE = 32
V_DIM = 64
Q_LORA = 384
KV_LORA = 256
ROPE_PAIRS = QK_ROPE // 4
ROPE_BASE = 10000.0
ATTN_SCALE = (QK_NOPE + QK_ROPE) ** -0.5
Q_SCALE = ATTN_SCALE * math.log2(math.e)

N_GROUPS = 8
EXP_PER_GROUP = 8
N_EXPERTS = N_GROUPS * EXP_PER_GROUP
TOP_K = 2
D_EXPERT = 256

HY_COLS = (HY_ORDER + 1) * HY_WIDTH
MLA_END = HY_COLS + Q_LORA + KV_LORA + QK_ROPE

LANES = 128
HEAD_PAD = LANES
VMEM_LIMIT = 56 << 20

TM = 256
N_LAT_TILES = SEQ // TM
N_ALL_TILES = S_ALL // TM
TQ = 512
TK = 2048
FB = 256
HY_SEG = 1024
MOE_BM = 256
SUBLANES = 8
ROUTER_PAD = LANES


def _cp(*sem):
    return pltpu.CompilerParams(dimension_semantics=sem, vmem_limit_bytes=VMEM_LIMIT)


def _dot(a, b):
    return jnp.dot(a, b, preferred_element_type=F32)


def _split(a):
    hi = a.astype(BF16)
    lo = (a - hi.astype(F32)).astype(BF16)
    return hi, lo


def _dot3(a, b):
    ah, al = _split(a)
    bh, bl = _split(b)
    return _dot(ah, bh) + (_dot(ah, bl) + _dot(al, bh))


def _rms(x, g):
    return x * lax.rsqrt(jnp.mean(x * x, axis=-1, keepdims=True) + EPS) * g


def _ada_kernel(c_ref, w_ref, b_ref, o_ref):
    cc = c_ref[...]
    s = cc * jax.nn.sigmoid(cc)
    o_ref[0] = _dot3(s, w_ref[0]) + b_ref[0]


def _ada(cc, w_ada, b_ada):
    tn = 1536
    n = N_MOD * D_MODEL
    return pl.pallas_call(
        _ada_kernel,
        out_shape=jax.ShapeDtypeStruct((DEPTH, 8, n), F32),
        grid=(DEPTH, n // tn),
        in_specs=[
            pl.BlockSpec((8, D_MODEL), lambda l, j: (0, 0)),
            pl.BlockSpec((1, D_MODEL, tn), lambda l, j: (l, 0, j)),
            pl.BlockSpec((1, 1, tn), lambda l, j: (l, 0, j)),
        ],
        out_specs=pl.BlockSpec((1, 8, tn), lambda l, j: (l, 0, j)),
        compiler_params=_cp("parallel", "parallel"),
        name="ada",
    )(cc, w_ada, b_ada.reshape(DEPTH, 1, n))


def _inproj_kernel(h_ref, hprev_ref, hnext_ref, mod_ref, g_ref, tab_ref, qg_ref, kvg_ref, scw_ref, scb_ref,
                   why_ref, wgt_ref, wq_ref, wkv_ref, wkr_ref, wuq_ref, wuqs_ref, wuk_ref, wuv_ref,
                   uhy_ref, sg_ref, q_ref, k_ref, v_ref):
    i = pl.program_id(1)
    shift = mod_ref[0, 0, 0, 0:1, :]
    scale = mod_ref[0, 0, 0, 1:2, :]

    def modulated(rows):
        return (_rms(rows, g_ref[0]) * (1.0 + scale) + shift).astype(BF16)

    u = modulated(h_ref[0])

    p = _dot(u, why_ref[0])
    p_halo = _dot(modulated(jnp.concatenate([hprev_ref[0], hnext_ref[0]], axis=0)), why_ref[0])
    starts_stream = (i == 0) | (i == N_LAT_TILES)
    ends_stream = (i == N_LAT_TILES - 1) | (i == N_ALL_TILES - 1)
    p_before = jnp.where(starts_stream, 0.0, p_halo[SUBLANES - 1:SUBLANES])
    p_after = jnp.where(ends_stream, 0.0, p_halo[SUBLANES:SUBLANES + 1])
    row = lax.broadcasted_iota(jnp.int32, (TM, 1), 0)
    prev = jnp.where(row == 0, p_before, pltpu.roll(p, 1, 0))
    nxt = jnp.where(row == TM - 1, p_after, pltpu.roll(p, TM - 1, 0))
    w = scw_ref[0]
    uhy_ref[0] = scb_ref[0] + prev * w[0:1] + p * w[1:2] + nxt * w[2:3]

    sg_ref[0] = jax.nn.sigmoid(_dot(u, wgt_ref[0])).astype(BF16)

    cosq, sinq, cosk, sink = tab_ref[0], tab_ref[1], tab_ref[2], tab_ref[3]
    cq = _rms(_dot(u, wq_ref[0]), qg_ref[0]).astype(BF16)
    qa = _dot(cq, wuq_ref[0])
    qs = _dot(cq, wuqs_ref[0])
    ckv = _rms(_dot(u, wkv_ref[0]), kvg_ref[0]).astype(BF16)
    ka = _dot(ckv, wuk_ref[0])
    va = _dot(ckv, wuv_ref[0])
    kr = _dot(u, wkr_ref[0])
    krk = kr[:, :HEAD_PAD] * cosk + kr[:, HEAD_PAD:] * sink
    ones_col = (lax.broadcasted_iota(jnp.int32, (1, HEAD_PAD), 1) == V_DIM).astype(F32)
    for hh in range(MLA_HEADS):
        sl = slice(hh * HEAD_PAD, (hh + 1) * HEAD_PAD)
        q_ref[0, hh] = (qa[:, sl] * cosq + qs[:, sl] * sinq).astype(BF16)
        k_ref[0, hh] = (ka[:, sl] + krk).astype(BF16)
        v_ref[0, hh] = (va[:, sl] + ones_col).astype(BF16)


def _inproj(l, h, mods, norm1_g, tabs, q_norm_g, kv_norm_g, sc_w, sc_b, wts):
    names = ("why", "wgt", "wq", "wkv", "wkr", "wuq", "wuqs", "wuk", "wuv")
    w_specs = [pl.BlockSpec((1,) + wts[n].shape[1:], lambda b, i: (l, 0, 0)) for n in names]
    qkv_shape = jax.ShapeDtypeStruct((BATCH, MLA_HEADS, S_ALL, HEAD_PAD), BF16)
    qkv_spec = pl.BlockSpec((1, MLA_HEADS, TM, HEAD_PAD), lambda b, i: (b, 0, i, 0))
    halo_blocks = TM // SUBLANES
    last_halo = S_ALL // SUBLANES - 1
    return pl.pallas_call(
        _inproj_kernel,
        out_shape=(
            jax.ShapeDtypeStruct((BATCH, S_ALL, HY_COLS), F32),
            jax.ShapeDtypeStruct((BATCH, S_ALL, 2 * D_MODEL), BF16),
            qkv_shape, qkv_shape, qkv_shape,
        ),
        grid=(BATCH, N_ALL_TILES),
        in_specs=[
            pl.BlockSpec((1, TM, D_MODEL), lambda b, i: (b, i, 0)),
            pl.BlockSpec((1, SUBLANES, D_MODEL), lambda b, i: (b, jnp.maximum(i * halo_blocks - 1, 0), 0)),
            pl.BlockSpec((1, SUBLANES, D_MODEL),
                         lambda b, i: (b, jnp.minimum((i + 1) * halo_blocks, last_halo), 0)),
            pl.BlockSpec((1, 1, 1, N_MOD, D_MODEL), lambda b, i: (l, b, i // N_LAT_TILES, 0, 0)),
            pl.BlockSpec((1, 1, D_MODEL), lambda b, i: (l, 0, 0)),
            pl.BlockSpec((4, TM, HEAD_PAD), lambda b, i: (0, i, 0)),
            pl.BlockSpec((1, 1, Q_LORA), lambda b, i: (l, 0, 0)),
            pl.BlockSpec((1, 1, KV_LORA), lambda b, i: (l, 0, 0)),
            pl.BlockSpec((1, SHORT_K, HY_COLS), lambda b, i: (l, 0, 0)),
            pl.BlockSpec((1, 1, HY_COLS), lambda b, i: (l, 0, 0)),
        ] + w_specs,
        out_specs=(
            pl.BlockSpec((1, TM, HY_COLS), lambda b, i: (b, i, 0)),
            pl.BlockSpec((1, TM, 2 * D_MODEL), lambda b, i: (b, i, 0)),
            qkv_spec, qkv_spec, qkv_spec,
        ),
        compiler_params=_cp("parallel", "parallel"),
        name="inproj",
    )(h, h, h, mods, norm1_g, tabs, q_norm_g, kv_norm_g, sc_w, sc_b, *[wts[n] for n in names])


def _flash_step(q_ref, k_ref, v_ref, rows, carry):
    out = []
    for hh in range(2):
        m, acc = carry[hh]
        s = lax.dot_general(q_ref[0, hh], k_ref[0, hh, rows, :], (((1,), (1,)), ((), ())),
                            preferred_element_type=F32)
        m_new = jnp.maximum(m, jnp.max(s, axis=-1, keepdims=True))
        p = jnp.exp2(s - m_new)
        acc = acc * jnp.exp2(m - m_new) + _dot(p.astype(BF16), v_ref[0, hh, rows, :])
        out.append((m_new, acc))
    return tuple(out)


def _attn_kernel(q_ref, k_ref, v_ref, o_ref, *, chunks):
    tq = q_ref.shape[2]
    init = (jnp.full((tq, 1), -1e30, F32), jnp.zeros((tq, HEAD_PAD), F32))
    carry = (init, init)
    start = 0
    for size in chunks:
        carry = _flash_step(q_ref, k_ref, v_ref, pl.ds(start, size), carry)
        start += size
    outs = [acc[:, :V_DIM] / acc[:, V_DIM:V_DIM + 1] for _, acc in carry]
    o_ref[0] = jnp.concatenate(outs, axis=-1).astype(BF16)


def _attention(q, k, v, latent):
    if latent:
        tq, nq, q0, kv_rows, kv_blk = TQ, SEQ // TQ, 0, S_ALL, 0
        kern = functools.partial(_attn_kernel, chunks=(TK,) * (SEQ // TK - 1) + (TK + CTX_LEN,))
    else:
        tq, nq, q0, kv_rows, kv_blk = CTX_LEN, 1, SEQ // CTX_LEN, CTX_LEN, SEQ // CTX_LEN
        kern = functools.partial(_attn_kernel, chunks=(CTX_LEN,))
    return pl.pallas_call(
        kern,
        out_shape=jax.ShapeDtypeStruct((BATCH, nq * tq, MLA_HEADS * V_DIM), BF16),
        grid=(BATCH, MLA_HEADS // 2, nq),
        in_specs=[
            pl.BlockSpec((1, 2, tq, HEAD_PAD), lambda b, hp, i: (b, hp, q0 + i, 0)),
            pl.BlockSpec((1, 2, kv_rows, HEAD_PAD), lambda b, hp, i: (b, hp, kv_blk, 0)),
            pl.BlockSpec((1, 2, kv_rows, HEAD_PAD), lambda b, hp, i: (b, hp, kv_blk, 0)),
        ],
        out_specs=pl.BlockSpec((1, tq, 2 * V_DIM), lambda b, hp, i: (b, i, hp)),
        compiler_params=_cp("parallel", "parallel", "parallel"),
        name="attn_lat" if latent else "attn_ctx",
    )(q, k, v)


def _filter_kernel(z_ref, w1_ref, b1_ref, fr_ref, w2_ref, b2_ref, w3f_ref, w3b_ref, dl_ref, hfb_ref, a_ref):
    z = z_ref[...]

    @pl.when((pl.program_id(0) == 0) & (pl.program_id(1) == 0))
    def _():
        fr = fr_ref[0]
        a1 = jnp.sin(fr * (_dot3(z, w1_ref[0]) + b1_ref[0]))
        a_ref[...] = jnp.sin(fr * (_dot3(a1, w2_ref[0]) + b2_ref[0]))

    a = a_ref[...]
    decay = jnp.exp(-z[:, 0:1] * jnp.abs(dl_ref[...]))
    rows = lax.broadcasted_iota(jnp.int32, (z.shape[0], 1), 0)

    def one_direction(w3_ref):
        h = _dot3(a, w3_ref[0]) * decay
        return h * lax.rsqrt(jnp.sum(h * h, axis=0, keepdims=True) + EPS)

    hfb_ref[0] = one_direction(w3f_ref).astype(BF16)
    hfb_ref[1] = jnp.where(rows == 0, 0.0, one_direction(w3b_ref)).astype(BF16)


def _filters(l, L, zemb, deltas, flt):
    w1, b1, fr, w2, b2, w3 = flt
    ncb = HY_WIDTH // LANES
    full = lambda *shape: pl.BlockSpec((1,) + shape, lambda o, j: (l,) + (0,) * len(shape))
    return pl.pallas_call(
        _filter_kernel,
        out_shape=jax.ShapeDtypeStruct((2, L, HY_ORDER * HY_WIDTH), BF16),
        grid=(HY_ORDER, ncb),
        in_specs=[
            pl.BlockSpec((L, LANES), lambda o, j: (0, 0)),
            full(LANES, HY_FFN), full(1, HY_FFN), full(1, HY_FFN),
            full(HY_FFN, HY_FFN), full(1, HY_FFN),
            pl.BlockSpec((1, HY_FFN, LANES), lambda o, j: (l, 0, o * 2 * ncb + j)),
            pl.BlockSpec((1, HY_FFN, LANES), lambda o, j: (l, 0, o * 2 * ncb + ncb + j)),
            pl.BlockSpec((1, LANES), lambda o, j: (0, j)),
        ],
        out_specs=pl.BlockSpec((2, L, LANES), lambda o, j: (0, 0, o * ncb + j)),
        scratch_shapes=[pltpu.VMEM((L, HY_FFN), F32)],
        compiler_params=_cp("arbitrary", "arbitrary"),
        name="hy_filter",
    )(zemb, w1, b1, fr, w2, b2, w3, w3, deltas)


def _packed_row0(block):
    return (lax.broadcasted_iota(jnp.int32, (FB, 1), 0) == 0) & (block == 0)


def _packed_mac(acc, c, z):
    (ar, ai), (cr, ci), (zr, zi) = acc, c, z
    return ar + (zr * cr - zi * ci), ai + (zr * ci + zi * cr)


def _spec_kernel(f_ref, h_ref, o_ref, *, seg, n_seg):
    fblk = pl.program_id(1)
    fr, fi = f_ref[0, 0], f_ref[0, 1]
    row0 = _packed_row0(fblk)
    sign = jnp.where((lax.broadcasted_iota(jnp.int32, (FB, 1), 0) & 1) == 0, 1.0, -1.0)

    def first_half(d, m):
        taps = h_ref[d, m * seg:(m + 1) * seg, :]
        return _dot(fr, taps), _dot(fi, taps), taps[0:1, :].astype(F32)

    def causal_block(d, m, halves):
        re, im, _ = halves[m]
        if m >= 1:
            pre, pim, tap0 = halves[m - 1]
            re = re + sign * (pre - tap0)
            im = im + sign * (pim - jnp.where(row0, tap0, 0.0))
        return re, im

    def conj(re, im):
        return re, jnp.where(row0, im, -im)

    halves = [[first_half(d, m) for m in range(n_seg)] for d in range(2)]
    for li in range(2 * n_seg - 1):
        lag = li - (n_seg - 1)
        if lag > 0:
            re, im = causal_block(0, lag, halves[0])
        elif lag < 0:
            re, im = conj(*causal_block(1, -lag, halves[1]))
        else:
            fre, fim = causal_block(0, 0, halves[0])
            gre, gim = conj(*causal_block(1, 0, halves[1]))
            re, im = fre + gre, fim + gim
        o_ref[0, 0, li, 0] = re
        o_ref[0, 0, li, 1] = im


def _spectrum(L, seg, fmat, hfb):
    n_seg, nfb = L // seg, seg // FB
    n_lag = 2 * n_seg - 1
    return pl.pallas_call(
        functools.partial(_spec_kernel, seg=seg, n_seg=n_seg),
        out_shape=jax.ShapeDtypeStruct((HY_ORDER, nfb, n_lag, 2, FB, HY_WIDTH), F32),
        grid=(HY_ORDER, nfb),
        in_specs=[
            pl.BlockSpec((1, 2, FB, seg), lambda o, f: (f, 0, 0, 0)),
            pl.BlockSpec((2, L, HY_WIDTH), lambda o, f: (0, 0, o)),
        ],
        out_specs=pl.BlockSpec((1, 1, n_lag, 2, FB, HY_WIDTH), lambda o, f: (o, f, 0, 0, 0, 0)),
        compiler_params=_cp("parallel", "parallel"),
        name="hy_spectrum",
    )(fmat, hfb)


def _fwd_kernel(z_ref, f_ref, c_ref, y_ref, zb_ref, *, seg, n_seg):
    fblk = pl.program_id(1)

    @pl.when(fblk == 0)
    def _():
        zb_ref[...] = z_ref[0].astype(BF16)

    fr, fi = f_ref[0, 0], f_ref[0, 1]
    row0 = _packed_row0(fblk)
    acc = [None] * n_seg
    edge = [None] * n_seg
    for j in range(n_seg):
        zj = zb_ref[j * seg:(j + 1) * seg, :]
        z = (_dot(fr, zj), _dot(fi, zj))
        for i in range(n_seg):
            c = (c_ref[0, 0, i - j + n_seg - 1, 0], c_ref[0, 0, i - j + n_seg - 1, 1])
            e = (z[0][0:1] * c[0][0:1], z[1][0:1] * c[1][0:1])
            if acc[i] is None:
                acc[i] = (z[0] * c[0] - z[1] * c[1], z[0] * c[1] + z[1] * c[0])
                edge[i] = e
            else:
                acc[i] = _packed_mac(acc[i], c, z)
                edge[i] = (edge[i][0] + e[0], edge[i][1] + e[1])
    for i in range(n_seg):
        y_ref[0, i, 0:FB] = jnp.where(row0, edge[i][0], acc[i][0]).astype(BF16)
        y_ref[0, i, FB:2 * FB] = jnp.where(row0, edge[i][1], acc[i][1]).astype(BF16)


def _hy_fwd(L, seg, z, z_rowblk, z_colblk, fmat, spec, order):
    n_seg, nfb = L // seg, seg // FB
    n_lag = 2 * n_seg - 1
    return pl.pallas_call(
        functools.partial(_fwd_kernel, seg=seg, n_seg=n_seg),
        out_shape=jax.ShapeDtypeStruct((BATCH, n_seg, 2 * seg, HY_WIDTH), BF16),
        grid=(BATCH, nfb),
        in_specs=[
            pl.BlockSpec((1, L, HY_WIDTH), lambda b, f: (b, z_rowblk, z_colblk)),
            pl.BlockSpec((1, 2, FB, seg), lambda b, f: (f, 0, 0, 0)),
            pl.BlockSpec((1, 1, n_lag, 2, FB, HY_WIDTH), lambda b, f: (order, f, 0, 0, 0, 0)),
        ],
        out_specs=pl.BlockSpec((1, n_seg, 2 * FB, HY_WIDTH), lambda b, f: (b, 0, f, 0)),
        scratch_shapes=[pltpu.VMEM((L, HY_WIDTH), BF16)],
        compiler_params=_cp("parallel", "arbitrary"),
        name="hy_fwd",
    )(z, fmat, spec)


def _inv_kernel(y_ref, g_ref, gate_ref, z_ref, bias_ref, o_ref):
    y = _dot(g_ref[...], y_ref[0, 0])
    o_ref[0] = gate_ref[0] * (y + bias_ref[0, 0] * z_ref[0])


def _hy_inv(l, L, seg, yf, gmat, u, row0, gate_colblk, z, z_row0, z_colblk, flt_bias, order):
    tmi = min(seg, 2 * TM)
    per_seg = seg // tmi
    r0, zr0 = row0 // tmi, z_row0 // tmi
    return pl.pallas_call(
        _inv_kernel,
        out_shape=jax.ShapeDtypeStruct((BATCH, L, HY_WIDTH), F32),
        grid=(BATCH, L // tmi),
        in_specs=[
            pl.BlockSpec((1, 1, 2 * seg, HY_WIDTH), lambda b, i: (b, i // per_seg, 0, 0)),
            pl.BlockSpec((tmi, 2 * seg), lambda b, i: (i % per_seg, 0)),
            pl.BlockSpec((1, tmi, HY_WIDTH), lambda b, i: (b, r0 + i, gate_colblk)),
            pl.BlockSpec((1, tmi, HY_WIDTH), lambda b, i: (b, zr0 + i, z_colblk)),
            pl.BlockSpec((1, 1, 1, HY_WIDTH), lambda b, i: (l, order, 0, 0)),
        ],
        out_specs=pl.BlockSpec((1, tmi, HY_WIDTH), lambda b, i: (b, i, 0)),
        compiler_params=_cp("parallel", "parallel"),
        name="hy_inv",
    )(yf, gmat, u, z, flt_bias)


def _route_tile(lg, bias):
    lane = lax.broadcasted_iota(jnp.int32, lg.shape, 1)
    lane_f = lane.astype(F32)
    neg = jnp.float32(-jnp.inf)
    big = jnp.float32(ROUTER_PAD)
    biased = lg + bias

    def first_argmax(v):
        m = jnp.max(v, axis=-1, keepdims=True)
        return jnp.min(jnp.where(v == m, lane_f, big), axis=-1, keepdims=True).astype(jnp.int32)

    def pick(v, idx):
        return jnp.sum(jnp.where(lane == idx, v, 0.0), axis=-1, keepdims=True)

    is_group = lane < N_GROUPS
    g_sel = first_argmax(jnp.where(is_group, biased, neg))
    raw_g = jnp.where(is_group, lg, neg)
    e_g = jnp.exp(raw_g - jnp.max(raw_g, axis=-1, keepdims=True))
    p_g = pick(e_g, g_sel) / jnp.sum(e_g, axis=-1, keepdims=True)

    lo = N_GROUPS + g_sel * EXP_PER_GROUP
    cand = jnp.where((lane >= lo) & (lane < lo + EXP_PER_GROUP), biased, neg)
    i1 = first_argmax(cand)
    i2 = first_argmax(jnp.where(lane == i1, neg, cand))
    l1, l2 = pick(lg, i1), pick(lg, i2)
    top = jnp.maximum(l1, l2)
    e1, e2 = jnp.exp(l1 - top), jnp.exp(l2 - top)
    inv = p_g / (e1 + e2)
    return (i1 - N_GROUPS, i2 - N_GROUPS), (e1 * inv, e2 * inv)


def _merge_kernel(h_ref, sg_ref, hyl_ref, hyc_ref, attl_ref, attc_ref, mod_ref, g_ref,
                  who_ref, wmo_ref, wout_ref, wr_ref, br_ref, h1_ref, u2_ref, eid_ref, ew_ref, cnt_ref):
    is_ctx = pl.program_id(1) >= N_LAT_TILES
    hy = jnp.where(is_ctx, hyc_ref[0], hyl_ref[0]).astype(BF16)
    att = jnp.where(is_ctx, attc_ref[0], attl_ref[0])
    sg = sg_ref[0].astype(F32)
    m = sg[:, :D_MODEL] * _dot(hy, who_ref[0]) + sg[:, D_MODEL:] * _dot(att, wmo_ref[0])
    y = _dot(m.astype(BF16), wout_ref[0])
    g1 = mod_ref[0, 0, 0, 2:3, :]
    sh2 = mod_ref[0, 0, 0, 3:4, :]
    sc2 = mod_ref[0, 0, 0, 4:5, :]
    h1 = h_ref[0] + g1 * y
    h1_ref[0] = h1
    u2 = _rms(h1, g_ref[0]) * (1.0 + sc2) + sh2
    u2_ref[0] = u2
    ids, ws = _route_tile(_dot3(u2, wr_ref[0]), br_ref[0])
    lane = lax.broadcasted_iota(jnp.int32, (TM, ROUTER_PAD), 1)
    hist = jnp.zeros((1, ROUTER_PAD), F32)
    for c in range(TOP_K):
        eid_ref[0, :, c:c + 1] = ids[c]
        ew_ref[0, :, c:c + 1] = ws[c]
        hist = hist + jnp.sum(jnp.where(lane == ids[c], 1.0, 0.0), axis=0, keepdims=True)
    cnt_ref[0, 0] = hist


def _merge(l, n_tiles, h, sg, hy_lat, hy_ctx, att_lat, att_ctx, mods, norm2_g, who, wmo, wout, wr, br):
    rows = n_tiles * TM
    tile = lambda w: pl.BlockSpec((1, TM, w), lambda b, i: (b, i, 0))
    lat = lambda w: pl.BlockSpec((1, TM, w), lambda b, i: (b, jnp.minimum(i, N_LAT_TILES - 1), 0))
    ctx = lambda w: pl.BlockSpec((1, TM, w), lambda b, i: (b, 0, 0))
    wfull = lambda a: pl.BlockSpec((1,) + a.shape[1:], lambda b, i: (l, 0, 0))
    return pl.pallas_call(
        _merge_kernel,
        out_shape=(
            jax.ShapeDtypeStruct((BATCH, rows, D_MODEL), F32),
            jax.ShapeDtypeStruct((BATCH, rows, D_MODEL), F32),
            jax.ShapeDtypeStruct((BATCH, rows, TOP_K), jnp.int32),
            jax.ShapeDtypeStruct((BATCH, rows, TOP_K), F32),
            jax.ShapeDtypeStruct((BATCH, n_tiles, 1, ROUTER_PAD), F32),
        ),
        grid=(BATCH, n_tiles),
        in_specs=[
            tile(D_MODEL), tile(2 * D_MODEL),
            lat(HY_WIDTH), ctx(HY_WIDTH), lat(MLA_HEADS * V_DIM), ctx(MLA_HEADS * V_DIM),
            pl.BlockSpec((1, 1, 1, N_MOD, D_MODEL), lambda b, i: (l, b, i // N_LAT_TILES, 0, 0)),
            pl.BlockSpec((1, 1, D_MODEL), lambda b, i: (l, 0, 0)),
            wfull(who), wfull(wmo), wfull(wout), wfull(wr), wfull(br),
        ],
        out_specs=(tile(D_MODEL), tile(D_MODEL), tile(TOP_K), tile(TOP_K),
                   pl.BlockSpec((1, 1, 1, ROUTER_PAD), lambda b, i: (b, i, 0, 0))),
        compiler_params=_cp("parallel", "parallel"),
        name="merge",
    )(h, sg, hy_lat, hy_ctx, att_lat, att_ctx, mods, norm2_g, who, wmo, wout, wr, br)


def _expert_kernel(be_ref, b0_ref, nr_ref, src_ref, dst_ref, u_hbm, wg_ref, wu_ref, wd_ref, y_hbm,
                   xbuf, ybuf, gsem, ssem):
    i = pl.program_id(0)
    nb = pl.num_programs(0)
    slot = i % 2

    def in_copy(blk, sl, g, j):
        tok = src_ref[b0_ref[blk] + g * SUBLANES + j]
        return pltpu.make_async_copy(u_hbm.at[pl.ds(tok, 1)], xbuf.at[sl, g, pl.ds(j, 1)], gsem.at[sl])

    def out_copy(blk, sl, g, j):
        r = g * SUBLANES + j
        return pltpu.make_async_copy(ybuf.at[sl, pl.ds(r, 1)], y_hbm.at[pl.ds(dst_ref[b0_ref[blk] + r], 1)],
                                     ssem.at[sl])

    def for_rows(blk, fn):
        n = nr_ref[blk]
        n_groups = n // SUBLANES

        def group(g, carry):
            for j in range(SUBLANES):
                fn(g, j)
            return carry

        def single(r, carry):
            fn(n_groups, r)
            return carry

        lax.fori_loop(0, n_groups, group, 0)
        lax.fori_loop(0, n - n_groups * SUBLANES, single, 0)

    @pl.when(i == 0)
    def _():
        xbuf[...] = jnp.zeros_like(xbuf)
        for_rows(0, lambda g, j: in_copy(0, 0, g, j).start())

    @pl.when(i + 1 < nb)
    def _():
        for_rows(i + 1, lambda g, j: in_copy(i + 1, 1 - slot, g, j).start())

    for_rows(i, lambda g, j: in_copy(i, slot, g, j).wait())

    @pl.when(i >= 2)
    def _():
        for_rows(i - 2, lambda g, j: out_copy(i - 2, slot, g, j).wait())

    @pl.when(nr_ref[i] > 0)
    def _():
        x = xbuf[slot].reshape(MOE_BM, D_MODEL).astype(BF16)
        hg = _dot(x, wg_ref[0, 0].astype(BF16))
        hu = _dot(x, wu_ref[0, 0].astype(BF16))
        hb = (hg * jax.nn.sigmoid(hg) * hu).astype(BF16)
        ybuf[slot] = _dot(hb, wd_ref[0, 0].astype(BF16))

    for_rows(i, lambda g, j: out_copy(i, slot, g, j).start())

    @pl.when(i == nb - 1)
    def _():
        for_rows(i - 1, lambda g, j: out_copy(i - 1, 1 - slot, g, j).wait())
        for_rows(i, lambda g, j: out_copy(i, slot, g, j).wait())


def _experts(l, block_expert, block_row0, block_rows, order, u, w_gate_e, w_up_e, w_down_e):
    n_blocks = block_expert.shape[0]
    n_tok = u.shape[0]
    src = lax.shift_right_logical(order, 1)
    dst = (order & 1) * n_tok + src
    wspec = lambda r, c: pl.BlockSpec((1, 1, r, c), lambda i, be, b0, nr, sr, ds: (l, be[i], 0, 0))
    grid_spec = pltpu.PrefetchScalarGridSpec(
        num_scalar_prefetch=5,
        grid=(n_blocks,),
        in_specs=[
            pl.BlockSpec(memory_space=pl.ANY),
            wspec(D_MODEL, D_EXPERT), wspec(D_MODEL, D_EXPERT), wspec(D_EXPERT, D_MODEL),
        ],
        out_specs=pl.BlockSpec(memory_space=pl.ANY),
        scratch_shapes=[
            pltpu.VMEM((2, MOE_BM // SUBLANES, SUBLANES, D_MODEL), F32),
            pltpu.VMEM((2, MOE_BM, D_MODEL), F32),
            pltpu.SemaphoreType.DMA((2,)),
            pltpu.SemaphoreType.DMA((2,)),
        ],
    )
    return pl.pallas_call(
        _expert_kernel,
        out_shape=jax.ShapeDtypeStruct((TOP_K * u.shape[0], D_MODEL), F32),
        grid_spec=grid_spec,
        compiler_params=_cp("arbitrary"),
        name="experts",
    )(block_expert, block_row0, block_rows, src, dst, u, w_gate_e, w_up_e, w_down_e)


def _dispatch(expert, counts):
    A = expert.shape[0]
    _, order = lax.sort((expert, jnp.arange(A, dtype=jnp.int32)), num_keys=1)
    start = jnp.cumsum(counts) - counts
    n_blk = (counts + MOE_BM - 1) // MOE_BM
    blk_end = jnp.cumsum(n_blk)
    blk_start = blk_end - n_blk
    n_blocks = -(-A // MOE_BM) + N_EXPERTS
    b = jnp.arange(n_blocks, dtype=jnp.int32)[:, None]
    mine = (b >= blk_start[None, :]) & (b < blk_end[None, :])
    used = jnp.any(mine, axis=1)
    pick = lambda table: jnp.sum(jnp.where(mine, table[None, :], 0), axis=1)
    row_in_expert = (b[:, 0] - pick(blk_start)) * MOE_BM
    block_expert = jnp.where(used, pick(jnp.arange(N_EXPERTS, dtype=jnp.int32)), N_EXPERTS - 1)
    block_rows = jnp.where(used, jnp.clip(pick(counts) - row_in_expert, 0, MOE_BM), 0)
    block_row0 = jnp.where(used, pick(start) + row_in_expert, 0)
    return order, block_expert.astype(jnp.int32), block_row0.astype(jnp.int32), block_rows.astype(jnp.int32)


def _moe_sum(y_ref, w_ref):
    w = w_ref[0]
    return sum(w[:, c:c + 1] * y_ref[c, 0] for c in range(TOP_K))


def _combine_kernel(h_ref, y_ref, w_ref, mod_ref, o_ref):
    o_ref[0] = h_ref[0] + mod_ref[0, 0, 0, 5:6, :] * _moe_sum(y_ref, w_ref)


def _final_kernel(h_ref, y_ref, w_ref, mod_ref, g_ref, o_ref):
    h2 = h_ref[0] + mod_ref[0, 0, 0, 5:6, :] * _moe_sum(y_ref, w_ref)
    o_ref[0] = _rms(h2, g_ref[...])


def _combine(l, n_tiles, h1, y, w, mods, final_g=None):
    rows = n_tiles * TM
    tile = pl.BlockSpec((1, TM, D_MODEL), lambda b, i: (b, i, 0))
    in_specs = [tile, pl.BlockSpec((TOP_K, 1, TM, D_MODEL), lambda b, i: (0, b, i, 0)),
                pl.BlockSpec((1, TM, TOP_K), lambda b, i: (b, i, 0)),
                pl.BlockSpec((1, 1, 1, N_MOD, D_MODEL), lambda b, i: (l, b, i // N_LAT_TILES, 0, 0))]
    args = [h1, y, w, mods]
    kern = _combine_kernel
    if final_g is not None:
        in_specs.append(pl.BlockSpec((1, D_MODEL), lambda b, i: (0, 0)))
        args.append(final_g.reshape(1, D_MODEL))
        kern = _final_kernel
    return pl.pallas_call(
        kern,
        out_shape=jax.ShapeDtypeStruct((BATCH, rows, D_MODEL), F32),
        grid=(BATCH, n_tiles),
        in_specs=in_specs,
        out_specs=tile,
        compiler_params=_cp("parallel", "parallel"),
        name="combine",
    )(*args)


def _rope_tables():
    n = jnp.arange(SEQ)
    pos = jnp.stack([n // GRID_W, n % GRID_W], axis=-1).astype(F32)
    inv = ROPE_BASE ** (-jnp.arange(ROPE_PAIRS, dtype=F32) / ROPE_PAIRS)
    ang = pos[:, :, None] * inv
    cos, sin = jnp.cos(ang), jnp.sin(ang)
    cos32 = jnp.stack([cos, cos], axis=2).reshape(SEQ, QK_ROPE)
    sin32 = jnp.stack([-sin, sin], axis=2).reshape(SEQ, QK_ROPE)
    one = jnp.ones((SEQ, QK_ROPE), F32)
    zero = jnp.zeros((SEQ, QK_ROPE), F32)
    cosq = jnp.concatenate([jnp.ones((SEQ, QK_NOPE), F32), cos32, one], axis=1) * Q_SCALE
    sinq = jnp.concatenate([jnp.zeros((SEQ, QK_NOPE), F32), sin32, zero], axis=1) * Q_SCALE
    cosk = jnp.concatenate([jnp.zeros((SEQ, QK_NOPE), F32), cos32, zero], axis=1)
    sink = jnp.concatenate([jnp.zeros((SEQ, QK_NOPE), F32), sin32, zero], axis=1)
    lat = jnp.stack([cosq, sinq, cosk, sink])
    ctx_row = jnp.ones((HEAD_PAD,), F32) * Q_SCALE
    ctx_cosk = jnp.concatenate([jnp.zeros((QK_NOPE + QK_ROPE,), F32), jnp.ones((QK_ROPE,), F32)])
    ctx = jnp.stack([ctx_row, jnp.zeros_like(ctx_row), ctx_cosk, jnp.zeros_like(ctx_row)])
    ctx = jnp.broadcast_to(ctx[:, None, :], (4, CTX_LEN, HEAD_PAD))
    return jnp.concatenate([lat, ctx], axis=1)


def _dft_matrices(L):
    N = 2 * L
    nfb = L // FB
    t = jnp.arange(L, dtype=jnp.int32)

    def tables(n_rows, step):
        ang = ((jnp.arange(n_rows, dtype=jnp.int32)[:, None] * step * t[None, :]) % N).astype(F32)
        ang = ang * (2.0 * math.pi / N)
        return jnp.cos(ang), jnp.sin(ang)

    ca, sa = tables(nfb, FB)
    cb, sb = tables(FB, 1)
    alt = jnp.where((t % 2) == 0, 1.0, -1.0).astype(F32)
    i_idx = jnp.arange(nfb)[:, None, None]
    r_idx = jnp.arange(FB)[None, :, None]
    dc = (i_idx == 0) & (r_idx == 0)
    c = ca[:, None, :] * cb[None] - sa[:, None, :] * sb[None]
    s = jnp.where(dc, alt, -(sa[:, None, :] * cb[None] + ca[:, None, :] * sb[None]))
    f = jnp.stack([c, s], axis=1).astype(BF16)
    wk = jnp.where(dc, 1.0 / N, 2.0 / N).astype(F32)[..., 0]
    cat, sat, cbt, sbt = ca.T[:, :, None], sa.T[:, :, None], cb.T[:, None, :], sb.T[:, None, :]
    ct = (cat * cbt - sat * sbt) * wk
    st = jnp.where(dc[..., 0], alt[:, None, None], -(sat * cbt + cat * sbt)) * wk
    g = jnp.stack([ct, st], axis=2).reshape(L, 2 * L).astype(BF16)
    return f, g


def _filter_embedding(L):
    t = jnp.linspace(0.0, 1.0, L, dtype=F32)[:, None]
    w = 2.0 * math.pi * jnp.arange(L, dtype=F32)[:, None] / L
    bands = jnp.linspace(1e-4, HY_BANDS - 1, HY_BANDS, dtype=F32)[None, :]
    return jnp.concatenate([t, jnp.cos(bands * w), -jnp.sin(bands * w),
                            jnp.zeros((L, LANES - HY_EMB), F32)], axis=-1)


def _rope_swap_perm():
    idx = np.arange(QK_ROPE)
    axis, half, pair = idx // (2 * ROPE_PAIRS), (idx // ROPE_PAIRS) % 2, idx % ROPE_PAIRS
    return axis * 2 * ROPE_PAIRS + (1 - half) * ROPE_PAIRS + pair


def _prep_weights(w_in, w_uq, w_ukv, w_hy_o, w_mla_o, w_out, w_router_g, w_router_e):
    perm = _rope_swap_perm()
    c_q = HY_COLS
    c_kv = HY_COLS + Q_LORA
    c_kr = c_kv + KV_LORA
    w_kr = w_in[:, :, c_kr:MLA_END]
    zeros = lambda n: jnp.zeros((DEPTH, D_MODEL, n), F32)
    wkr = jnp.concatenate([zeros(QK_NOPE), w_kr, w_kr,
                           zeros(QK_NOPE), w_kr[:, :, perm], zeros(QK_ROPE)], axis=-1)
    uq = w_uq.reshape(DEPTH, Q_LORA, MLA_HEADS, QK_NOPE + QK_ROPE)
    uq_r = uq[..., QK_NOPE:]
    wuq = jnp.concatenate([uq, uq_r], axis=-1).reshape(DEPTH, Q_LORA, MLA_HEADS * HEAD_PAD)
    wuqs = jnp.concatenate([jnp.zeros_like(uq[..., :QK_NOPE]), uq_r[..., perm], jnp.zeros_like(uq_r)],
                           axis=-1).reshape(DEPTH, Q_LORA, MLA_HEADS * HEAD_PAD)
    ukv = w_ukv.reshape(DEPTH, KV_LORA, MLA_HEADS, QK_NOPE + V_DIM)
    zpad = jnp.zeros_like(ukv[..., :HEAD_PAD - QK_NOPE])
    wuk = jnp.concatenate([ukv[..., :QK_NOPE], zpad], axis=-1).reshape(DEPTH, KV_LORA, MLA_HEADS * HEAD_PAD)
    wuv = jnp.concatenate([ukv[..., QK_NOPE:], zpad], axis=-1).reshape(DEPTH, KV_LORA, MLA_HEADS * HEAD_PAD)
    wr = jnp.concatenate([w_router_g, w_router_e,
                          jnp.zeros((DEPTH, D_MODEL, ROUTER_PAD - N_GROUPS - N_EXPERTS), F32)], axis=-1)
    bf = lambda a: a.astype(BF16)
    return dict(
        why=bf(w_in[:, :, :HY_COLS]), wgt=bf(w_in[:, :, MLA_END:]),
        wq=bf(w_in[:, :, c_q:c_kv]), wkv=bf(w_in[:, :, c_kv:c_kr]), wkr=bf(wkr),
        wuq=bf(wuq), wuqs=bf(wuqs), wuk=bf(wuk), wuv=bf(wuv),
        who=bf(w_hy_o), wmo=bf(w_mla_o), wout=bf(w_out), wr=wr,
    )


def _hyena(l, L, row0, u, fmat, gmat, zemb, deltas, flt, flt_bias):
    seg = fmat.shape[-1]
    spec = _spectrum(L, seg, fmat, _filters(l, L, zemb, deltas, flt))
    rb = row0 // L
    yf = _hy_fwd(L, seg, u, rb, 0, fmat, spec, 0)
    z1 = _hy_inv(l, L, seg, yf, gmat, u, row0, 1, u, row0, 0, flt_bias, 0)
    yf = _hy_fwd(L, seg, z1, 0, 0, fmat, spec, 1)
    return _hy_inv(l, L, seg, yf, gmat, u, row0, 2, z1, 0, 0, flt_bias, 1)


def kernel(x, c, ctx, c_ctx, w_ada, b_ada, norm1_g, w_in, q_norm_g, kv_norm_g, w_uq, w_ukv,
           sc_w, sc_b, flt_w1, flt_b1, flt_freq, flt_w2, flt_b2, flt_w3, flt_bias,
           w_hy_o, w_mla_o, w_out, norm2_g, w_router_g, b_router_g, w_router_e, b_router_e,
           w_gate_e, w_up_e, w_down_e, final_g):
    wts = _prep_weights(w_in, w_uq, w_ukv, w_hy_o, w_mla_o, w_out, w_router_g, w_router_e)
    tabs = _rope_tables()
    f_lat, g_lat = _dft_matrices(min(SEQ, HY_SEG))
    f_ctx, g_ctx = _dft_matrices(min(CTX_LEN, HY_SEG))
    z_lat, z_ctx = _filter_embedding(SEQ), _filter_embedding(CTX_LEN)
    deltas = jnp.linspace(math.log(HY_TARGET) / HY_FAST_DECAY, math.log(HY_TARGET) / HY_SLOW_DECAY,
                          HY_WIDTH, dtype=F32)[None, :]

    cc = jnp.concatenate([c, c_ctx[None, :], jnp.zeros((3, D_MODEL), F32)], axis=0)
    mod = _ada(cc, w_ada, b_ada).reshape(DEPTH, 8, N_MOD, D_MODEL)
    mods = jnp.stack([mod[:, :BATCH], jnp.broadcast_to(mod[:, BATCH:BATCH + 1], (DEPTH, BATCH, N_MOD, D_MODEL))],
                     axis=2)

    r3 = lambda a: a.reshape(DEPTH, 1, a.shape[-1])
    norm1, norm2, qg, kvg = r3(norm1_g), r3(norm2_g), r3(q_norm_g), r3(kv_norm_g)
    scb = r3(sc_b)
    fb1, ffr, fb2 = r3(flt_b1), r3(flt_freq), r3(flt_b2)
    fbias = flt_bias.reshape(DEPTH, HY_ORDER, 1, HY_WIDTH)
    w1_pad = jnp.concatenate([flt_w1, jnp.zeros((DEPTH, LANES - HY_EMB, HY_FFN), F32)], axis=1)
    flt = (w1_pad, fb1, ffr, flt_w2, fb2, flt_w3)

    b_router = jnp.concatenate([b_router_g, b_router_e,
                                jnp.zeros((DEPTH, ROUTER_PAD - N_GROUPS - N_EXPERTS), F32)], axis=-1)
    b_router = b_router.reshape(DEPTH, 1, ROUTER_PAD)

    h = jnp.concatenate([x, ctx], axis=1)
    for l in range(DEPTH):
        last = l == DEPTH - 1
        n_tiles = N_LAT_TILES if last else N_ALL_TILES
        u, sg, q, k, v = _inproj(l, h, mods, norm1, tabs, qg, kvg, sc_w, scb, wts)
        att_lat = _attention(q, k, v, latent=True)
        hy_lat = _hyena(l, SEQ, 0, u, f_lat, g_lat, z_lat, deltas, flt, fbias)
        if last:
            att_ctx, hy_ctx = att_lat, hy_lat
        else:
            att_ctx = _attention(q, k, v, latent=False)
            hy_ctx = _hyena(l, CTX_LEN, SEQ, u, f_ctx, g_ctx, z_ctx, deltas, flt, fbias)

        h1, u2, eid, ew, hist = _merge(l, n_tiles, h, sg, hy_lat, hy_ctx, att_lat, att_ctx, mods, norm2,
                                       wts["who"], wts["wmo"], wts["wout"], wts["wr"], b_router)
        counts = jnp.sum(hist, axis=(0, 1, 2))[:N_EXPERTS].astype(jnp.int32)

        T = BATCH * n_tiles * TM
        order, block_expert, block_row0, block_rows = _dispatch(eid.reshape(T * TOP_K), counts)
        y = _experts(l, block_expert, block_row0, block_rows, order, u2.reshape(T, D_MODEL),
                     w_gate_e, w_up_e, w_down_e)
        h = _combine(l, n_tiles, h1, y.reshape(TOP_K, BATCH, n_tiles * TM, D_MODEL), ew, mods,
                     final_g if last else None)
    return h
```

```python
import functools
import math

import jax
import jax.numpy as jnp
import numpy as np
from jax import lax
from jax.experimental import pallas as pl
from jax.experimental.pallas import tpu as pltpu

F32 = jnp.float32
BF16 = jnp.bfloat16

D_MODEL = 1024
BATCH = 4
SEQ = 4096
DEPTH = 4
GRID_W = 64
CTX_LEN = 256
S_ALL = SEQ + CTX_LEN
EPS = 1e-6
N_MOD = 6

HY_WIDTH = 512
HY_ORDER = 2
HY_BANDS = 16
HY_EMB = 1 + 2 * HY_BANDS
HY_FFN = 64
SHORT_K = 3
HY_FAST_DECAY = 0.3
HY_SLOW_DECAY = 1.5
HY_TARGET = 1e-2

MLA_HEADS = 8
QK_NOPE = 64
QK_ROPE = 32
V_DIM = 64
Q_LORA = 384
KV_LORA = 256
ROPE_PAIRS = QK_ROPE // 4
ROPE_BASE = 10000.0
ATTN_SCALE = (QK_NOPE + QK_ROPE) ** -0.5
Q_SCALE = ATTN_SCALE * math.log2(math.e)

N_GROUPS = 8
EXP_PER_GROUP = 8
N_EXPERTS = N_GROUPS * EXP_PER_GROUP
TOP_K = 2
D_EXPERT = 256

HY_COLS = (HY_ORDER + 1) * HY_WIDTH
MLA_END = HY_COLS + Q_LORA + KV_LORA + QK_ROPE

LANES = 128
HEAD_PAD = LANES
VMEM_LIMIT = 56 << 20

TM = 256
N_LAT_TILES = SEQ // TM
N_ALL_TILES = S_ALL // TM
TQ = 512
TK = 2048
ATT_HEADS = 4
FB = 256
HY_SEG = 1024
MOE_BM = 256
SUBLANES = 8
ROUTER_PAD = LANES


def _cp(*sem):
    return pltpu.CompilerParams(dimension_semantics=sem, vmem_limit_bytes=VMEM_LIMIT)


def _dot(a, b):
    return jnp.dot(a, b, preferred_element_type=F32)


def _split(a):
    hi = a.astype(BF16)
    lo = (a - hi.astype(F32)).astype(BF16)
    return hi, lo


def _dot3(a, b):
    ah, al = _split(a)
    bh, bl = _split(b)
    return _dot(ah, bh) + (_dot(ah, bl) + _dot(al, bh))


def _rms(x, g):
    return x * lax.rsqrt(jnp.mean(x * x, axis=-1, keepdims=True) + EPS) * g


def _ada_kernel(c_ref, w_ref, b_ref, o_ref):
    cc = c_ref[...]
    s = cc * jax.nn.sigmoid(cc)
    o_ref[0] = _dot3(s, w_ref[0]) + b_ref[0]


def _ada(cc, w_ada, b_ada):
    tn = 1536
    n = N_MOD * D_MODEL
    return pl.pallas_call(
        _ada_kernel,
        out_shape=jax.ShapeDtypeStruct((DEPTH, 8, n), F32),
        grid=(DEPTH, n // tn),
        in_specs=[
            pl.BlockSpec((8, D_MODEL), lambda l, j: (0, 0)),
            pl.BlockSpec((1, D_MODEL, tn), lambda l, j: (l, 0, j)),
            pl.BlockSpec((1, 1, tn), lambda l, j: (l, 0, j)),
        ],
        out_specs=pl.BlockSpec((1, 8, tn), lambda l, j: (l, 0, j)),
        compiler_params=_cp("parallel", "parallel"),
        name="ada",
    )(cc, w_ada, b_ada.reshape(DEPTH, 1, n))


def _inproj_kernel(h_ref, hprev_ref, hnext_ref, mod_ref, g_ref, tab_ref, qg_ref, kvg_ref, scw_ref, scb_ref,
                   why_ref, wgt_ref, wq_ref, wkv_ref, wkr_ref, wuq_ref, wuqs_ref, wuk_ref, wuv_ref,
                   uhy_ref, sg_ref, q_ref, k_ref, v_ref):
    i = pl.program_id(1)
    shift = mod_ref[0, 0, 0, 0:1, :]
    scale = mod_ref[0, 0, 0, 1:2, :]

    def modulated(rows):
        return (_rms(rows, g_ref[0]) * (1.0 + scale) + shift).astype(BF16)

    u = modulated(h_ref[0])

    p = _dot(u, why_ref[0])
    p_halo = _dot(modulated(jnp.concatenate([hprev_ref[0], hnext_ref[0]], axis=0)), why_ref[0])
    starts_stream = (i == 0) | (i == N_LAT_TILES)
    ends_stream = (i == N_LAT_TILES - 1) | (i == N_ALL_TILES - 1)
    p_before = jnp.where(starts_stream, 0.0, p_halo[SUBLANES - 1:SUBLANES])
    p_after = jnp.where(ends_stream, 0.0, p_halo[SUBLANES:SUBLANES + 1])
    row = lax.broadcasted_iota(jnp.int32, (TM, 1), 0)
    prev = jnp.where(row == 0, p_before, pltpu.roll(p, 1, 0))
    nxt = jnp.where(row == TM - 1, p_after, pltpu.roll(p, TM - 1, 0))
    w = scw_ref[0]
    uhy_ref[0] = scb_ref[0] + prev * w[0:1] + p * w[1:2] + nxt * w[2:3]

    sg_ref[0] = jax.nn.sigmoid(_dot(u, wgt_ref[0])).astype(BF16)

    cosq, sinq, cosk, sink = tab_ref[0], tab_ref[1], tab_ref[2], tab_ref[3]
    cq = _rms(_dot(u, wq_ref[0]), qg_ref[0]).astype(BF16)
    qa = _dot(cq, wuq_ref[0])
    qs = _dot(cq, wuqs_ref[0])
    ckv = _rms(_dot(u, wkv_ref[0]), kvg_ref[0]).astype(BF16)
    ka = _dot(ckv, wuk_ref[0])
    va = _dot(ckv, wuv_ref[0])
    kr = _dot(u, wkr_ref[0])
    krk = kr[:, :HEAD_PAD] * cosk + kr[:, HEAD_PAD:] * sink
    ones_col = (lax.broadcasted_iota(jnp.int32, (1, HEAD_PAD), 1) == V_DIM).astype(F32)
    for hh in range(MLA_HEADS):
        sl = slice(hh * HEAD_PAD, (hh + 1) * HEAD_PAD)
        q_ref[0, hh] = (qa[:, sl] * cosq + qs[:, sl] * sinq).astype(BF16)
        k_ref[0, hh] = (ka[:, sl] + krk).astype(BF16)
        v_ref[0, hh] = (va[:, sl] + ones_col).astype(BF16)


def _inproj(l, h, mods, norm1_g, tabs, q_norm_g, kv_norm_g, sc_w, sc_b, wts):
    names = ("why", "wgt", "wq", "wkv", "wkr", "wuq", "wuqs", "wuk", "wuv")
    w_specs = [pl.BlockSpec((1,) + wts[n].shape[1:], lambda b, i: (l, 0, 0)) for n in names]
    qkv_shape = jax.ShapeDtypeStruct((BATCH, MLA_HEADS, S_ALL, HEAD_PAD), BF16)
    qkv_spec = pl.BlockSpec((1, MLA_HEADS, TM, HEAD_PAD), lambda b, i: (b, 0, i, 0))
    halo_blocks = TM // SUBLANES
    last_halo = S_ALL // SUBLANES - 1
    return pl.pallas_call(
        _inproj_kernel,
        out_shape=(
            jax.ShapeDtypeStruct((BATCH, S_ALL, HY_COLS), F32),
            jax.ShapeDtypeStruct((BATCH, S_ALL, 2 * D_MODEL), BF16),
            qkv_shape, qkv_shape, qkv_shape,
        ),
        grid=(BATCH, N_ALL_TILES),
        in_specs=[
            pl.BlockSpec((1, TM, D_MODEL), lambda b, i: (b, i, 0)),
            pl.BlockSpec((1, SUBLANES, D_MODEL), lambda b, i: (b, jnp.maximum(i * halo_blocks - 1, 0), 0)),
            pl.BlockSpec((1, SUBLANES, D_MODEL),
                         lambda b, i: (b, jnp.minimum((i + 1) * halo_blocks, last_halo), 0)),
            pl.BlockSpec((1, 1, 1, N_MOD, D_MODEL), lambda b, i: (l, b, i // N_LAT_TILES, 0, 0)),
            pl.BlockSpec((1, 1, D_MODEL), lambda b, i: (l, 0, 0)),
            pl.BlockSpec((4, TM, HEAD_PAD), lambda b, i: (0, i, 0)),
            pl.BlockSpec((1, 1, Q_LORA), lambda b, i: (l, 0, 0)),
            pl.BlockSpec((1, 1, KV_LORA), lambda b, i: (l, 0, 0)),
            pl.BlockSpec((1, SHORT_K, HY_COLS), lambda b, i: (l, 0, 0)),
            pl.BlockSpec((1, 1, HY_COLS), lambda b, i: (l, 0, 0)),
        ] + w_specs,
        out_specs=(
            pl.BlockSpec((1, TM, HY_COLS), lambda b, i: (b, i, 0)),
            pl.BlockSpec((1, TM, 2 * D_MODEL), lambda b, i: (b, i, 0)),
            qkv_spec, qkv_spec, qkv_spec,
        ),
        compiler_params=_cp("parallel", "parallel"),
        name="inproj",
    )(h, h, h, mods, norm1_g, tabs, q_norm_g, kv_norm_g, sc_w, sc_b, *[wts[n] for n in names])


def _flash_step(q_ref, k_ref, v_ref, rows, carry):
    out = []
    for hh in range(ATT_HEADS):
        m, acc = carry[hh]
        s = lax.dot_general(q_ref[0, hh], k_ref[0, hh, rows, :], (((1,), (1,)), ((), ())),
                            preferred_element_type=F32)
        m_new = jnp.maximum(m, jnp.max(s, axis=-1, keepdims=True))
        p = jnp.exp2(s - m_new)
        acc = acc * jnp.exp2(m - m_new) + _dot(p.astype(BF16), v_ref[0, hh, rows, :])
        out.append((m_new, acc))
    return tuple(out)


def _attn_kernel(q_ref, k_ref, v_ref, o_ref, *, chunks):
    tq = q_ref.shape[2]
    init = (jnp.full((tq, 1), -1e30, F32), jnp.zeros((tq, HEAD_PAD), F32))
    carry = (init,) * ATT_HEADS
    start = 0
    for size in chunks:
        carry = _flash_step(q_ref, k_ref, v_ref, pl.ds(start, size), carry)
        start += size
    outs = [acc[:, :V_DIM] / acc[:, V_DIM:V_DIM + 1] for _, acc in carry]
    o_ref[0] = jnp.concatenate(outs, axis=-1).astype(BF16)


def _attention(q, k, v, latent):
    if latent:
        tq, nq, q0, kv_rows, kv_blk = TQ, SEQ // TQ, 0, S_ALL, 0
        kern = functools.partial(_attn_kernel, chunks=(TK,) * (SEQ // TK - 1) + (TK + CTX_LEN,))
    else:
        tq, nq, q0, kv_rows, kv_blk = CTX_LEN, 1, SEQ // CTX_LEN, CTX_LEN, SEQ // CTX_LEN
        kern = functools.partial(_attn_kernel, chunks=(CTX_LEN,))
    return pl.pallas_call(
        kern,
        out_shape=jax.ShapeDtypeStruct((BATCH, nq * tq, MLA_HEADS * V_DIM), BF16),
        grid=(BATCH, MLA_HEADS // ATT_HEADS, nq),
        in_specs=[
            pl.BlockSpec((1, ATT_HEADS, tq, HEAD_PAD), lambda b, hp, i: (b, hp, q0 + i, 0)),
            pl.BlockSpec((1, ATT_HEADS, kv_rows, HEAD_PAD), lambda b, hp, i: (b, hp, kv_blk, 0)),
            pl.BlockSpec((1, ATT_HEADS, kv_rows, HEAD_PAD), lambda b, hp, i: (b, hp, kv_blk, 0)),
        ],
        out_specs=pl.BlockSpec((1, tq, ATT_HEADS * V_DIM), lambda b, hp, i: (b, i, hp)),
        compiler_params=_cp("parallel", "parallel", "parallel"),
        name="attn_lat" if latent else "attn_ctx",
    )(q, k, v)


def _filter_kernel(z_ref, w1_ref, b1_ref, fr_ref, w2_ref, b2_ref, w3f_ref, w3b_ref, dl_ref, hfb_ref, a_ref):
    z = z_ref[...]

    @pl.when((pl.program_id(0) == 0) & (pl.program_id(1) == 0))
    def _():
        fr = fr_ref[0]
        a1 = jnp.sin(fr * (_dot3(z, w1_ref[0]) + b1_ref[0]))
        a_ref[...] = jnp.sin(fr * (_dot3(a1, w2_ref[0]) + b2_ref[0]))

    a = a_ref[...]
    decay = jnp.exp(-z[:, 0:1] * jnp.abs(dl_ref[...]))
    rows = lax.broadcasted_iota(jnp.int32, (z.shape[0], 1), 0)

    def one_direction(w3_ref):
        h = _dot3(a, w3_ref[0]) * decay
        return h * lax.rsqrt(jnp.sum(h * h, axis=0, keepdims=True) + EPS)

    hfb_ref[0] = one_direction(w3f_ref).astype(BF16)
    hfb_ref[1] = jnp.where(rows == 0, 0.0, one_direction(w3b_ref)).astype(BF16)


def _filters(l, L, zemb, deltas, flt):
    w1, b1, fr, w2, b2, w3 = flt
    ncb = HY_WIDTH // LANES
    full = lambda *shape: pl.BlockSpec((1,) + shape, lambda o, j: (l,) + (0,) * len(shape))
    return pl.pallas_call(
        _filter_kernel,
        out_shape=jax.ShapeDtypeStruct((2, L, HY_ORDER * HY_WIDTH), BF16),
        grid=(HY_ORDER, ncb),
        in_specs=[
            pl.BlockSpec((L, LANES), lambda o, j: (0, 0)),
            full(LANES, HY_FFN), full(1, HY_FFN), full(1, HY_FFN),
            full(HY_FFN, HY_FFN), full(1, HY_FFN),
            pl.BlockSpec((1, HY_FFN, LANES), lambda o, j: (l, 0, o * 2 * ncb + j)),
            pl.BlockSpec((1, HY_FFN, LANES), lambda o, j: (l, 0, o * 2 * ncb + ncb + j)),
            pl.BlockSpec((1, LANES), lambda o, j: (0, j)),
        ],
        out_specs=pl.BlockSpec((2, L, LANES), lambda o, j: (0, 0, o * ncb + j)),
        scratch_shapes=[pltpu.VMEM((L, HY_FFN), F32)],
        compiler_params=_cp("arbitrary", "arbitrary"),
        name="hy_filter",
    )(zemb, w1, b1, fr, w2, b2, w3, w3, deltas)


def _packed_row0(block):
    return (lax.broadcasted_iota(jnp.int32, (FB, 1), 0) == 0) & (block == 0)


def _packed_mac(acc, c, z):
    (ar, ai), (cr, ci), (zr, zi) = acc, c, z
    return ar + (zr * cr - zi * ci), ai + (zr * ci + zi * cr)


def _spec_kernel(f_ref, h_ref, o_ref, *, seg, n_seg):
    fblk = pl.program_id(1)
    fr, fi = f_ref[0, 0], f_ref[0, 1]
    row0 = _packed_row0(fblk)
    sign = jnp.where((lax.broadcasted_iota(jnp.int32, (FB, 1), 0) & 1) == 0, 1.0, -1.0)

    def first_half(d, m):
        taps = h_ref[d, m * seg:(m + 1) * seg, :]
        return _dot(fr, taps), _dot(fi, taps), taps[0:1, :].astype(F32)

    def causal_block(d, m, halves):
        re, im, _ = halves[m]
        if m >= 1:
            pre, pim, tap0 = halves[m - 1]
            re = re + sign * (pre - tap0)
            im = im + sign * (pim - jnp.where(row0, tap0, 0.0))
        return re, im

    def conj(re, im):
        return re, jnp.where(row0, im, -im)

    halves = [[first_half(d, m) for m in range(n_seg)] for d in range(2)]
    for li in range(2 * n_seg - 1):
        lag = li - (n_seg - 1)
        if lag > 0:
            re, im = causal_block(0, lag, halves[0])
        elif lag < 0:
            re, im = conj(*causal_block(1, -lag, halves[1]))
        else:
            fre, fim = causal_block(0, 0, halves[0])
            gre, gim = conj(*causal_block(1, 0, halves[1]))
            re, im = fre + gre, fim + gim
        o_ref[0, 0, li, 0] = re
        o_ref[0, 0, li, 1] = im


def _spectrum(L, seg, fmat, hfb):
    n_seg, nfb = L // seg, seg // FB
    n_lag = 2 * n_seg - 1
    return pl.pallas_call(
        functools.partial(_spec_kernel, seg=seg, n_seg=n_seg),
        out_shape=jax.ShapeDtypeStruct((HY_ORDER, nfb, n_lag, 2, FB, HY_WIDTH), F32),
        grid=(HY_ORDER, nfb),
        in_specs=[
            pl.BlockSpec((1, 2, FB, seg), lambda o, f: (f, 0, 0, 0)),
            pl.BlockSpec((2, L, HY_WIDTH), lambda o, f: (0, 0, o)),
        ],
        out_specs=pl.BlockSpec((1, 1, n_lag, 2, FB, HY_WIDTH), lambda o, f: (o, f, 0, 0, 0, 0)),
        compiler_params=_cp("parallel", "parallel"),
        name="hy_spectrum",
    )(fmat, hfb)


def _fwd_kernel(z_ref, f_ref, c_ref, y_ref, zb_ref, *, seg, n_seg):
    fblk = pl.program_id(1)

    @pl.when(fblk == 0)
    def _():
        zb_ref[...] = z_ref[0].astype(BF16)

    fr, fi = f_ref[0, 0], f_ref[0, 1]
    row0 = _packed_row0(fblk)
    acc = [None] * n_seg
    edge = [None] * n_seg
    for j in range(n_seg):
        zj = zb_ref[j * seg:(j + 1) * seg, :]
        z = (_dot(fr, zj), _dot(fi, zj))
        for i in range(n_seg):
            c = (c_ref[0, 0, i - j + n_seg - 1, 0], c_ref[0, 0, i - j + n_seg - 1, 1])
            e = (z[0][0:1] * c[0][0:1], z[1][0:1] * c[1][0:1])
            if acc[i] is None:
                acc[i] = (z[0] * c[0] - z[1] * c[1], z[0] * c[1] + z[1] * c[0])
                edge[i] = e
            else:
                acc[i] = _packed_mac(acc[i], c, z)
                edge[i] = (edge[i][0] + e[0], edge[i][1] + e[1])
    for i in range(n_seg):
        y_ref[0, i, 0:FB] = jnp.where(row0, edge[i][0], acc[i][0]).astype(BF16)
        y_ref[0, i, FB:2 * FB] = jnp.where(row0, edge[i][1], acc[i][1]).astype(BF16)


def _hy_fwd(L, seg, z, z_rowblk, z_colblk, fmat, spec, order):
    n_seg, nfb = L // seg, seg // FB
    n_lag = 2 * n_seg - 1
    return pl.pallas_call(
        functools.partial(_fwd_kernel, seg=seg, n_seg=n_seg),
        out_shape=jax.ShapeDtypeStruct((BATCH, n_seg, 2 * seg, HY_WIDTH), BF16),
        grid=(BATCH, nfb),
        in_specs=[
            pl.BlockSpec((1, L, HY_WIDTH), lambda b, f: (b, z_rowblk, z_colblk)),
            pl.BlockSpec((1, 2, FB, seg), lambda b, f: (f, 0, 0, 0)),
            pl.BlockSpec((1, 1, n_lag, 2, FB, HY_WIDTH), lambda b, f: (order, f, 0, 0, 0, 0)),
        ],
        out_specs=pl.BlockSpec((1, n_seg, 2 * FB, HY_WIDTH), lambda b, f: (b, 0, f, 0)),
        scratch_shapes=[pltpu.VMEM((L, HY_WIDTH), BF16)],
        compiler_params=_cp("parallel", "arbitrary"),
        name="hy_fwd",
    )(z, fmat, spec)


def _inv_kernel(y_ref, g_ref, gate_ref, z_ref, bias_ref, o_ref):
    y = _dot(g_ref[...], y_ref[0, 0])
    o_ref[0] = gate_ref[0] * (y + bias_ref[0, 0] * z_ref[0])


def _hy_inv(l, L, seg, yf, gmat, u, row0, gate_colblk, z, z_row0, z_colblk, flt_bias, order):
    tmi = min(seg, 2 * TM)
    per_seg = seg // tmi
    r0, zr0 = row0 // tmi, z_row0 // tmi
    return pl.pallas_call(
        _inv_kernel,
        out_shape=jax.ShapeDtypeStruct((BATCH, L, HY_WIDTH), F32),
        grid=(BATCH, L // tmi),
        in_specs=[
            pl.BlockSpec((1, 1, 2 * seg, HY_WIDTH), lambda b, i: (b, i // per_seg, 0, 0)),
            pl.BlockSpec((tmi, 2 * seg), lambda b, i: (i % per_seg, 0)),
            pl.BlockSpec((1, tmi, HY_WIDTH), lambda b, i: (b, r0 + i, gate_colblk)),
            pl.BlockSpec((1, tmi, HY_WIDTH), lambda b, i: (b, zr0 + i, z_colblk)),
            pl.BlockSpec((1, 1, 1, HY_WIDTH), lambda b, i: (l, order, 0, 0)),
        ],
        out_specs=pl.BlockSpec((1, tmi, HY_WIDTH), lambda b, i: (b, i, 0)),
        compiler_params=_cp("parallel", "parallel"),
        name="hy_inv",
    )(yf, gmat, u, z, flt_bias)


def _route_tile(lg, bias):
    lane = lax.broadcasted_iota(jnp.int32, lg.shape, 1)
    lane_f = lane.astype(F32)
    neg = jnp.float32(-jnp.inf)
    big = jnp.float32(ROUTER_PAD)
    biased = lg + bias

    def first_argmax(v):
        m = jnp.max(v, axis=-1, keepdims=True)
        return jnp.min(jnp.where(v == m, lane_f, big), axis=-1, keepdims=True).astype(jnp.int32)

    def pick(v, idx):
        return jnp.sum(jnp.where(lane == idx, v, 0.0), axis=-1, keepdims=True)

    is_group = lane < N_GROUPS
    g_sel = first_argmax(jnp.where(is_group, biased, neg))
    raw_g = jnp.where(is_group, lg, neg)
    e_g = jnp.exp(raw_g - jnp.max(raw_g, axis=-1, keepdims=True))
    p_g = pick(e_g, g_sel) / jnp.sum(e_g, axis=-1, keepdims=True)

    lo = N_GROUPS + g_sel * EXP_PER_GROUP
    cand = jnp.where((lane >= lo) & (lane < lo + EXP_PER_GROUP), biased, neg)
    i1 = first_argmax(cand)
    i2 = first_argmax(jnp.where(lane == i1, neg, cand))
    l1, l2 = pick(lg, i1), pick(lg, i2)
    top = jnp.maximum(l1, l2)
    e1, e2 = jnp.exp(l1 - top), jnp.exp(l2 - top)
    inv = p_g / (e1 + e2)
    return (i1 - N_GROUPS, i2 - N_GROUPS), (e1 * inv, e2 * inv)


def _merge_kernel(h_ref, sg_ref, hyl_ref, hyc_ref, attl_ref, attc_ref, mod_ref, g_ref,
                  who_ref, wmo_ref, wout_ref, wr_ref, br_ref, h1_ref, u2_ref, eid_ref, ew_ref, cnt_ref):
    is_ctx = pl.program_id(1) >= N_LAT_TILES
    hy = jnp.where(is_ctx, hyc_ref[0], hyl_ref[0]).astype(BF16)
    att = jnp.where(is_ctx, attc_ref[0], attl_ref[0])
    sg = sg_ref[0].astype(F32)
    m = sg[:, :D_MODEL] * _dot(hy, who_ref[0]) + sg[:, D_MODEL:] * _dot(att, wmo_ref[0])
    y = _dot(m.astype(BF16), wout_ref[0])
    g1 = mod_ref[0, 0, 0, 2:3, :]
    sh2 = mod_ref[0, 0, 0, 3:4, :]
    sc2 = mod_ref[0, 0, 0, 4:5, :]
    h1 = h_ref[0] + g1 * y
    h1_ref[0] = h1
    u2 = _rms(h1, g_ref[0]) * (1.0 + sc2) + sh2
    u2_ref[0] = u2
    ids, ws = _route_tile(_dot3(u2, wr_ref[0]), br_ref[0])
    lane = lax.broadcasted_iota(jnp.int32, (TM, ROUTER_PAD), 1)
    hist = jnp.zeros((1, ROUTER_PAD), F32)
    for c in range(TOP_K):
        eid_ref[0, :, c:c + 1] = ids[c]
        ew_ref[0, :, c:c + 1] = ws[c]
        hist = hist + jnp.sum(jnp.where(lane == ids[c], 1.0, 0.0), axis=0, keepdims=True)
    cnt_ref[0, 0] = hist


def _merge(l, n_tiles, h, sg, hy_lat, hy_ctx, att_lat, att_ctx, mods, norm2_g, who, wmo, wout, wr, br):
    rows = n_tiles * TM
    tile = lambda w: pl.BlockSpec((1, TM, w), lambda b, i: (b, i, 0))
    lat = lambda w: pl.BlockSpec((1, TM, w), lambda b, i: (b, jnp.minimum(i, N_LAT_TILES - 1), 0))
    ctx = lambda w: pl.BlockSpec((1, TM, w), lambda b, i: (b, 0, 0))
    wfull = lambda a: pl.BlockSpec((1,) + a.shape[1:], lambda b, i: (l, 0, 0))
    return pl.pallas_call(
        _merge_kernel,
        out_shape=(
            jax.ShapeDtypeStruct((BATCH, rows, D_MODEL), F32),
            jax.ShapeDtypeStruct((BATCH, rows, D_MODEL), F32),
            jax.ShapeDtypeStruct((BATCH, rows, TOP_K), jnp.int32),
            jax.ShapeDtypeStruct((BATCH, rows, TOP_K), F32),
            jax.ShapeDtypeStruct((BATCH, n_tiles, 1, ROUTER_PAD), F32),
        ),
        grid=(BATCH, n_tiles),
        in_specs=[
            tile(D_MODEL), tile(2 * D_MODEL),
            lat(HY_WIDTH), ctx(HY_WIDTH), lat(MLA_HEADS * V_DIM), ctx(MLA_HEADS * V_DIM),
            pl.BlockSpec((1, 1, 1, N_MOD, D_MODEL), lambda b, i: (l, b, i // N_LAT_TILES, 0, 0)),
            pl.BlockSpec((1, 1, D_MODEL), lambda b, i: (l, 0, 0)),
            wfull(who), wfull(wmo), wfull(wout), wfull(wr), wfull(br),
        ],
        out_specs=(tile(D_MODEL), tile(D_MODEL), tile(TOP_K), tile(TOP_K),
                   pl.BlockSpec((1, 1, 1, ROUTER_PAD), lambda b, i: (b, i, 0, 0))),
        compiler_params=_cp("parallel", "parallel"),
        name="merge",
    )(h, sg, hy_lat, hy_ctx, att_lat, att_ctx, mods, norm2_g, who, wmo, wout, wr, br)


def _expert_kernel(be_ref, b0_ref, nr_ref, src_ref, dst_ref, u_hbm, wg_ref, wu_ref, wd_ref, y_hbm,
                   xbuf, ybuf, gsem, ssem):
    i = pl.program_id(0)
    nb = pl.num_programs(0)
    slot = i % 2

    def in_copy(blk, sl, g, j):
        tok = src_ref[b0_ref[blk] + g * SUBLANES + j]
        return pltpu.make_async_copy(u_hbm.at[pl.ds(tok, 1)], xbuf.at[sl, g, pl.ds(j, 1)], gsem.at[sl])

    def out_copy(blk, sl, g, j):
        r = g * SUBLANES + j
        return pltpu.make_async_copy(ybuf.at[sl, pl.ds(r, 1)], y_hbm.at[pl.ds(dst_ref[b0_ref[blk] + r], 1)],
                                     ssem.at[sl])

    def for_rows(blk, fn):
        n = nr_ref[blk]
        n_groups = n // SUBLANES

        def group(g, carry):
            for j in range(SUBLANES):
                fn(g, j)
            return carry

        def single(r, carry):
            fn(n_groups, r)
            return carry

        lax.fori_loop(0, n_groups, group, 0)
        lax.fori_loop(0, n - n_groups * SUBLANES, single, 0)

    @pl.when(i == 0)
    def _():
        xbuf[...] = jnp.zeros_like(xbuf)
        for_rows(0, lambda g, j: in_copy(0, 0, g, j).start())

    @pl.when(i + 1 < nb)
    def _():
        for_rows(i + 1, lambda g, j: in_copy(i + 1, 1 - slot, g, j).start())

    for_rows(i, lambda g, j: in_copy(i, slot, g, j).wait())

    @pl.when(i >= 2)
    def _():
        for_rows(i - 2, lambda g, j: out_copy(i - 2, slot, g, j).wait())

    @pl.when(nr_ref[i] > 0)
    def _():
        x = xbuf[slot].reshape(MOE_BM, D_MODEL).astype(BF16)
        hg = _dot(x, wg_ref[0, 0].astype(BF16))
        hu = _dot(x, wu_ref[0, 0].astype(BF16))
        hb = (hg * jax.nn.sigmoid(hg) * hu).astype(BF16)
        ybuf[slot] = _dot(hb, wd_ref[0, 0].astype(BF16))

    for_rows(i, lambda g, j: out_copy(i, slot, g, j).start())

    @pl.when(i == nb - 1)
    def _():
        for_rows(i - 1, lambda g, j: out_copy(i - 1, 1 - slot, g, j).wait())
        for_rows(i, lambda g, j: out_copy(i, slot, g, j).wait())


def _experts(l, block_expert, block_row0, block_rows, order, u, w_gate_e, w_up_e, w_down_e):
    n_blocks = block_expert.shape[0]
    n_tok = u.shape[0]
    src = lax.shift_right_logical(order, 1)
    dst = (order & 1) * n_tok + src
    wspec = lambda r, c: pl.BlockSpec((1, 1, r, c), lambda i, be, b0, nr, sr, ds: (l, be[i], 0, 0))
    grid_spec = pltpu.PrefetchScalarGridSpec(
        num_scalar_prefetch=5,
        grid=(n_blocks,),
        in_specs=[
            pl.BlockSpec(memory_space=pl.ANY),
            wspec(D_MODEL, D_EXPERT), wspec(D_MODEL, D_EXPERT), wspec(D_EXPERT, D_MODEL),
        ],
        out_specs=pl.BlockSpec(memory_space=pl.ANY),
        scratch_shapes=[
            pltpu.VMEM((2, MOE_BM // SUBLANES, SUBLANES, D_MODEL), F32),
            pltpu.VMEM((2, MOE_BM, D_MODEL), F32),
            pltpu.SemaphoreType.DMA((2,)),
            pltpu.SemaphoreType.DMA((2,)),
        ],
    )
    return pl.pallas_call(
        _expert_kernel,
        out_shape=jax.ShapeDtypeStruct((TOP_K * u.shape[0], D_MODEL), F32),
        grid_spec=grid_spec,
        compiler_params=_cp("arbitrary"),
        name="experts",
    )(block_expert, block_row0, block_rows, src, dst, u, w_gate_e, w_up_e, w_down_e)


def _dispatch(expert, counts):
    A = expert.shape[0]
    _, order = lax.sort((expert, jnp.arange(A, dtype=jnp.int32)), num_keys=1)
    start = jnp.cumsum(counts) - counts
    n_blk = (counts + MOE_BM - 1) // MOE_BM
    blk_end = jnp.cumsum(n_blk)
    blk_start = blk_end - n_blk
    n_blocks = -(-A // MOE_BM) + N_EXPERTS
    b = jnp.arange(n_blocks, dtype=jnp.int32)[:, None]
    mine = (b >= blk_start[None, :]) & (b < blk_end[None, :])
    used = jnp.any(mine, axis=1)
    pick = lambda table: jnp.sum(jnp.where(mine, table[None, :], 0), axis=1)
    row_in_expert = (b[:, 0] - pick(blk_start)) * MOE_BM
    block_expert = jnp.where(used, pick(jnp.arange(N_EXPERTS, dtype=jnp.int32)), N_EXPERTS - 1)
    block_rows = jnp.where(used, jnp.clip(pick(counts) - row_in_expert, 0, MOE_BM), 0)
    block_row0 = jnp.where(used, pick(start) + row_in_expert, 0)
    return order, block_expert.astype(jnp.int32), block_row0.astype(jnp.int32), block_rows.astype(jnp.int32)


def _moe_sum(y_ref, w_ref):
    w = w_ref[0]
    return sum(w[:, c:c + 1] * y_ref[c, 0] for c in range(TOP_K))


def _combine_kernel(h_ref, y_ref, w_ref, mod_ref, o_ref):
    o_ref[0] = h_ref[0] + mod_ref[0, 0, 0, 5:6, :] * _moe_sum(y_ref, w_ref)


def _final_kernel(h_ref, y_ref, w_ref, mod_ref, g_ref, o_ref):
    h2 = h_ref[0] + mod_ref[0, 0, 0, 5:6, :] * _moe_sum(y_ref, w_ref)
    o_ref[0] = _rms(h2, g_ref[...])


def _combine(l, n_tiles, h1, y, w, mods, final_g=None):
    rows = n_tiles * TM
    tile = pl.BlockSpec((1, TM, D_MODEL), lambda b, i: (b, i, 0))
    in_specs = [tile, pl.BlockSpec((TOP_K, 1, TM, D_MODEL), lambda b, i: (0, b, i, 0)),
                pl.BlockSpec((1, TM, TOP_K), lambda b, i: (b, i, 0)),
                pl.BlockSpec((1, 1, 1, N_MOD, D_MODEL), lambda b, i: (l, b, i // N_LAT_TILES, 0, 0))]
    args = [h1, y, w, mods]
    kern = _combine_kernel
    if final_g is not None:
        in_specs.append(pl.BlockSpec((1, D_MODEL), lambda b, i: (0, 0)))
        args.append(final_g.reshape(1, D_MODEL))
        kern = _final_kernel
    return pl.pallas_call(
        kern,
        out_shape=jax.ShapeDtypeStruct((BATCH, rows, D_MODEL), F32),
        grid=(BATCH, n_tiles),
        in_specs=in_specs,
        out_specs=tile,
        compiler_params=_cp("parallel", "parallel"),
        name="combine",
    )(*args)


def _rope_tables():
    n = jnp.arange(SEQ)
    pos = jnp.stack([n // GRID_W, n % GRID_W], axis=-1).astype(F32)
    inv = ROPE_BASE ** (-jnp.arange(ROPE_PAIRS, dtype=F32) / ROPE_PAIRS)
    ang = pos[:, :, None] * inv
    cos, sin = jnp.cos(ang), jnp.sin(ang)
    cos32 = jnp.stack([cos, cos], axis=2).reshape(SEQ, QK_ROPE)
    sin32 = jnp.stack([-sin, sin], axis=2).reshape(SEQ, QK_ROPE)
    one = jnp.ones((SEQ, QK_ROPE), F32)
    zero = jnp.zeros((SEQ, QK_ROPE), F32)
    cosq = jnp.concatenate([jnp.ones((SEQ, QK_NOPE), F32), cos32, one], axis=1) * Q_SCALE
    sinq = jnp.concatenate([jnp.zeros((SEQ, QK_NOPE), F32), sin32, zero], axis=1) * Q_SCALE
    cosk = jnp.concatenate([jnp.zeros((SEQ, QK_NOPE), F32), cos32, zero], axis=1)
    sink = jnp.concatenate([jnp.zeros((SEQ, QK_NOPE), F32), sin32, zero], axis=1)
    lat = jnp.stack([cosq, sinq, cosk, sink])
    ctx_row = jnp.ones((HEAD_PAD,), F32) * Q_SCALE
    ctx_cosk = jnp.concatenate([jnp.zeros((QK_NOPE + QK_ROPE,), F32), jnp.ones((QK_ROPE,), F32)])
    ctx = jnp.stack([ctx_row, jnp.zeros_like(ctx_row), ctx_cosk, jnp.zeros_like(ctx_row)])
    ctx = jnp.broadcast_to(ctx[:, None, :], (4, CTX_LEN, HEAD_PAD))
    return jnp.concatenate([lat, ctx], axis=1)


def _dft_matrices(L):
    N = 2 * L
    nfb = L // FB
    t = jnp.arange(L, dtype=jnp.int32)

    def tables(n_rows, step):
        ang = ((jnp.arange(n_rows, dtype=jnp.int32)[:, None] * step * t[None, :]) % N).astype(F32)
        ang = ang * (2.0 * math.pi / N)
        return jnp.cos(ang), jnp.sin(ang)

    ca, sa = tables(nfb, FB)
    cb, sb = tables(FB, 1)
    alt = jnp.where((t % 2) == 0, 1.0, -1.0).astype(F32)
    i_idx = jnp.arange(nfb)[:, None, None]
    r_idx = jnp.arange(FB)[None, :, None]
    dc = (i_idx == 0) & (r_idx == 0)
    c = ca[:, None, :] * cb[None] - sa[:, None, :] * sb[None]
    s = jnp.where(dc, alt, -(sa[:, None, :] * cb[None] + ca[:, None, :] * sb[None]))
    f = jnp.stack([c, s], axis=1).astype(BF16)
    wk = jnp.where(dc, 1.0 / N, 2.0 / N).astype(F32)[..., 0]
    cat, sat, cbt, sbt = ca.T[:, :, None], sa.T[:, :, None], cb.T[:, None, :], sb.T[:, None, :]
    ct = (cat * cbt - sat * sbt) * wk
    st = jnp.where(dc[..., 0], alt[:, None, None], -(sat * cbt + cat * sbt)) * wk
    g = jnp.stack([ct, st], axis=2).reshape(L, 2 * L).astype(BF16)
    return f, g


def _filter_embedding(L):
    t = jnp.linspace(0.0, 1.0, L, dtype=F32)[:, None]
    w = 2.0 * math.pi * jnp.arange(L, dtype=F32)[:, None] / L
    bands = jnp.linspace(1e-4, HY_BANDS - 1, HY_BANDS, dtype=F32)[None, :]
    return jnp.concatenate([t, jnp.cos(bands * w), -jnp.sin(bands * w),
                            jnp.zeros((L, LANES - HY_EMB), F32)], axis=-1)


def _rope_swap_perm():
    idx = np.arange(QK_ROPE)
    axis, half, pair = idx // (2 * ROPE_PAIRS), (idx // ROPE_PAIRS) % 2, idx % ROPE_PAIRS
    return axis * 2 * ROPE_PAIRS + (1 - half) * ROPE_PAIRS + pair


def _prep_weights(w_in, w_uq, w_ukv, w_hy_o, w_mla_o, w_out, w_router_g, w_router_e):
    perm = _rope_swap_perm()
    c_q = HY_COLS
    c_kv = HY_COLS + Q_LORA
    c_kr = c_kv + KV_LORA
    w_kr = w_in[:, :, c_kr:MLA_END]
    zeros = lambda n: jnp.zeros((DEPTH, D_MODEL, n), F32)
    wkr = jnp.concatenate([zeros(QK_NOPE), w_kr, w_kr,
                           zeros(QK_NOPE), w_kr[:, :, perm], zeros(QK_ROPE)], axis=-1)
    uq = w_uq.reshape(DEPTH, Q_LORA, MLA_HEADS, QK_NOPE + QK_ROPE)
    uq_r = uq[..., QK_NOPE:]
    wuq = jnp.concatenate([uq, uq_r], axis=-1).reshape(DEPTH, Q_LORA, MLA_HEADS * HEAD_PAD)
    wuqs = jnp.concatenate([jnp.zeros_like(uq[..., :QK_NOPE]), uq_r[..., perm], jnp.zeros_like(uq_r)],
                           axis=-1).reshape(DEPTH, Q_LORA, MLA_HEADS * HEAD_PAD)
    ukv = w_ukv.reshape(DEPTH, KV_LORA, MLA_HEADS, QK_NOPE + V_DIM)
    zpad = jnp.zeros_like(ukv[..., :HEAD_PAD - QK_NOPE])
    wuk = jnp.concatenate([ukv[..., :QK_NOPE], zpad], axis=-1).reshape(DEPTH, KV_LORA, MLA_HEADS * HEAD_PAD)
    wuv = jnp.concatenate([ukv[..., QK_NOPE:], zpad], axis=-1).reshape(DEPTH, KV_LORA, MLA_HEADS * HEAD_PAD)
    wr = jnp.concatenate([w_router_g, w_router_e,
                          jnp.zeros((DEPTH, D_MODEL, ROUTER_PAD - N_GROUPS - N_EXPERTS), F32)], axis=-1)
    bf = lambda a: a.astype(BF16)
    return dict(
        why=bf(w_in[:, :, :HY_COLS]), wgt=bf(w_in[:, :, MLA_END:]),
        wq=bf(w_in[:, :, c_q:c_kv]), wkv=bf(w_in[:, :, c_kv:c_kr]), wkr=bf(wkr),
        wuq=bf(wuq), wuqs=bf(wuqs), wuk=bf(wuk), wuv=bf(wuv),
        who=bf(w_hy_o), wmo=bf(w_mla_o), wout=bf(w_out), wr=wr,
    )


def _hyena(l, L, row0, u, fmat, gmat, zemb, deltas, flt, flt_bias):
    seg = fmat.shape[-1]
    spec = _spectrum(L, seg, fmat, _filters(l, L, zemb, deltas, flt))
    rb = row0 // L
    yf = _hy_fwd(L, seg, u, rb, 0, fmat, spec, 0)
    z1 = _hy_inv(l, L, seg, yf, gmat, u, row0, 1, u, row0, 0, flt_bias, 0)
    yf = _hy_fwd(L, seg, z1, 0, 0, fmat, spec, 1)
    return _hy_inv(l, L, seg, yf, gmat, u, row0, 2, z1, 0, 0, flt_bias, 1)


def kernel(x, c, ctx, c_ctx, w_ada, b_ada, norm1_g, w_in, q_norm_g, kv_norm_g, w_uq, w_ukv,
           sc_w, sc_b, flt_w1, flt_b1, flt_freq, flt_w2, flt_b2, flt_w3, flt_bias,
           w_hy_o, w_mla_o, w_out, norm2_g, w_router_g, b_router_g, w_router_e, b_router_e,
           w_gate_e, w_up_e, w_down_e, final_g):
    wts = _prep_weights(w_in, w_uq, w_ukv, w_hy_o, w_mla_o, w_out, w_router_g, w_router_e)
    tabs = _rope_tables()
    f_lat, g_lat = _dft_matrices(min(SEQ, HY_SEG))
    f_ctx, g_ctx = _dft_matrices(min(CTX_LEN, HY_SEG))
    z_lat, z_ctx = _filter_embedding(SEQ), _filter_embedding(CTX_LEN)
    deltas = jnp.linspace(math.log(HY_TARGET) / HY_FAST_DECAY, math.log(HY_TARGET) / HY_SLOW_DECAY,
                          HY_WIDTH, dtype=F32)[None, :]

    cc = jnp.concatenate([c, c_ctx[None, :], jnp.zeros((3, D_MODEL), F32)], axis=0)
    mod = _ada(cc, w_ada, b_ada).reshape(DEPTH, 8, N_MOD, D_MODEL)
    mods = jnp.stack([mod[:, :BATCH], jnp.broadcast_to(mod[:, BATCH:BATCH + 1], (DEPTH, BATCH, N_MOD, D_MODEL))],
                     axis=2)

    r3 = lambda a: a.reshape(DEPTH, 1, a.shape[-1])
    norm1, norm2, qg, kvg = r3(norm1_g), r3(norm2_g), r3(q_norm_g), r3(kv_norm_g)
    scb = r3(sc_b)
    fb1, ffr, fb2 = r3(flt_b1), r3(flt_freq), r3(flt_b2)
    fbias = flt_bias.reshape(DEPTH, HY_ORDER, 1, HY_WIDTH)
    w1_pad = jnp.concatenate([flt_w1, jnp.zeros((DEPTH, LANES - HY_EMB, HY_FFN), F32)], axis=1)
    flt = (w1_pad, fb1, ffr, flt_w2, fb2, flt_w3)

    b_router = jnp.concatenate([b_router_g, b_router_e,
                                jnp.zeros((DEPTH, ROUTER_PAD - N_GROUPS - N_EXPERTS), F32)], axis=-1)
    b_router = b_router.reshape(DEPTH, 1, ROUTER_PAD)

    h = jnp.concatenate([x, ctx], axis=1)
    for l in range(DEPTH):
        last = l == DEPTH - 1
        n_tiles = N_LAT_TILES if last else N_ALL_TILES
        u, sg, q, k, v = _inproj(l, h, mods, norm1, tabs, qg, kvg, sc_w, scb, wts)
        att_lat = _attention(q, k, v, latent=True)
        hy_lat = _hyena(l, SEQ, 0, u, f_lat, g_lat, z_lat, deltas, flt, fbias)
        if last:
            att_ctx, hy_ctx = att_lat, hy_lat
        else:
            att_ctx = _attention(q, k, v, latent=False)
            hy_ctx = _hyena(l, CTX_LEN, SEQ, u, f_ctx, g_ctx, z_ctx, deltas, flt, fbias)

        h1, u2, eid, ew, hist = _merge(l, n_tiles, h, sg, hy_lat, hy_ctx, att_lat, att_ctx, mods, norm2,
                                       wts["who"], wts["wmo"], wts["wout"], wts["wr"], b_router)
        counts = jnp.sum(hist, axis=(0, 1, 2))[:N_EXPERTS].astype(jnp.int32)

        T = BATCH * n_tiles * TM
        order, block_expert, block_row0, block_rows = _dispatch(eid.reshape(T * TOP_K), counts)
        y = _experts(l, block_expert, block_row0, block_rows, order, u2.reshape(T, D_MODEL),
                     w_gate_e, w_up_e, w_down_e)
        h = _combine(l, n_tiles, h1, y.reshape(TOP_K, BATCH, n_tiles * TM, D_MODEL), ew, mods,
                     final_g if last else None)
    return h
```

```python
import functools
import math

import jax
import jax.numpy as jnp
import numpy as np
from jax import lax
from jax.experimental import pallas as pl
from jax.experimental.pallas import tpu as pltpu

F32 = jnp.float32
BF16 = jnp.bfloat16

D_MODEL = 1024
BATCH = 4
SEQ = 4096
DEPTH = 4
GRID_W = 64
CTX_LEN = 256
S_ALL = SEQ + CTX_LEN
EPS = 1e-6
N_MOD = 6

HY_WIDTH = 512
HY_ORDER = 2
HY_BANDS = 16
HY_EMB = 1 + 2 * HY_BANDS
HY_FFN = 64
SHORT_K = 3
HY_FAST_DECAY = 0.3
HY_SLOW_DECAY = 1.5
HY_TARGET = 1e-2

MLA_HEADS = 8
QK_NOPE = 64
QK_ROPE = 32
V_DIM = 64
Q_LORA = 384
KV_LORA = 256
ROPE_PAIRS = QK_ROPE // 4
ROPE_BASE = 10000.0
ATTN_SCALE = (QK_NOPE + QK_ROPE) ** -0.5
Q_SCALE = ATTN_SCALE * math.log2(math.e)

N_GROUPS = 8
EXP_PER_GROUP = 8
N_EXPERTS = N_GROUPS * EXP_PER_GROUP
TOP_K = 2
D_EXPERT = 256

HY_COLS = (HY_ORDER + 1) * HY_WIDTH
MLA_END = HY_COLS + Q_LORA + KV_LORA + QK_ROPE

LANES = 128
HEAD_PAD = LANES
V7X_VMEM_BYTES = 64 << 20
VMEM_LIMIT = V7X_VMEM_BYTES * 7 // 8

TM = 256
N_LAT_TILES = SEQ // TM
N_ALL_TILES = S_ALL // TM
TQ = 1024
TK = 2048
ATT_HEADS = 4
FB = 256
HY_SEG = 1024
MOE_BM = 256
SUBLANES = 8
ROUTER_PAD = LANES


def _cp(*sem):
    return pltpu.CompilerParams(dimension_semantics=sem, vmem_limit_bytes=VMEM_LIMIT)


def _dot(a, b):
    return jnp.dot(a, b, preferred_element_type=F32)


def _split(a):
    hi = a.astype(BF16)
    lo = (a - hi.astype(F32)).astype(BF16)
    return hi, lo


def _dot3(a, b):
    ah, al = _split(a)
    bh, bl = _split(b)
    return _dot(ah, bh) + (_dot(ah, bl) + _dot(al, bh))


def _rms(x, g):
    return x * lax.rsqrt(jnp.mean(x * x, axis=-1, keepdims=True) + EPS) * g


def _ada_kernel(c_ref, w_ref, b_ref, o_ref):
    cc = c_ref[...]
    s = cc * jax.nn.sigmoid(cc)
    o_ref[0] = _dot3(s, w_ref[0]) + b_ref[0]


def _ada(cc, w_ada, b_ada):
    tn = 1536
    n = N_MOD * D_MODEL
    return pl.pallas_call(
        _ada_kernel,
        out_shape=jax.ShapeDtypeStruct((DEPTH, 8, n), F32),
        grid=(DEPTH, n // tn),
        in_specs=[
            pl.BlockSpec((8, D_MODEL), lambda l, j: (0, 0)),
            pl.BlockSpec((1, D_MODEL, tn), lambda l, j: (l, 0, j)),
            pl.BlockSpec((1, 1, tn), lambda l, j: (l, 0, j)),
        ],
        out_specs=pl.BlockSpec((1, 8, tn), lambda l, j: (l, 0, j)),
        compiler_params=_cp("parallel", "parallel"),
        name="ada",
    )(cc, w_ada, b_ada.reshape(DEPTH, 1, n))


def _inproj_kernel(h_ref, hprev_ref, hnext_ref, mod_ref, g_ref, tab_ref, qg_ref, kvg_ref, scw_ref, scb_ref,
                   why_ref, wgt_ref, wq_ref, wkv_ref, wkr_ref, wuq_ref, wuqs_ref, wuk_ref, wuv_ref,
                   uhy_ref, sg_ref, q_ref, k_ref, v_ref):
    i = pl.program_id(1)
    shift = mod_ref[0, 0, 0, 0:1, :]
    scale = mod_ref[0, 0, 0, 1:2, :]

    def modulated(rows):
        return (_rms(rows, g_ref[0]) * (1.0 + scale) + shift).astype(BF16)

    u = modulated(h_ref[0])

    p = _dot(u, why_ref[0])
    p_halo = _dot(modulated(jnp.concatenate([hprev_ref[0], hnext_ref[0]], axis=0)), why_ref[0])
    starts_stream = (i == 0) | (i == N_LAT_TILES)
    ends_stream = (i == N_LAT_TILES - 1) | (i == N_ALL_TILES - 1)
    p_before = jnp.where(starts_stream, 0.0, p_halo[SUBLANES - 1:SUBLANES])
    p_after = jnp.where(ends_stream, 0.0, p_halo[SUBLANES:SUBLANES + 1])
    row = lax.broadcasted_iota(jnp.int32, (TM, 1), 0)
    prev = jnp.where(row == 0, p_before, pltpu.roll(p, 1, 0))
    nxt = jnp.where(row == TM - 1, p_after, pltpu.roll(p, TM - 1, 0))
    w = scw_ref[0]
    uhy_ref[0] = scb_ref[0] + prev * w[0:1] + p * w[1:2] + nxt * w[2:3]

    sg_ref[0] = jax.nn.sigmoid(_dot(u, wgt_ref[0])).astype(BF16)

    cosq, sinq, cosk, sink = tab_ref[0], tab_ref[1], tab_ref[2], tab_ref[3]
    cq = _rms(_dot(u, wq_ref[0]), qg_ref[0]).astype(BF16)
    qa = _dot(cq, wuq_ref[0])
    qs = _dot(cq, wuqs_ref[0])
    ckv = _rms(_dot(u, wkv_ref[0]), kvg_ref[0]).astype(BF16)
    ka = _dot(ckv, wuk_ref[0])
    va = _dot(ckv, wuv_ref[0])
    kr = _dot(u, wkr_ref[0])
    krk = kr[:, :HEAD_PAD] * cosk + kr[:, HEAD_PAD:] * sink
    ones_col = (lax.broadcasted_iota(jnp.int32, (1, HEAD_PAD), 1) == V_DIM).astype(F32)
    for hh in range(MLA_HEADS):
        sl = slice(hh * HEAD_PAD, (hh + 1) * HEAD_PAD)
        q_ref[0, hh] = (qa[:, sl] * cosq + qs[:, sl] * sinq).astype(BF16)
        k_ref[0, hh] = (ka[:, sl] + krk).astype(BF16)
        v_ref[0, hh] = (va[:, sl] + ones_col).astype(BF16)


def _inproj(l, h, mods, norm1_g, tabs, q_norm_g, kv_norm_g, sc_w, sc_b, wts):
    names = ("why", "wgt", "wq", "wkv", "wkr", "wuq", "wuqs", "wuk", "wuv")
    w_specs = [pl.BlockSpec((1,) + wts[n].shape[1:], lambda b, i: (l, 0, 0)) for n in names]
    qkv_shape = jax.ShapeDtypeStruct((BATCH, MLA_HEADS, S_ALL, HEAD_PAD), BF16)
    qkv_spec = pl.BlockSpec((1, MLA_HEADS, TM, HEAD_PAD), lambda b, i: (b, 0, i, 0))
    halo_blocks = TM // SUBLANES
    last_halo = S_ALL // SUBLANES - 1
    return pl.pallas_call(
        _inproj_kernel,
        out_shape=(
            jax.ShapeDtypeStruct((BATCH, S_ALL, HY_COLS), F32),
            jax.ShapeDtypeStruct((BATCH, S_ALL, 2 * D_MODEL), BF16),
            qkv_shape, qkv_shape, qkv_shape,
        ),
        grid=(BATCH, N_ALL_TILES),
        in_specs=[
            pl.BlockSpec((1, TM, D_MODEL), lambda b, i: (b, i, 0)),
            pl.BlockSpec((1, SUBLANES, D_MODEL), lambda b, i: (b, jnp.maximum(i * halo_blocks - 1, 0), 0)),
            pl.BlockSpec((1, SUBLANES, D_MODEL),
                         lambda b, i: (b, jnp.minimum((i + 1) * halo_blocks, last_halo), 0)),
            pl.BlockSpec((1, 1, 1, N_MOD, D_MODEL), lambda b, i: (l, b, i // N_LAT_TILES, 0, 0)),
            pl.BlockSpec((1, 1, D_MODEL), lambda b, i: (l, 0, 0)),
            pl.BlockSpec((4, TM, HEAD_PAD), lambda b, i: (0, i, 0)),
            pl.BlockSpec((1, 1, Q_LORA), lambda b, i: (l, 0, 0)),
            pl.BlockSpec((1, 1, KV_LORA), lambda b, i: (l, 0, 0)),
            pl.BlockSpec((1, SHORT_K, HY_COLS), lambda b, i: (l, 0, 0)),
            pl.BlockSpec((1, 1, HY_COLS), lambda b, i: (l, 0, 0)),
        ] + w_specs,
        out_specs=(
            pl.BlockSpec((1, TM, HY_COLS), lambda b, i: (b, i, 0)),
            pl.BlockSpec((1, TM, 2 * D_MODEL), lambda b, i: (b, i, 0)),
            qkv_spec, qkv_spec, qkv_spec,
        ),
        compiler_params=_cp("parallel", "parallel"),
        name="inproj",
    )(h, h, h, mods, norm1_g, tabs, q_norm_g, kv_norm_g, sc_w, sc_b, *[wts[n] for n in names])


def _flash_step(q_ref, k_ref, v_ref, rows, carry):
    out = []
    for hh in range(ATT_HEADS):
        m, acc = carry[hh]
        s = lax.dot_general(q_ref[0, hh], k_ref[0, hh, rows, :], (((1,), (1,)), ((), ())),
                            preferred_element_type=F32)
        m_new = jnp.maximum(m, jnp.max(s, axis=-1, keepdims=True))
        p = jnp.exp2(s - m_new)
        acc = acc * jnp.exp2(m - m_new) + _dot(p.astype(BF16), v_ref[0, hh, rows, :])
        out.append((m_new, acc))
    return tuple(out)


def _attn_kernel(q_ref, k_ref, v_ref, o_ref, *, chunks):
    tq = q_ref.shape[2]
    init = (jnp.full((tq, 1), -1e30, F32), jnp.zeros((tq, HEAD_PAD), F32))
    carry = (init,) * ATT_HEADS
    start = 0
    for size in chunks:
        carry = _flash_step(q_ref, k_ref, v_ref, pl.ds(start, size), carry)
        start += size
    outs = [acc[:, :V_DIM] / acc[:, V_DIM:V_DIM + 1] for _, acc in carry]
    o_ref[0] = jnp.concatenate(outs, axis=-1).astype(BF16)


def _attention(q, k, v, latent):
    if latent:
        tq, nq, q0, kv_rows, kv_blk = TQ, SEQ // TQ, 0, S_ALL, 0
        kern = functools.partial(_attn_kernel, chunks=(TK,) * (SEQ // TK - 1) + (TK + CTX_LEN,))
    else:
        tq, nq, q0, kv_rows, kv_blk = CTX_LEN, 1, SEQ // CTX_LEN, CTX_LEN, SEQ // CTX_LEN
        kern = functools.partial(_attn_kernel, chunks=(CTX_LEN,))
    return pl.pallas_call(
        kern,
        out_shape=jax.ShapeDtypeStruct((BATCH, nq * tq, MLA_HEADS * V_DIM), BF16),
        grid=(BATCH, MLA_HEADS // ATT_HEADS, nq),
        in_specs=[
            pl.BlockSpec((1, ATT_HEADS, tq, HEAD_PAD), lambda b, hp, i: (b, hp, q0 + i, 0)),
            pl.BlockSpec((1, ATT_HEADS, kv_rows, HEAD_PAD), lambda b, hp, i: (b, hp, kv_blk, 0)),
            pl.BlockSpec((1, ATT_HEADS, kv_rows, HEAD_PAD), lambda b, hp, i: (b, hp, kv_blk, 0)),
        ],
        out_specs=pl.BlockSpec((1, tq, ATT_HEADS * V_DIM), lambda b, hp, i: (b, i, hp)),
        compiler_params=_cp("parallel", "parallel", "parallel"),
        name="attn_lat" if latent else "attn_ctx",
    )(q, k, v)


def _filter_kernel(z_ref, w1_ref, b1_ref, fr_ref, w2_ref, b2_ref, w3f_ref, w3b_ref, dl_ref, hfb_ref, a_ref):
    z = z_ref[...]

    @pl.when((pl.program_id(0) == 0) & (pl.program_id(1) == 0))
    def _():
        fr = fr_ref[0]
        a1 = jnp.sin(fr * (_dot3(z, w1_ref[0]) + b1_ref[0]))
        a_ref[...] = jnp.sin(fr * (_dot3(a1, w2_ref[0]) + b2_ref[0]))

    a = a_ref[...]
    decay = jnp.exp(-z[:, 0:1] * jnp.abs(dl_ref[...]))
    rows = lax.broadcasted_iota(jnp.int32, (z.shape[0], 1), 0)

    def one_direction(w3_ref):
        h = _dot3(a, w3_ref[0]) * decay
        return h * lax.rsqrt(jnp.sum(h * h, axis=0, keepdims=True) + EPS)

    hfb_ref[0] = one_direction(w3f_ref).astype(BF16)
    hfb_ref[1] = jnp.where(rows == 0, 0.0, one_direction(w3b_ref)).astype(BF16)


def _filters(l, L, zemb, deltas, flt):
    w1, b1, fr, w2, b2, w3 = flt
    ncb = HY_WIDTH // LANES
    full = lambda *shape: pl.BlockSpec((1,) + shape, lambda o, j: (l,) + (0,) * len(shape))
    return pl.pallas_call(
        _filter_kernel,
        out_shape=jax.ShapeDtypeStruct((2, L, HY_ORDER * HY_WIDTH), BF16),
        grid=(HY_ORDER, ncb),
        in_specs=[
            pl.BlockSpec((L, LANES), lambda o, j: (0, 0)),
            full(LANES, HY_FFN), full(1, HY_FFN), full(1, HY_FFN),
            full(HY_FFN, HY_FFN), full(1, HY_FFN),
            pl.BlockSpec((1, HY_FFN, LANES), lambda o, j: (l, 0, o * 2 * ncb + j)),
            pl.BlockSpec((1, HY_FFN, LANES), lambda o, j: (l, 0, o * 2 * ncb + ncb + j)),
            pl.BlockSpec((1, LANES), lambda o, j: (0, j)),
        ],
        out_specs=pl.BlockSpec((2, L, LANES), lambda o, j: (0, 0, o * ncb + j)),
        scratch_shapes=[pltpu.VMEM((L, HY_FFN), F32)],
        compiler_params=_cp("arbitrary", "arbitrary"),
        name="hy_filter",
    )(zemb, w1, b1, fr, w2, b2, w3, w3, deltas)


def _packed_row0(block):
    return (lax.broadcasted_iota(jnp.int32, (FB, 1), 0) == 0) & (block == 0)


def _packed_mac(acc, c, z):
    (ar, ai), (cr, ci), (zr, zi) = acc, c, z
    return ar + (zr * cr - zi * ci), ai + (zr * ci + zi * cr)


def _spec_kernel(f_ref, h_ref, o_ref, *, seg, n_seg):
    fblk = pl.program_id(1)
    fr, fi = f_ref[0, 0], f_ref[0, 1]
    row0 = _packed_row0(fblk)
    sign = jnp.where((lax.broadcasted_iota(jnp.int32, (FB, 1), 0) & 1) == 0, 1.0, -1.0)

    def first_half(d, m):
        taps = h_ref[d, m * seg:(m + 1) * seg, :]
        return _dot(fr, taps), _dot(fi, taps), taps[0:1, :].astype(F32)

    def causal_block(d, m, halves):
        re, im, _ = halves[m]
        if m >= 1:
            pre, pim, tap0 = halves[m - 1]
            re = re + sign * (pre - tap0)
            im = im + sign * (pim - jnp.where(row0, tap0, 0.0))
        return re, im

    def conj(re, im):
        return re, jnp.where(row0, im, -im)

    halves = [[first_half(d, m) for m in range(n_seg)] for d in range(2)]
    for li in range(2 * n_seg - 1):
        lag = li - (n_seg - 1)
        if lag > 0:
            re, im = causal_block(0, lag, halves[0])
        elif lag < 0:
            re, im = conj(*causal_block(1, -lag, halves[1]))
        else:
            fre, fim = causal_block(0, 0, halves[0])
            gre, gim = conj(*causal_block(1, 0, halves[1]))
            re, im = fre + gre, fim + gim
        o_ref[0, 0, li, 0] = re
        o_ref[0, 0, li, 1] = im


def _spectrum(L, seg, fmat, hfb):
    n_seg, nfb = L // seg, seg // FB
    n_lag = 2 * n_seg - 1
    return pl.pallas_call(
        functools.partial(_spec_kernel, seg=seg, n_seg=n_seg),
        out_shape=jax.ShapeDtypeStruct((HY_ORDER, nfb, n_lag, 2, FB, HY_WIDTH), F32),
        grid=(HY_ORDER, nfb),
        in_specs=[
            pl.BlockSpec((1, 2, FB, seg), lambda o, f: (f, 0, 0, 0)),
            pl.BlockSpec((2, L, HY_WIDTH), lambda o, f: (0, 0, o)),
        ],
        out_specs=pl.BlockSpec((1, 1, n_lag, 2, FB, HY_WIDTH), lambda o, f: (o, f, 0, 0, 0, 0)),
        compiler_params=_cp("parallel", "parallel"),
        name="hy_spectrum",
    )(fmat, hfb)


def _fwd_kernel(z_ref, f_ref, c_ref, y_ref, zb_ref, *, seg, n_seg):
    fblk = pl.program_id(1)

    @pl.when(fblk == 0)
    def _():
        zb_ref[...] = z_ref[0].astype(BF16)

    fr, fi = f_ref[0, 0], f_ref[0, 1]
    row0 = _packed_row0(fblk)
    acc = [None] * n_seg
    edge = [None] * n_seg
    for j in range(n_seg):
        zj = zb_ref[j * seg:(j + 1) * seg, :]
        z = (_dot(fr, zj), _dot(fi, zj))
        for i in range(n_seg):
            c = (c_ref[0, 0, i - j + n_seg - 1, 0], c_ref[0, 0, i - j + n_seg - 1, 1])
            e = (z[0][0:1] * c[0][0:1], z[1][0:1] * c[1][0:1])
            if acc[i] is None:
                acc[i] = (z[0] * c[0] - z[1] * c[1], z[0] * c[1] + z[1] * c[0])
                edge[i] = e
            else:
                acc[i] = _packed_mac(acc[i], c, z)
                edge[i] = (edge[i][0] + e[0], edge[i][1] + e[1])
    for i in range(n_seg):
        y_ref[0, i, 0:FB] = jnp.where(row0, edge[i][0], acc[i][0]).astype(BF16)
        y_ref[0, i, FB:2 * FB] = jnp.where(row0, edge[i][1], acc[i][1]).astype(BF16)


def _hy_fwd(L, seg, z, z_rowblk, z_colblk, fmat, spec, order):
    n_seg, nfb = L // seg, seg // FB
    n_lag = 2 * n_seg - 1
    return pl.pallas_call(
        functools.partial(_fwd_kernel, seg=seg, n_seg=n_seg),
        out_shape=jax.ShapeDtypeStruct((BATCH, n_seg, 2 * seg, HY_WIDTH), BF16),
        grid=(BATCH, nfb),
        in_specs=[
            pl.BlockSpec((1, L, HY_WIDTH), lambda b, f: (b, z_rowblk, z_colblk)),
            pl.BlockSpec((1, 2, FB, seg), lambda b, f: (f, 0, 0, 0)),
            pl.BlockSpec((1, 1, n_lag, 2, FB, HY_WIDTH), lambda b, f: (order, f, 0, 0, 0, 0)),
        ],
        out_specs=pl.BlockSpec((1, n_seg, 2 * FB, HY_WIDTH), lambda b, f: (b, 0, f, 0)),
        scratch_shapes=[pltpu.VMEM((L, HY_WIDTH), BF16)],
        compiler_params=_cp("parallel", "arbitrary"),
        name="hy_fwd",
    )(z, fmat, spec)


def _inv_kernel(y_ref, g_ref, gate_ref, z_ref, bias_ref, o_ref):
    y = _dot(g_ref[...], y_ref[0, 0])
    o_ref[0] = gate_ref[0] * (y + bias_ref[0, 0] * z_ref[0])


def _hy_inv(l, L, seg, yf, gmat, u, row0, gate_colblk, z, z_row0, z_colblk, flt_bias, order):
    tmi = min(seg, 2 * TM)
    per_seg = seg // tmi
    r0, zr0 = row0 // tmi, z_row0 // tmi
    return pl.pallas_call(
        _inv_kernel,
        out_shape=jax.ShapeDtypeStruct((BATCH, L, HY_WIDTH), F32),
        grid=(BATCH, L // tmi),
        in_specs=[
            pl.BlockSpec((1, 1, 2 * seg, HY_WIDTH), lambda b, i: (b, i // per_seg, 0, 0)),
            pl.BlockSpec((tmi, 2 * seg), lambda b, i: (i % per_seg, 0)),
            pl.BlockSpec((1, tmi, HY_WIDTH), lambda b, i: (b, r0 + i, gate_colblk)),
            pl.BlockSpec((1, tmi, HY_WIDTH), lambda b, i: (b, zr0 + i, z_colblk)),
            pl.BlockSpec((1, 1, 1, HY_WIDTH), lambda b, i: (l, order, 0, 0)),
        ],
        out_specs=pl.BlockSpec((1, tmi, HY_WIDTH), lambda b, i: (b, i, 0)),
        compiler_params=_cp("parallel", "parallel"),
        name="hy_inv",
    )(yf, gmat, u, z, flt_bias)


def _route_tile(lg, bias):
    lane = lax.broadcasted_iota(jnp.int32, lg.shape, 1)
    lane_f = lane.astype(F32)
    neg = jnp.float32(-jnp.inf)
    big = jnp.float32(ROUTER_PAD)
    biased = lg + bias

    def first_argmax(v):
        m = jnp.max(v, axis=-1, keepdims=True)
        return jnp.min(jnp.where(v == m, lane_f, big), axis=-1, keepdims=True).astype(jnp.int32)

    def pick(v, idx):
        return jnp.sum(jnp.where(lane == idx, v, 0.0), axis=-1, keepdims=True)

    is_group = lane < N_GROUPS
    g_sel = first_argmax(jnp.where(is_group, biased, neg))
    raw_g = jnp.where(is_group, lg, neg)
    e_g = jnp.exp(raw_g - jnp.max(raw_g, axis=-1, keepdims=True))
    p_g = pick(e_g, g_sel) / jnp.sum(e_g, axis=-1, keepdims=True)

    lo = N_GROUPS + g_sel * EXP_PER_GROUP
    cand = jnp.where((lane >= lo) & (lane < lo + EXP_PER_GROUP), biased, neg)
    i1 = first_argmax(cand)
    i2 = first_argmax(jnp.where(lane == i1, neg, cand))
    l1, l2 = pick(lg, i1), pick(lg, i2)
    top = jnp.maximum(l1, l2)
    e1, e2 = jnp.exp(l1 - top), jnp.exp(l2 - top)
    inv = p_g / (e1 + e2)
    return (i1 - N_GROUPS, i2 - N_GROUPS), (e1 * inv, e2 * inv)


def _merge_kernel(h_ref, sg_ref, hyl_ref, hyc_ref, attl_ref, attc_ref, mod_ref, g_ref,
                  who_ref, wmo_ref, wout_ref, wr_ref, br_ref, h1_ref, u2_ref, eid_ref, ew_ref, cnt_ref):
    is_ctx = pl.program_id(1) >= N_LAT_TILES
    hy = jnp.where(is_ctx, hyc_ref[0], hyl_ref[0]).astype(BF16)
    att = jnp.where(is_ctx, attc_ref[0], attl_ref[0])
    sg = sg_ref[0].astype(F32)
    m = sg[:, :D_MODEL] * _dot(hy, who_ref[0]) + sg[:, D_MODEL:] * _dot(att, wmo_ref[0])
    y = _dot(m.astype(BF16), wout_ref[0])
    g1 = mod_ref[0, 0, 0, 2:3, :]
    sh2 = mod_ref[0, 0, 0, 3:4, :]
    sc2 = mod_ref[0, 0, 0, 4:5, :]
    h1 = h_ref[0] + g1 * y
    h1_ref[0] = h1
    u2 = _rms(h1, g_ref[0]) * (1.0 + sc2) + sh2
    u2_ref[0] = u2
    ids, ws = _route_tile(_dot3(u2, wr_ref[0]), br_ref[0])
    lane = lax.broadcasted_iota(jnp.int32, (TM, ROUTER_PAD), 1)
    hist = jnp.zeros((1, ROUTER_PAD), F32)
    for c in range(TOP_K):
        eid_ref[0, :, c:c + 1] = ids[c]
        ew_ref[0, :, c:c + 1] = ws[c]
        hist = hist + jnp.sum(jnp.where(lane == ids[c], 1.0, 0.0), axis=0, keepdims=True)
    cnt_ref[0, 0] = hist


def _merge(l, n_tiles, h, sg, hy_lat, hy_ctx, att_lat, att_ctx, mods, norm2_g, who, wmo, wout, wr, br):
    rows = n_tiles * TM
    tile = lambda w: pl.BlockSpec((1, TM, w), lambda b, i: (b, i, 0))
    lat = lambda w: pl.BlockSpec((1, TM, w), lambda b, i: (b, jnp.minimum(i, N_LAT_TILES - 1), 0))
    ctx = lambda w: pl.BlockSpec((1, TM, w), lambda b, i: (b, 0, 0))
    wfull = lambda a: pl.BlockSpec((1,) + a.shape[1:], lambda b, i: (l, 0, 0))
    return pl.pallas_call(
        _merge_kernel,
        out_shape=(
            jax.ShapeDtypeStruct((BATCH, rows, D_MODEL), F32),
            jax.ShapeDtypeStruct((BATCH, rows, D_MODEL), F32),
            jax.ShapeDtypeStruct((BATCH, rows, TOP_K), jnp.int32),
            jax.ShapeDtypeStruct((BATCH, rows, TOP_K), F32),
            jax.ShapeDtypeStruct((BATCH, n_tiles, 1, ROUTER_PAD), F32),
        ),
        grid=(BATCH, n_tiles),
        in_specs=[
            tile(D_MODEL), tile(2 * D_MODEL),
            lat(HY_WIDTH), ctx(HY_WIDTH), lat(MLA_HEADS * V_DIM), ctx(MLA_HEADS * V_DIM),
            pl.BlockSpec((1, 1, 1, N_MOD, D_MODEL), lambda b, i: (l, b, i // N_LAT_TILES, 0, 0)),
            pl.BlockSpec((1, 1, D_MODEL), lambda b, i: (l, 0, 0)),
            wfull(who), wfull(wmo), wfull(wout), wfull(wr), wfull(br),
        ],
        out_specs=(tile(D_MODEL), tile(D_MODEL), tile(TOP_K), tile(TOP_K),
                   pl.BlockSpec((1, 1, 1, ROUTER_PAD), lambda b, i: (b, i, 0, 0))),
        compiler_params=_cp("parallel", "parallel"),
        name="merge",
    )(h, sg, hy_lat, hy_ctx, att_lat, att_ctx, mods, norm2_g, who, wmo, wout, wr, br)


def _expert_kernel(be_ref, b0_ref, nr_ref, src_ref, dst_ref, u_hbm, wg_ref, wu_ref, wd_ref, y_hbm,
                   xbuf, ybuf, gsem, ssem):
    i = pl.program_id(0)
    nb = pl.num_programs(0)
    slot = i % 2

    def in_copy(blk, sl, g, j):
        tok = src_ref[b0_ref[blk] + g * SUBLANES + j]
        return pltpu.make_async_copy(u_hbm.at[pl.ds(tok, 1)], xbuf.at[sl, g, pl.ds(j, 1)], gsem.at[sl])

    def out_copy(blk, sl, g, j):
        r = g * SUBLANES + j
        return pltpu.make_async_copy(ybuf.at[sl, pl.ds(r, 1)], y_hbm.at[pl.ds(dst_ref[b0_ref[blk] + r], 1)],
                                     ssem.at[sl])

    def for_rows(blk, fn):
        n = nr_ref[blk]
        n_groups = n // SUBLANES

        def group(g, carry):
            for j in range(SUBLANES):
                fn(g, j)
            return carry

        def single(r, carry):
            fn(n_groups, r)
            return carry

        lax.fori_loop(0, n_groups, group, 0)
        lax.fori_loop(0, n - n_groups * SUBLANES, single, 0)

    @pl.when(i == 0)
    def _():
        xbuf[...] = jnp.zeros_like(xbuf)
        for_rows(0, lambda g, j: in_copy(0, 0, g, j).start())

    @pl.when(i + 1 < nb)
    def _():
        for_rows(i + 1, lambda g, j: in_copy(i + 1, 1 - slot, g, j).start())

    for_rows(i, lambda g, j: in_copy(i, slot, g, j).wait())

    @pl.when(i >= 2)
    def _():
        for_rows(i - 2, lambda g, j: out_copy(i - 2, slot, g, j).wait())

    @pl.when(nr_ref[i] > 0)
    def _():
        x = xbuf[slot].reshape(MOE_BM, D_MODEL).astype(BF16)
        hg = _dot(x, wg_ref[0, 0].astype(BF16))
        hu = _dot(x, wu_ref[0, 0].astype(BF16))
        hb = (hg * jax.nn.sigmoid(hg) * hu).astype(BF16)
        ybuf[slot] = _dot(hb, wd_ref[0, 0].astype(BF16))

    for_rows(i, lambda g, j: out_copy(i, slot, g, j).start())

    @pl.when(i == nb - 1)
    def _():
        for_rows(i - 1, lambda g, j: out_copy(i - 1, 1 - slot, g, j).wait())
        for_rows(i, lambda g, j: out_copy(i, slot, g, j).wait())


def _experts(l, block_expert, block_row0, block_rows, order, u, w_gate_e, w_up_e, w_down_e):
    n_blocks = block_expert.shape[0]
    n_tok = u.shape[0]
    src = lax.shift_right_logical(order, 1)
    dst = (order & 1) * n_tok + src
    wspec = lambda r, c: pl.BlockSpec((1, 1, r, c), lambda i, be, b0, nr, sr, ds: (l, be[i], 0, 0))
    grid_spec = pltpu.PrefetchScalarGridSpec(
        num_scalar_prefetch=5,
        grid=(n_blocks,),
        in_specs=[
            pl.BlockSpec(memory_space=pl.ANY),
            wspec(D_MODEL, D_EXPERT), wspec(D_MODEL, D_EXPERT), wspec(D_EXPERT, D_MODEL),
        ],
        out_specs=pl.BlockSpec(memory_space=pl.ANY),
        scratch_shapes=[
            pltpu.VMEM((2, MOE_BM // SUBLANES, SUBLANES, D_MODEL), F32),
            pltpu.VMEM((2, MOE_BM, D_MODEL), F32),
            pltpu.SemaphoreType.DMA((2,)),
            pltpu.SemaphoreType.DMA((2,)),
        ],
    )
    return pl.pallas_call(
        _expert_kernel,
        out_shape=jax.ShapeDtypeStruct((TOP_K * u.shape[0], D_MODEL), F32),
        grid_spec=grid_spec,
        compiler_params=_cp("arbitrary"),
        name="experts",
    )(block_expert, block_row0, block_rows, src, dst, u, w_gate_e, w_up_e, w_down_e)


def _dispatch(expert, counts):
    A = expert.shape[0]
    _, order = lax.sort((expert, jnp.arange(A, dtype=jnp.int32)), num_keys=1)
    start = jnp.cumsum(counts) - counts
    n_blk = (counts + MOE_BM - 1) // MOE_BM
    blk_end = jnp.cumsum(n_blk)
    blk_start = blk_end - n_blk
    n_blocks = -(-A // MOE_BM) + N_EXPERTS
    b = jnp.arange(n_blocks, dtype=jnp.int32)[:, None]
    mine = (b >= blk_start[None, :]) & (b < blk_end[None, :])
    used = jnp.any(mine, axis=1)
    pick = lambda table: jnp.sum(jnp.where(mine, table[None, :], 0), axis=1)
    row_in_expert = (b[:, 0] - pick(blk_start)) * MOE_BM
    block_expert = jnp.where(used, pick(jnp.arange(N_EXPERTS, dtype=jnp.int32)), N_EXPERTS - 1)
    block_rows = jnp.where(used, jnp.clip(pick(counts) - row_in_expert, 0, MOE_BM), 0)
    block_row0 = jnp.where(used, pick(start) + row_in_expert, 0)
    return order, block_expert.astype(jnp.int32), block_row0.astype(jnp.int32), block_rows.astype(jnp.int32)


def _moe_sum(y_ref, w_ref):
    w = w_ref[0]
    return sum(w[:, c:c + 1] * y_ref[c, 0] for c in range(TOP_K))


def _combine_kernel(h_ref, y_ref, w_ref, mod_ref, o_ref):
    o_ref[0] = h_ref[0] + mod_ref[0, 0, 0, 5:6, :] * _moe_sum(y_ref, w_ref)


def _final_kernel(h_ref, y_ref, w_ref, mod_ref, g_ref, o_ref):
    h2 = h_ref[0] + mod_ref[0, 0, 0, 5:6, :] * _moe_sum(y_ref, w_ref)
    o_ref[0] = _rms(h2, g_ref[...])


def _combine(l, n_tiles, h1, y, w, mods, final_g=None):
    rows = n_tiles * TM
    tile = pl.BlockSpec((1, TM, D_MODEL), lambda b, i: (b, i, 0))
    in_specs = [tile, pl.BlockSpec((TOP_K, 1, TM, D_MODEL), lambda b, i: (0, b, i, 0)),
                pl.BlockSpec((1, TM, TOP_K), lambda b, i: (b, i, 0)),
                pl.BlockSpec((1, 1, 1, N_MOD, D_MODEL), lambda b, i: (l, b, i // N_LAT_TILES, 0, 0))]
    args = [h1, y, w, mods]
    kern = _combine_kernel
    if final_g is not None:
        in_specs.append(pl.BlockSpec((1, D_MODEL), lambda b, i: (0, 0)))
        args.append(final_g.reshape(1, D_MODEL))
        kern = _final_kernel
    return pl.pallas_call(
        kern,
        out_shape=jax.ShapeDtypeStruct((BATCH, rows, D_MODEL), F32),
        grid=(BATCH, n_tiles),
        in_specs=in_specs,
        out_specs=tile,
        compiler_params=_cp("parallel", "parallel"),
        name="combine",
    )(*args)


def _rope_tables():
    n = jnp.arange(SEQ)
    pos = jnp.stack([n // GRID_W, n % GRID_W], axis=-1).astype(F32)
    inv = ROPE_BASE ** (-jnp.arange(ROPE_PAIRS, dtype=F32) / ROPE_PAIRS)
    ang = pos[:, :, None] * inv
    cos, sin = jnp.cos(ang), jnp.sin(ang)
    cos32 = jnp.stack([cos, cos], axis=2).reshape(SEQ, QK_ROPE)
    sin32 = jnp.stack([-sin, sin], axis=2).reshape(SEQ, QK_ROPE)
    one = jnp.ones((SEQ, QK_ROPE), F32)
    zero = jnp.zeros((SEQ, QK_ROPE), F32)
    cosq = jnp.concatenate([jnp.ones((SEQ, QK_NOPE), F32), cos32, one], axis=1) * Q_SCALE
    sinq = jnp.concatenate([jnp.zeros((SEQ, QK_NOPE), F32), sin32, zero], axis=1) * Q_SCALE
    cosk = jnp.concatenate([jnp.zeros((SEQ, QK_NOPE), F32), cos32, zero], axis=1)
    sink = jnp.concatenate([jnp.zeros((SEQ, QK_NOPE), F32), sin32, zero], axis=1)
    lat = jnp.stack([cosq, sinq, cosk, sink])
    ctx_row = jnp.ones((HEAD_PAD,), F32) * Q_SCALE
    ctx_cosk = jnp.concatenate([jnp.zeros((QK_NOPE + QK_ROPE,), F32), jnp.ones((QK_ROPE,), F32)])
    ctx = jnp.stack([ctx_row, jnp.zeros_like(ctx_row), ctx_cosk, jnp.zeros_like(ctx_row)])
    ctx = jnp.broadcast_to(ctx[:, None, :], (4, CTX_LEN, HEAD_PAD))
    return jnp.concatenate([lat, ctx], axis=1)


def _dft_matrices(L):
    N = 2 * L
    nfb = L // FB
    t = jnp.arange(L, dtype=jnp.int32)

    def tables(n_rows, step):
        ang = ((jnp.arange(n_rows, dtype=jnp.int32)[:, None] * step * t[None, :]) % N).astype(F32)
        ang = ang * (2.0 * math.pi / N)
        return jnp.cos(ang), jnp.sin(ang)

    ca, sa = tables(nfb, FB)
    cb, sb = tables(FB, 1)
    alt = jnp.where((t % 2) == 0, 1.0, -1.0).astype(F32)
    i_idx = jnp.arange(nfb)[:, None, None]
    r_idx = jnp.arange(FB)[None, :, None]
    dc = (i_idx == 0) & (r_idx == 0)
    c = ca[:, None, :] * cb[None] - sa[:, None, :] * sb[None]
    s = jnp.where(dc, alt, -(sa[:, None, :] * cb[None] + ca[:, None, :] * sb[None]))
    f = jnp.stack([c, s], axis=1).astype(BF16)
    wk = jnp.where(dc, 1.0 / N, 2.0 / N).astype(F32)[..., 0]
    cat, sat, cbt, sbt = ca.T[:, :, None], sa.T[:, :, None], cb.T[:, None, :], sb.T[:, None, :]
    ct = (cat * cbt - sat * sbt) * wk
    st = jnp.where(dc[..., 0], alt[:, None, None], -(sat * cbt + cat * sbt)) * wk
    g = jnp.stack([ct, st], axis=2).reshape(L, 2 * L).astype(BF16)
    return f, g


def _filter_embedding(L):
    t = jnp.linspace(0.0, 1.0, L, dtype=F32)[:, None]
    w = 2.0 * math.pi * jnp.arange(L, dtype=F32)[:, None] / L
    bands = jnp.linspace(1e-4, HY_BANDS - 1, HY_BANDS, dtype=F32)[None, :]
    return jnp.concatenate([t, jnp.cos(bands * w), -jnp.sin(bands * w),
                            jnp.zeros((L, LANES - HY_EMB), F32)], axis=-1)


def _rope_swap_perm():
    idx = np.arange(QK_ROPE)
    axis, half, pair = idx // (2 * ROPE_PAIRS), (idx // ROPE_PAIRS) % 2, idx % ROPE_PAIRS
    return axis * 2 * ROPE_PAIRS + (1 - half) * ROPE_PAIRS + pair


def _prep_weights(w_in, w_uq, w_ukv, w_hy_o, w_mla_o, w_out, w_router_g, w_router_e):
    perm = _rope_swap_perm()
    c_q = HY_COLS
    c_kv = HY_COLS + Q_LORA
    c_kr = c_kv + KV_LORA
    w_kr = w_in[:, :, c_kr:MLA_END]
    zeros = lambda n: jnp.zeros((DEPTH, D_MODEL, n), F32)
    wkr = jnp.concatenate([zeros(QK_NOPE), w_kr, w_kr,
                           zeros(QK_NOPE), w_kr[:, :, perm], zeros(QK_ROPE)], axis=-1)
    uq = w_uq.reshape(DEPTH, Q_LORA, MLA_HEADS, QK_NOPE + QK_ROPE)
    uq_r = uq[..., QK_NOPE:]
    wuq = jnp.concatenate([uq, uq_r], axis=-1).reshape(DEPTH, Q_LORA, MLA_HEADS * HEAD_PAD)
    wuqs = jnp.concatenate([jnp.zeros_like(uq[..., :QK_NOPE]), uq_r[..., perm], jnp.zeros_like(uq_r)],
                           axis=-1).reshape(DEPTH, Q_LORA, MLA_HEADS * HEAD_PAD)
    ukv = w_ukv.reshape(DEPTH, KV_LORA, MLA_HEADS, QK_NOPE + V_DIM)
    zpad = jnp.zeros_like(ukv[..., :HEAD_PAD - QK_NOPE])
    wuk = jnp.concatenate([ukv[..., :QK_NOPE], zpad], axis=-1).reshape(DEPTH, KV_LORA, MLA_HEADS * HEAD_PAD)
    wuv = jnp.concatenate([ukv[..., QK_NOPE:], zpad], axis=-1).reshape(DEPTH, KV_LORA, MLA_HEADS * HEAD_PAD)
    wr = jnp.concatenate([w_router_g, w_router_e,
                          jnp.zeros((DEPTH, D_MODEL, ROUTER_PAD - N_GROUPS - N_EXPERTS), F32)], axis=-1)
    bf = lambda a: a.astype(BF16)
    return dict(
        why=bf(w_in[:, :, :HY_COLS]), wgt=bf(w_in[:, :, MLA_END:]),
        wq=bf(w_in[:, :, c_q:c_kv]), wkv=bf(w_in[:, :, c_kv:c_kr]), wkr=bf(wkr),
        wuq=bf(wuq), wuqs=bf(wuqs), wuk=bf(wuk), wuv=bf(wuv),
        who=bf(w_hy_o), wmo=bf(w_mla_o), wout=bf(w_out), wr=wr,
    )


def _hyena(l, L, row0, u, fmat, gmat, zemb, deltas, flt, flt_bias):
    seg = fmat.shape[-1]
    spec = _spectrum(L, seg, fmat, _filters(l, L, zemb, deltas, flt))
    rb = row0 // L
    yf = _hy_fwd(L, seg, u, rb, 0, fmat, spec, 0)
    z1 = _hy_inv(l, L, seg, yf, gmat, u, row0, 1, u, row0, 0, flt_bias, 0)
    yf = _hy_fwd(L, seg, z1, 0, 0, fmat, spec, 1)
    return _hy_inv(l, L, seg, yf, gmat, u, row0, 2, z1, 0, 0, flt_bias, 1)


def kernel(x, c, ctx, c_ctx, w_ada, b_ada, norm1_g, w_in, q_norm_g, kv_norm_g, w_uq, w_ukv,
           sc_w, sc_b, flt_w1, flt_b1, flt_freq, flt_w2, flt_b2, flt_w3, flt_bias,
           w_hy_o, w_mla_o, w_out, norm2_g, w_router_g, b_router_g, w_router_e, b_router_e,
           w_gate_e, w_up_e, w_down_e, final_g):
    wts = _prep_weights(w_in, w_uq, w_ukv, w_hy_o, w_mla_o, w_out, w_router_g, w_router_e)
    tabs = _rope_tables()
    f_lat, g_lat = _dft_matrices(min(SEQ, HY_SEG))
    f_ctx, g_ctx = _dft_matrices(min(CTX_LEN, HY_SEG))
    z_lat, z_ctx = _filter_embedding(SEQ), _filter_embedding(CTX_LEN)
    deltas = jnp.linspace(math.log(HY_TARGET) / HY_FAST_DECAY, math.log(HY_TARGET) / HY_SLOW_DECAY,
                          HY_WIDTH, dtype=F32)[None, :]

    cc = jnp.concatenate([c, c_ctx[None, :], jnp.zeros((3, D_MODEL), F32)], axis=0)
    mod = _ada(cc, w_ada, b_ada).reshape(DEPTH, 8, N_MOD, D_MODEL)
    mods = jnp.stack([mod[:, :BATCH], jnp.broadcast_to(mod[:, BATCH:BATCH + 1], (DEPTH, BATCH, N_MOD, D_MODEL))],
                     axis=2)

    r3 = lambda a: a.reshape(DEPTH, 1, a.shape[-1])
    norm1, norm2, qg, kvg = r3(norm1_g), r3(norm2_g), r3(q_norm_g), r3(kv_norm_g)
    scb = r3(sc_b)
    fb1, ffr, fb2 = r3(flt_b1), r3(flt_freq), r3(flt_b2)
    fbias = flt_bias.reshape(DEPTH, HY_ORDER, 1, HY_WIDTH)
    w1_pad = jnp.concatenate([flt_w1, jnp.zeros((DEPTH, LANES - HY_EMB, HY_FFN), F32)], axis=1)
    flt = (w1_pad, fb1, ffr, flt_w2, fb2, flt_w3)

    b_router = jnp.concatenate([b_router_g, b_router_e,
                                jnp.zeros((DEPTH, ROUTER_PAD - N_GROUPS - N_EXPERTS), F32)], axis=-1)
    b_router = b_router.reshape(DEPTH, 1, ROUTER_PAD)

    h = jnp.concatenate([x, ctx], axis=1)
    for l in range(DEPTH):
        last = l == DEPTH - 1
        n_tiles = N_LAT_TILES if last else N_ALL_TILES
        u, sg, q, k, v = _inproj(l, h, mods, norm1, tabs, qg, kvg, sc_w, scb, wts)
        att_lat = _attention(q, k, v, latent=True)
        hy_lat = _hyena(l, SEQ, 0, u, f_lat, g_lat, z_lat, deltas, flt, fbias)
        if last:
            att_ctx, hy_ctx = att_lat, hy_lat
        else:
            att_ctx = _attention(q, k, v, latent=False)
            hy_ctx = _hyena(l, CTX_LEN, SEQ, u, f_ctx, g_ctx, z_ctx, deltas, flt, fbias)

        h1, u2, eid, ew, hist = _merge(l, n_tiles, h, sg, hy_lat, hy_ctx, att_lat, att_ctx, mods, norm2,
                                       wts["who"], wts["wmo"], wts["wout"], wts["wr"], b_router)
        counts = jnp.sum(hist, axis=(0, 1, 2))[:N_EXPERTS].astype(jnp.int32)

        T = BATCH * n_tiles * TM
        order, block_expert, block_row0, block_rows = _dispatch(eid.reshape(T * TOP_K), counts)
        y = _experts(l, block_expert, block_row0, block_rows, order, u2.reshape(T, D_MODEL),
                     w_gate_e, w_up_e, w_down_e)
        h = _combine(l, n_tiles, h1, y.reshape(TOP_K, BATCH, n_tiles * TM, D_MODEL), ew, mods,
                     final_g if last else None)
    return h
```

```python
import functools
import math

import jax
import jax.numpy as jnp
import numpy as np
from jax import lax
from jax.experimental import pallas as pl
from jax.experimental.pallas import tpu as pltpu

F32 = jnp.float32
BF16 = jnp.bfloat16

D_MODEL = 1024
BATCH = 4
SEQ = 4096
DEPTH = 4
GRID_W = 64
CTX_LEN = 256
S_ALL = SEQ + CTX_LEN
EPS = 1e-6
N_MOD = 6

HY_WIDTH = 512
HY_ORDER = 2
HY_BANDS = 16
HY_EMB = 1 + 2 * HY_BANDS
HY_FFN = 64
SHORT_K = 3
HY_FAST_DECAY = 0.3
HY_SLOW_DECAY = 1.5
HY_TARGET = 1e-2

MLA_HEADS = 8
QK_NOPE = 64
QK_ROPE = 32
V_DIM = 64
Q_LORA = 384
KV_LORA = 256
ROPE_PAIRS = QK_ROPE // 4
ROPE_BASE = 10000.0
ATTN_SCALE = (QK_NOPE + QK_ROPE) ** -0.5
Q_SCALE = ATTN_SCALE * math.log2(math.e)

N_GROUPS = 8
EXP_PER_GROUP = 8
N_EXPERTS = N_GROUPS * EXP_PER_GROUP
TOP_K = 2
D_EXPERT = 256

HY_COLS = (HY_ORDER + 1) * HY_WIDTH
MLA_END = HY_COLS + Q_LORA + KV_LORA + QK_ROPE

LANES = 128
HEAD_PAD = LANES
V7X_VMEM_BYTES = 64 << 20
VMEM_LIMIT = V7X_VMEM_BYTES * 7 // 8

TM = 256
N_LAT_TILES = SEQ // TM
N_ALL_TILES = S_ALL // TM
TQ = 1024
TK = 2048
ATT_HEADS = 4
FB = 256
HY_SEG = 1024
MOE_BM = 256
SUBLANES = 8
ROUTER_PAD = LANES


def _cp(*sem):
    return pltpu.CompilerParams(dimension_semantics=sem, vmem_limit_bytes=VMEM_LIMIT)


def _dot(a, b):
    return jnp.dot(a, b, preferred_element_type=F32)


def _split(a):
    hi = a.astype(BF16)
    lo = (a - hi.astype(F32)).astype(BF16)
    return hi, lo


def _dot3(a, b):
    ah, al = _split(a)
    bh, bl = _split(b)
    return _dot(ah, bh) + (_dot(ah, bl) + _dot(al, bh))


def _rms(x, g):
    return x * lax.rsqrt(jnp.mean(x * x, axis=-1, keepdims=True) + EPS) * g


def _ada_kernel(c_ref, w_ref, b_ref, o_ref):
    cc = c_ref[...]
    s = cc * jax.nn.sigmoid(cc)
    o_ref[0] = _dot3(s, w_ref[0]) + b_ref[0]


def _ada(cc, w_ada, b_ada):
    tn = 1536
    n = N_MOD * D_MODEL
    return pl.pallas_call(
        _ada_kernel,
        out_shape=jax.ShapeDtypeStruct((DEPTH, 8, n), F32),
        grid=(DEPTH, n // tn),
        in_specs=[
            pl.BlockSpec((8, D_MODEL), lambda l, j: (0, 0)),
            pl.BlockSpec((1, D_MODEL, tn), lambda l, j: (l, 0, j)),
            pl.BlockSpec((1, 1, tn), lambda l, j: (l, 0, j)),
        ],
        out_specs=pl.BlockSpec((1, 8, tn), lambda l, j: (l, 0, j)),
        compiler_params=_cp("parallel", "parallel"),
        name="ada",
    )(cc, w_ada, b_ada.reshape(DEPTH, 1, n))


def _inproj_kernel(h_ref, hprev_ref, hnext_ref, mod_ref, g_ref, tab_ref, qg_ref, kvg_ref, scw_ref, scb_ref,
                   why_ref, wgt_ref, wq_ref, wkv_ref, wkr_ref, wuq_ref, wuqs_ref, wuk_ref, wuv_ref,
                   uhy_ref, sg_ref, q_ref, k_ref, v_ref):
    i = pl.program_id(1)
    shift = mod_ref[0, 0, 0, 0:1, :]
    scale = mod_ref[0, 0, 0, 1:2, :]

    def modulated(rows):
        return (_rms(rows, g_ref[0]) * (1.0 + scale) + shift).astype(BF16)

    u = modulated(h_ref[0])

    p = _dot(u, why_ref[0])
    p_halo = _dot(modulated(jnp.concatenate([hprev_ref[0], hnext_ref[0]], axis=0)), why_ref[0])
    starts_stream = (i == 0) | (i == N_LAT_TILES)
    ends_stream = (i == N_LAT_TILES - 1) | (i == N_ALL_TILES - 1)
    p_before = jnp.where(starts_stream, 0.0, p_halo[SUBLANES - 1:SUBLANES])
    p_after = jnp.where(ends_stream, 0.0, p_halo[SUBLANES:SUBLANES + 1])
    row = lax.broadcasted_iota(jnp.int32, (TM, 1), 0)
    prev = jnp.where(row == 0, p_before, pltpu.roll(p, 1, 0))
    nxt = jnp.where(row == TM - 1, p_after, pltpu.roll(p, TM - 1, 0))
    w = scw_ref[0]
    uhy_ref[0] = scb_ref[0] + prev * w[0:1] + p * w[1:2] + nxt * w[2:3]

    sg_ref[0] = jax.nn.sigmoid(_dot(u, wgt_ref[0])).astype(BF16)

    cosq, sinq, cosk, sink = tab_ref[0], tab_ref[1], tab_ref[2], tab_ref[3]
    cq = _rms(_dot(u, wq_ref[0]), qg_ref[0]).astype(BF16)
    qa = _dot(cq, wuq_ref[0])
    qs = _dot(cq, wuqs_ref[0])
    ckv = _rms(_dot(u, wkv_ref[0]), kvg_ref[0]).astype(BF16)
    ka = _dot(ckv, wuk_ref[0])
    va = _dot(ckv, wuv_ref[0])
    kr = _dot(u, wkr_ref[0])
    krk = kr[:, :HEAD_PAD] * cosk + kr[:, HEAD_PAD:] * sink
    ones_col = (lax.broadcasted_iota(jnp.int32, (1, HEAD_PAD), 1) == V_DIM).astype(F32)
    for hh in range(MLA_HEADS):
        sl = slice(hh * HEAD_PAD, (hh + 1) * HEAD_PAD)
        q_ref[0, hh] = (qa[:, sl] * cosq + qs[:, sl] * sinq).astype(BF16)
        k_ref[0, hh] = (ka[:, sl] + krk).astype(BF16)
        v_ref[0, hh] = (va[:, sl] + ones_col).astype(BF16)


def _inproj(l, h, mods, norm1_g, tabs, q_norm_g, kv_norm_g, sc_w, sc_b, wts):
    names = ("why", "wgt", "wq", "wkv", "wkr", "wuq", "wuqs", "wuk", "wuv")
    w_specs = [pl.BlockSpec((1,) + wts[n].shape[1:], lambda b, i: (l, 0, 0)) for n in names]
    qkv_shape = jax.ShapeDtypeStruct((BATCH, MLA_HEADS, S_ALL, HEAD_PAD), BF16)
    qkv_spec = pl.BlockSpec((1, MLA_HEADS, TM, HEAD_PAD), lambda b, i: (b, 0, i, 0))
    halo_blocks = TM // SUBLANES
    last_halo = S_ALL // SUBLANES - 1
    return pl.pallas_call(
        _inproj_kernel,
        out_shape=(
            jax.ShapeDtypeStruct((BATCH, S_ALL, HY_COLS), F32),
            jax.ShapeDtypeStruct((BATCH, S_ALL, 2 * D_MODEL), BF16),
            qkv_shape, qkv_shape, qkv_shape,
        ),
        grid=(BATCH, N_ALL_TILES),
        in_specs=[
            pl.BlockSpec((1, TM, D_MODEL), lambda b, i: (b, i, 0)),
            pl.BlockSpec((1, SUBLANES, D_MODEL), lambda b, i: (b, jnp.maximum(i * halo_blocks - 1, 0), 0)),
            pl.BlockSpec((1, SUBLANES, D_MODEL),
                         lambda b, i: (b, jnp.minimum((i + 1) * halo_blocks, last_halo), 0)),
            pl.BlockSpec((1, 1, 1, N_MOD, D_MODEL), lambda b, i: (l, b, i // N_LAT_TILES, 0, 0)),
            pl.BlockSpec((1, 1, D_MODEL), lambda b, i: (l, 0, 0)),
            pl.BlockSpec((4, TM, HEAD_PAD), lambda b, i: (0, i, 0)),
            pl.BlockSpec((1, 1, Q_LORA), lambda b, i: (l, 0, 0)),
            pl.BlockSpec((1, 1, KV_LORA), lambda b, i: (l, 0, 0)),
            pl.BlockSpec((1, SHORT_K, HY_COLS), lambda b, i: (l, 0, 0)),
            pl.BlockSpec((1, 1, HY_COLS), lambda b, i: (l, 0, 0)),
        ] + w_specs,
        out_specs=(
            pl.BlockSpec((1, TM, HY_COLS), lambda b, i: (b, i, 0)),
            pl.BlockSpec((1, TM, 2 * D_MODEL), lambda b, i: (b, i, 0)),
            qkv_spec, qkv_spec, qkv_spec,
        ),
        compiler_params=_cp("parallel", "parallel"),
        name="inproj",
    )(h, h, h, mods, norm1_g, tabs, q_norm_g, kv_norm_g, sc_w, sc_b, *[wts[n] for n in names])


def _flash_step(q_ref, k_ref, v_ref, rows, carry):
    out = []
    for hh in range(ATT_HEADS):
        m, acc = carry[hh]
        s = lax.dot_general(q_ref[0, hh], k_ref[0, hh, rows, :], (((1,), (1,)), ((), ())),
                            preferred_element_type=F32)
        m_new = jnp.maximum(m, jnp.max(s, axis=-1, keepdims=True))
        p = jnp.exp2(s - m_new)
        acc = acc * jnp.exp2(m - m_new) + _dot(p.astype(BF16), v_ref[0, hh, rows, :])
        out.append((m_new, acc))
    return tuple(out)


def _attn_kernel(q_ref, k_ref, v_ref, o_ref, *, chunks):
    tq = q_ref.shape[2]
    init = (jnp.full((tq, 1), -1e30, F32), jnp.zeros((tq, HEAD_PAD), F32))
    carry = (init,) * ATT_HEADS
    start = 0
    for size in chunks:
        carry = _flash_step(q_ref, k_ref, v_ref, pl.ds(start, size), carry)
        start += size
    outs = [acc[:, :V_DIM] / acc[:, V_DIM:V_DIM + 1] for _, acc in carry]
    o_ref[0] = jnp.concatenate(outs, axis=-1).astype(BF16)


def _attention(q, k, v, latent):
    if latent:
        tq, nq, q0, kv_rows, kv_blk = TQ, SEQ // TQ, 0, S_ALL, 0
        kern = functools.partial(_attn_kernel, chunks=(TK,) * (SEQ // TK - 1) + (TK + CTX_LEN,))
    else:
        tq, nq, q0, kv_rows, kv_blk = CTX_LEN, 1, SEQ // CTX_LEN, CTX_LEN, SEQ // CTX_LEN
        kern = functools.partial(_attn_kernel, chunks=(CTX_LEN,))
    return pl.pallas_call(
        kern,
        out_shape=jax.ShapeDtypeStruct((BATCH, nq * tq, MLA_HEADS * V_DIM), BF16),
        grid=(BATCH, MLA_HEADS // ATT_HEADS, nq),
        in_specs=[
            pl.BlockSpec((1, ATT_HEADS, tq, HEAD_PAD), lambda b, hp, i: (b, hp, q0 + i, 0)),
            pl.BlockSpec((1, ATT_HEADS, kv_rows, HEAD_PAD), lambda b, hp, i: (b, hp, kv_blk, 0)),
            pl.BlockSpec((1, ATT_HEADS, kv_rows, HEAD_PAD), lambda b, hp, i: (b, hp, kv_blk, 0)),
        ],
        out_specs=pl.BlockSpec((1, tq, ATT_HEADS * V_DIM), lambda b, hp, i: (b, i, hp)),
        compiler_params=_cp("parallel", "parallel", "parallel"),
        name="attn_lat" if latent else "attn_ctx",
    )(q, k, v)


def _filter_kernel(z_ref, w1_ref, b1_ref, fr_ref, w2_ref, b2_ref, w3f_ref, w3b_ref, dl_ref, hfb_ref, a_ref):
    z = z_ref[...]

    @pl.when((pl.program_id(0) == 0) & (pl.program_id(1) == 0))
    def _():
        fr = fr_ref[0]
        a1 = jnp.sin(fr * (_dot3(z, w1_ref[0]) + b1_ref[0]))
        a_ref[...] = jnp.sin(fr * (_dot3(a1, w2_ref[0]) + b2_ref[0]))

    a = a_ref[...]
    decay = jnp.exp(-z[:, 0:1] * jnp.abs(dl_ref[...]))
    rows = lax.broadcasted_iota(jnp.int32, (z.shape[0], 1), 0)

    def one_direction(w3_ref):
        h = _dot3(a, w3_ref[0]) * decay
        return h * lax.rsqrt(jnp.sum(h * h, axis=0, keepdims=True) + EPS)

    hfb_ref[0] = one_direction(w3f_ref).astype(BF16)
    hfb_ref[1] = jnp.where(rows == 0, 0.0, one_direction(w3b_ref)).astype(BF16)


def _filters(l, L, zemb, deltas, flt):
    w1, b1, fr, w2, b2, w3 = flt
    ncb = HY_WIDTH // LANES
    full = lambda *shape: pl.BlockSpec((1,) + shape, lambda o, j: (l,) + (0,) * len(shape))
    return pl.pallas_call(
        _filter_kernel,
        out_shape=jax.ShapeDtypeStruct((2, L, HY_ORDER * HY_WIDTH), BF16),
        grid=(HY_ORDER, ncb),
        in_specs=[
            pl.BlockSpec((L, LANES), lambda o, j: (0, 0)),
            full(LANES, HY_FFN), full(1, HY_FFN), full(1, HY_FFN),
            full(HY_FFN, HY_FFN), full(1, HY_FFN),
            pl.BlockSpec((1, HY_FFN, LANES), lambda o, j: (l, 0, o * 2 * ncb + j)),
            pl.BlockSpec((1, HY_FFN, LANES), lambda o, j: (l, 0, o * 2 * ncb + ncb + j)),
            pl.BlockSpec((1, LANES), lambda o, j: (0, j)),
        ],
        out_specs=pl.BlockSpec((2, L, LANES), lambda o, j: (0, 0, o * ncb + j)),
        scratch_shapes=[pltpu.VMEM((L, HY_FFN), F32)],
        compiler_params=_cp("arbitrary", "arbitrary"),
        name="hy_filter",
    )(zemb, w1, b1, fr, w2, b2, w3, w3, deltas)


def _packed_row0(block):
    return (lax.broadcasted_iota(jnp.int32, (FB, 1), 0) == 0) & (block == 0)


def _packed_mac(acc, c, z):
    (ar, ai), (cr, ci), (zr, zi) = acc, c, z
    return ar + (zr * cr - zi * ci), ai + (zr * ci + zi * cr)


def _spec_kernel(f_ref, h_ref, o_ref, *, seg, n_seg):
    fblk = pl.program_id(1)
    fr, fi = f_ref[0, 0], f_ref[0, 1]
    row0 = _packed_row0(fblk)
    sign = jnp.where((lax.broadcasted_iota(jnp.int32, (FB, 1), 0) & 1) == 0, 1.0, -1.0)

    def first_half(d, m):
        taps = h_ref[d, m * seg:(m + 1) * seg, :]
        return _dot(fr, taps), _dot(fi, taps), taps[0:1, :].astype(F32)

    def causal_block(d, m, halves):
        re, im, _ = halves[m]
        if m >= 1:
            pre, pim, tap0 = halves[m - 1]
            re = re + sign * (pre - tap0)
            im = im + sign * (pim - jnp.where(row0, tap0, 0.0))
        return re, im

    def conj(re, im):
        return re, jnp.where(row0, im, -im)

    halves = [[first_half(d, m) for m in range(n_seg)] for d in range(2)]
    for li in range(2 * n_seg - 1):
        lag = li - (n_seg - 1)
        if lag > 0:
            re, im = causal_block(0, lag, halves[0])
        elif lag < 0:
            re, im = conj(*causal_block(1, -lag, halves[1]))
        else:
            fre, fim = causal_block(0, 0, halves[0])
            gre, gim = conj(*causal_block(1, 0, halves[1]))
            re, im = fre + gre, fim + gim
        o_ref[0, 0, li, 0] = re
        o_ref[0, 0, li, 1] = im


def _spectrum(L, seg, fmat, hfb):
    n_seg, nfb = L // seg, seg // FB
    n_lag = 2 * n_seg - 1
    return pl.pallas_call(
        functools.partial(_spec_kernel, seg=seg, n_seg=n_seg),
        out_shape=jax.ShapeDtypeStruct((HY_ORDER, nfb, n_lag, 2, FB, HY_WIDTH), F32),
        grid=(HY_ORDER, nfb),
        in_specs=[
            pl.BlockSpec((1, 2, FB, seg), lambda o, f: (f, 0, 0, 0)),
            pl.BlockSpec((2, L, HY_WIDTH), lambda o, f: (0, 0, o)),
        ],
        out_specs=pl.BlockSpec((1, 1, n_lag, 2, FB, HY_WIDTH), lambda o, f: (o, f, 0, 0, 0, 0)),
        compiler_params=_cp("parallel", "parallel"),
        name="hy_spectrum",
    )(fmat, hfb)


def _fwd_kernel(z_ref, f_ref, c_ref, y_ref, zb_ref, *, seg, n_seg):
    fblk = pl.program_id(1)

    @pl.when(fblk == 0)
    def _():
        zb_ref[...] = z_ref[0].astype(BF16)

    fr, fi = f_ref[0, 0], f_ref[0, 1]
    row0 = _packed_row0(fblk)
    acc = [None] * n_seg
    edge = [None] * n_seg
    for j in range(n_seg):
        zj = zb_ref[j * seg:(j + 1) * seg, :]
        z = (_dot(fr, zj), _dot(fi, zj))
        for i in range(n_seg):
            c = (c_ref[0, 0, i - j + n_seg - 1, 0], c_ref[0, 0, i - j + n_seg - 1, 1])
            e = (z[0][0:1] * c[0][0:1], z[1][0:1] * c[1][0:1])
            if acc[i] is None:
                acc[i] = (z[0] * c[0] - z[1] * c[1], z[0] * c[1] + z[1] * c[0])
                edge[i] = e
            else:
                acc[i] = _packed_mac(acc[i], c, z)
                edge[i] = (edge[i][0] + e[0], edge[i][1] + e[1])
    for i in range(n_seg):
        y_ref[0, i, 0:FB] = jnp.where(row0, edge[i][0], acc[i][0]).astype(BF16)
        y_ref[0, i, FB:2 * FB] = jnp.where(row0, edge[i][1], acc[i][1]).astype(BF16)


def _hy_fwd(L, seg, z, z_rowblk, z_colblk, fmat, spec, order):
    n_seg, nfb = L // seg, seg // FB
    n_lag = 2 * n_seg - 1
    return pl.pallas_call(
        functools.partial(_fwd_kernel, seg=seg, n_seg=n_seg),
        out_shape=jax.ShapeDtypeStruct((BATCH, n_seg, 2 * seg, HY_WIDTH), BF16),
        grid=(BATCH, nfb),
        in_specs=[
            pl.BlockSpec((1, L, HY_WIDTH), lambda b, f: (b, z_rowblk, z_colblk)),
            pl.BlockSpec((1, 2, FB, seg), lambda b, f: (f, 0, 0, 0)),
            pl.BlockSpec((1, 1, n_lag, 2, FB, HY_WIDTH), lambda b, f: (order, f, 0, 0, 0, 0)),
        ],
        out_specs=pl.BlockSpec((1, n_seg, 2 * FB, HY_WIDTH), lambda b, f: (b, 0, f, 0)),
        scratch_shapes=[pltpu.VMEM((L, HY_WIDTH), BF16)],
        compiler_params=_cp("parallel", "arbitrary"),
        name="hy_fwd",
    )(z, fmat, spec)


def _inv_kernel(y_ref, g_ref, gate_ref, z_ref, bias_ref, o_ref):
    y = _dot(g_ref[...], y_ref[0, 0])
    o_ref[0] = gate_ref[0] * (y + bias_ref[0, 0] * z_ref[0])


def _hy_inv(l, L, seg, yf, gmat, u, row0, gate_colblk, z, z_row0, z_colblk, flt_bias, order):
    tmi = min(seg, 2 * TM)
    per_seg = seg // tmi
    r0, zr0 = row0 // tmi, z_row0 // tmi
    return pl.pallas_call(
        _inv_kernel,
        out_shape=jax.ShapeDtypeStruct((BATCH, L, HY_WIDTH), F32),
        grid=(BATCH, L // tmi),
        in_specs=[
            pl.BlockSpec((1, 1, 2 * seg, HY_WIDTH), lambda b, i: (b, i // per_seg, 0, 0)),
            pl.BlockSpec((tmi, 2 * seg), lambda b, i: (i % per_seg, 0)),
            pl.BlockSpec((1, tmi, HY_WIDTH), lambda b, i: (b, r0 + i, gate_colblk)),
            pl.BlockSpec((1, tmi, HY_WIDTH), lambda b, i: (b, zr0 + i, z_colblk)),
            pl.BlockSpec((1, 1, 1, HY_WIDTH), lambda b, i: (l, order, 0, 0)),
        ],
        out_specs=pl.BlockSpec((1, tmi, HY_WIDTH), lambda b, i: (b, i, 0)),
        compiler_params=_cp("parallel", "parallel"),
        name="hy_inv",
    )(yf, gmat, u, z, flt_bias)


def _route_tile(lg, bias):
    lane = lax.broadcasted_iota(jnp.int32, lg.shape, 1)
    lane_f = lane.astype(F32)
    neg = jnp.float32(-jnp.inf)
    big = jnp.float32(ROUTER_PAD)
    biased = lg + bias

    def first_argmax(v):
        m = jnp.max(v, axis=-1, keepdims=True)
        return jnp.min(jnp.where(v == m, lane_f, big), axis=-1, keepdims=True).astype(jnp.int32)

    def pick(v, idx):
        return jnp.sum(jnp.where(lane == idx, v, 0.0), axis=-1, keepdims=True)

    is_group = lane < N_GROUPS
    g_sel = first_argmax(jnp.where(is_group, biased, neg))
    raw_g = jnp.where(is_group, lg, neg)
    e_g = jnp.exp(raw_g - jnp.max(raw_g, axis=-1, keepdims=True))
    p_g = pick(e_g, g_sel) / jnp.sum(e_g, axis=-1, keepdims=True)

    lo = N_GROUPS + g_sel * EXP_PER_GROUP
    cand = jnp.where((lane >= lo) & (lane < lo + EXP_PER_GROUP), biased, neg)
    i1 = first_argmax(cand)
    i2 = first_argmax(jnp.where(lane == i1, neg, cand))
    l1, l2 = pick(lg, i1), pick(lg, i2)
    top = jnp.maximum(l1, l2)
    e1, e2 = jnp.exp(l1 - top), jnp.exp(l2 - top)
    inv = p_g / (e1 + e2)
    return (i1 - N_GROUPS, i2 - N_GROUPS), (e1 * inv, e2 * inv)


def _merge_kernel(h_ref, sg_ref, hyl_ref, hyc_ref, attl_ref, attc_ref, mod_ref, g_ref,
                  who_ref, wmo_ref, wout_ref, wr_ref, br_ref, h1_ref, u2_ref, eid_ref, ew_ref, cnt_ref):
    is_ctx = pl.program_id(1) >= N_LAT_TILES
    hy = jnp.where(is_ctx, hyc_ref[0], hyl_ref[0]).astype(BF16)
    att = jnp.where(is_ctx, attc_ref[0], attl_ref[0])
    sg = sg_ref[0].astype(F32)
    m = sg[:, :D_MODEL] * _dot(hy, who_ref[0]) + sg[:, D_MODEL:] * _dot(att, wmo_ref[0])
    y = _dot(m.astype(BF16), wout_ref[0])
    g1 = mod_ref[0, 0, 0, 2:3, :]
    sh2 = mod_ref[0, 0, 0, 3:4, :]
    sc2 = mod_ref[0, 0, 0, 4:5, :]
    h1 = h_ref[0] + g1 * y
    h1_ref[0] = h1
    u2 = _rms(h1, g_ref[0]) * (1.0 + sc2) + sh2
    u2_ref[0] = u2
    ids, ws = _route_tile(_dot3(u2, wr_ref[0]), br_ref[0])
    lane = lax.broadcasted_iota(jnp.int32, (TM, ROUTER_PAD), 1)
    hist = jnp.zeros((1, ROUTER_PAD), F32)
    for c in range(TOP_K):
        eid_ref[0, :, c:c + 1] = ids[c]
        ew_ref[0, :, c:c + 1] = ws[c]
        hist = hist + jnp.sum(jnp.where(lane == ids[c], 1.0, 0.0), axis=0, keepdims=True)
    cnt_ref[0, 0] = hist


def _merge(l, n_tiles, h, sg, hy_lat, hy_ctx, att_lat, att_ctx, mods, norm2_g, who, wmo, wout, wr, br):
    rows = n_tiles * TM
    tile = lambda w: pl.BlockSpec((1, TM, w), lambda b, i: (b, i, 0))
    lat = lambda w: pl.BlockSpec((1, TM, w), lambda b, i: (b, jnp.minimum(i, N_LAT_TILES - 1), 0))
    ctx = lambda w: pl.BlockSpec((1, TM, w), lambda b, i: (b, 0, 0))
    wfull = lambda a: pl.BlockSpec((1,) + a.shape[1:], lambda b, i: (l, 0, 0))
    return pl.pallas_call(
        _merge_kernel,
        out_shape=(
            jax.ShapeDtypeStruct((BATCH, rows, D_MODEL), F32),
            jax.ShapeDtypeStruct((BATCH, rows, D_MODEL), F32),
            jax.ShapeDtypeStruct((BATCH, rows, TOP_K), jnp.int32),
            jax.ShapeDtypeStruct((BATCH, rows, TOP_K), F32),
            jax.ShapeDtypeStruct((BATCH, n_tiles, 1, ROUTER_PAD), F32),
        ),
        grid=(BATCH, n_tiles),
        in_specs=[
            tile(D_MODEL), tile(2 * D_MODEL),
            lat(HY_WIDTH), ctx(HY_WIDTH), lat(MLA_HEADS * V_DIM), ctx(MLA_HEADS * V_DIM),
            pl.BlockSpec((1, 1, 1, N_MOD, D_MODEL), lambda b, i: (l, b, i // N_LAT_TILES, 0, 0)),
            pl.BlockSpec((1, 1, D_MODEL), lambda b, i: (l, 0, 0)),
            wfull(who), wfull(wmo), wfull(wout), wfull(wr), wfull(br),
        ],
        out_specs=(tile(D_MODEL), tile(D_MODEL), tile(TOP_K), tile(TOP_K),
                   pl.BlockSpec((1, 1, 1, ROUTER_PAD), lambda b, i: (b, i, 0, 0))),
        compiler_params=_cp("parallel", "parallel"),
        name="merge",
    )(h, sg, hy_lat, hy_ctx, att_lat, att_ctx, mods, norm2_g, who, wmo, wout, wr, br)


def _expert_kernel(be_ref, b0_ref, nr_ref, src_ref, dst_ref, u_hbm, wg_ref, wu_ref, wd_ref, y_hbm,
                   xbuf, ybuf, gsem, ssem):
    i = pl.program_id(0)
    nb = pl.num_programs(0)
    slot = i % 2

    def in_copy(blk, sl, g, j):
        tok = src_ref[b0_ref[blk] + g * SUBLANES + j]
        return pltpu.make_async_copy(u_hbm.at[pl.ds(tok, 1)], xbuf.at[sl, g, pl.ds(j, 1)], gsem.at[sl])

    def out_copy(blk, sl, g, j):
        r = g * SUBLANES + j
        return pltpu.make_async_copy(ybuf.at[sl, pl.ds(r, 1)], y_hbm.at[pl.ds(dst_ref[b0_ref[blk] + r], 1)],
                                     ssem.at[sl])

    def for_rows(blk, fn):
        n = nr_ref[blk]
        n_groups = n // SUBLANES

        def group(g, carry):
            for j in range(SUBLANES):
                fn(g, j)
            return carry

        def single(r, carry):
            fn(n_groups, r)
            return carry

        lax.fori_loop(0, n_groups, group, 0)
        lax.fori_loop(0, n - n_groups * SUBLANES, single, 0)

    @pl.when(i == 0)
    def _():
        xbuf[...] = jnp.zeros_like(xbuf)
        for_rows(0, lambda g, j: in_copy(0, 0, g, j).start())

    @pl.when(i + 1 < nb)
    def _():
        for_rows(i + 1, lambda g, j: in_copy(i + 1, 1 - slot, g, j).start())

    for_rows(i, lambda g, j: in_copy(i, slot, g, j).wait())

    @pl.when(i >= 2)
    def _():
        for_rows(i - 2, lambda g, j: out_copy(i - 2, slot, g, j).wait())

    @pl.when(nr_ref[i] > 0)
    def _():
        x = xbuf[slot].reshape(MOE_BM, D_MODEL).astype(BF16)
        hg = _dot(x, wg_ref[0, 0].astype(BF16))
        hu = _dot(x, wu_ref[0, 0].astype(BF16))
        hb = (hg * jax.nn.sigmoid(hg) * hu).astype(BF16)
        ybuf[slot] = _dot(hb, wd_ref[0, 0].astype(BF16))

    for_rows(i, lambda g, j: out_copy(i, slot, g, j).start(priority=j % 2 if isinstance(j, int) else 0))

    @pl.when(i == nb - 1)
    def _():
        for_rows(i - 1, lambda g, j: out_copy(i - 1, 1 - slot, g, j).wait())
        for_rows(i, lambda g, j: out_copy(i, slot, g, j).wait())


def _experts(l, block_expert, block_row0, block_rows, order, u, w_gate_e, w_up_e, w_down_e):
    n_blocks = block_expert.shape[0]
    n_tok = u.shape[0]
    src = lax.shift_right_logical(order, 1)
    dst = (order & 1) * n_tok + src
    wspec = lambda r, c: pl.BlockSpec((1, 1, r, c), lambda i, be, b0, nr, sr, ds: (l, be[i], 0, 0))
    grid_spec = pltpu.PrefetchScalarGridSpec(
        num_scalar_prefetch=5,
        grid=(n_blocks,),
        in_specs=[
            pl.BlockSpec(memory_space=pl.ANY),
            wspec(D_MODEL, D_EXPERT), wspec(D_MODEL, D_EXPERT), wspec(D_EXPERT, D_MODEL),
        ],
        out_specs=pl.BlockSpec(memory_space=pl.ANY),
        scratch_shapes=[
            pltpu.VMEM((2, MOE_BM // SUBLANES, SUBLANES, D_MODEL), F32),
            pltpu.VMEM((2, MOE_BM, D_MODEL), F32),
            pltpu.SemaphoreType.DMA((2,)),
            pltpu.SemaphoreType.DMA((2,)),
        ],
    )
    return pl.pallas_call(
        _expert_kernel,
        out_shape=jax.ShapeDtypeStruct((TOP_K * u.shape[0], D_MODEL), F32),
        grid_spec=grid_spec,
        compiler_params=_cp("arbitrary"),
        name="experts",
    )(block_expert, block_row0, block_rows, src, dst, u, w_gate_e, w_up_e, w_down_e)


def _dispatch(expert, counts):
    A = expert.shape[0]
    _, order = lax.sort((expert, jnp.arange(A, dtype=jnp.int32)), num_keys=1)
    start = jnp.cumsum(counts) - counts
    n_blk = (counts + MOE_BM - 1) // MOE_BM
    blk_end = jnp.cumsum(n_blk)
    blk_start = blk_end - n_blk
    n_blocks = -(-A // MOE_BM) + N_EXPERTS
    b = jnp.arange(n_blocks, dtype=jnp.int32)[:, None]
    mine = (b >= blk_start[None, :]) & (b < blk_end[None, :])
    used = jnp.any(mine, axis=1)
    pick = lambda table: jnp.sum(jnp.where(mine, table[None, :], 0), axis=1)
    row_in_expert = (b[:, 0] - pick(blk_start)) * MOE_BM
    block_expert = jnp.where(used, pick(jnp.arange(N_EXPERTS, dtype=jnp.int32)), N_EXPERTS - 1)
    block_rows = jnp.where(used, jnp.clip(pick(counts) - row_in_expert, 0, MOE_BM), 0)
    block_row0 = jnp.where(used, pick(start) + row_in_expert, 0)
    return order, block_expert.astype(jnp.int32), block_row0.astype(jnp.int32), block_rows.astype(jnp.int32)


def _moe_sum(y_ref, w_ref):
    w = w_ref[0]
    return sum(w[:, c:c + 1] * y_ref[c, 0] for c in range(TOP_K))


def _combine_kernel(h_ref, y_ref, w_ref, mod_ref, o_ref):
    o_ref[0] = h_ref[0] + mod_ref[0, 0, 0, 5:6, :] * _moe_sum(y_ref, w_ref)


def _final_kernel(h_ref, y_ref, w_ref, mod_ref, g_ref, o_ref):
    h2 = h_ref[0] + mod_ref[0, 0, 0, 5:6, :] * _moe_sum(y_ref, w_ref)
    o_ref[0] = _rms(h2, g_ref[...])


def _combine(l, n_tiles, h1, y, w, mods, final_g=None):
    rows = n_tiles * TM
    tile = pl.BlockSpec((1, TM, D_MODEL), lambda b, i: (b, i, 0))
    in_specs = [tile, pl.BlockSpec((TOP_K, 1, TM, D_MODEL), lambda b, i: (0, b, i, 0)),
                pl.BlockSpec((1, TM, TOP_K), lambda b, i: (b, i, 0)),
                pl.BlockSpec((1, 1, 1, N_MOD, D_MODEL), lambda b, i: (l, b, i // N_LAT_TILES, 0, 0))]
    args = [h1, y, w, mods]
    kern = _combine_kernel
    if final_g is not None:
        in_specs.append(pl.BlockSpec((1, D_MODEL), lambda b, i: (0, 0)))
        args.append(final_g.reshape(1, D_MODEL))
        kern = _final_kernel
    return pl.pallas_call(
        kern,
        out_shape=jax.ShapeDtypeStruct((BATCH, rows, D_MODEL), F32),
        grid=(BATCH, n_tiles),
        in_specs=in_specs,
        out_specs=tile,
        compiler_params=_cp("parallel", "parallel"),
        name="combine",
    )(*args)


def _rope_tables():
    n = jnp.arange(SEQ)
    pos = jnp.stack([n // GRID_W, n % GRID_W], axis=-1).astype(F32)
    inv = ROPE_BASE ** (-jnp.arange(ROPE_PAIRS, dtype=F32) / ROPE_PAIRS)
    ang = pos[:, :, None] * inv
    cos, sin = jnp.cos(ang), jnp.sin(ang)
    cos32 = jnp.stack([cos, cos], axis=2).reshape(SEQ, QK_ROPE)
    sin32 = jnp.stack([-sin, sin], axis=2).reshape(SEQ, QK_ROPE)
    one = jnp.ones((SEQ, QK_ROPE), F32)
    zero = jnp.zeros((SEQ, QK_ROPE), F32)
    cosq = jnp.concatenate([jnp.ones((SEQ, QK_NOPE), F32), cos32, one], axis=1) * Q_SCALE
    sinq = jnp.concatenate([jnp.zeros((SEQ, QK_NOPE), F32), sin32, zero], axis=1) * Q_SCALE
    cosk = jnp.concatenate([jnp.zeros((SEQ, QK_NOPE), F32), cos32, zero], axis=1)
    sink = jnp.concatenate([jnp.zeros((SEQ, QK_NOPE), F32), sin32, zero], axis=1)
    lat = jnp.stack([cosq, sinq, cosk, sink])
    ctx_row = jnp.ones((HEAD_PAD,), F32) * Q_SCALE
    ctx_cosk = jnp.concatenate([jnp.zeros((QK_NOPE + QK_ROPE,), F32), jnp.ones((QK_ROPE,), F32)])
    ctx = jnp.stack([ctx_row, jnp.zeros_like(ctx_row), ctx_cosk, jnp.zeros_like(ctx_row)])
    ctx = jnp.broadcast_to(ctx[:, None, :], (4, CTX_LEN, HEAD_PAD))
    return jnp.concatenate([lat, ctx], axis=1)


def _dft_matrices(L):
    N = 2 * L
    nfb = L // FB
    t = jnp.arange(L, dtype=jnp.int32)

    def tables(n_rows, step):
        ang = ((jnp.arange(n_rows, dtype=jnp.int32)[:, None] * step * t[None, :]) % N).astype(F32)
        ang = ang * (2.0 * math.pi / N)
        return jnp.cos(ang), jnp.sin(ang)

    ca, sa = tables(nfb, FB)
    cb, sb = tables(FB, 1)
    alt = jnp.where((t % 2) == 0, 1.0, -1.0).astype(F32)
    i_idx = jnp.arange(nfb)[:, None, None]
    r_idx = jnp.arange(FB)[None, :, None]
    dc = (i_idx == 0) & (r_idx == 0)
    c = ca[:, None, :] * cb[None] - sa[:, None, :] * sb[None]
    s = jnp.where(dc, alt, -(sa[:, None, :] * cb[None] + ca[:, None, :] * sb[None]))
    f = jnp.stack([c, s], axis=1).astype(BF16)
    wk = jnp.where(dc, 1.0 / N, 2.0 / N).astype(F32)[..., 0]
    cat, sat, cbt, sbt = ca.T[:, :, None], sa.T[:, :, None], cb.T[:, None, :], sb.T[:, None, :]
    ct = (cat * cbt - sat * sbt) * wk
    st = jnp.where(dc[..., 0], alt[:, None, None], -(sat * cbt + cat * sbt)) * wk
    g = jnp.stack([ct, st], axis=2).reshape(L, 2 * L).astype(BF16)
    return f, g


def _filter_embedding(L):
    t = jnp.linspace(0.0, 1.0, L, dtype=F32)[:, None]
    w = 2.0 * math.pi * jnp.arange(L, dtype=F32)[:, None] / L
    bands = jnp.linspace(1e-4, HY_BANDS - 1, HY_BANDS, dtype=F32)[None, :]
    return jnp.concatenate([t, jnp.cos(bands * w), -jnp.sin(bands * w),
                            jnp.zeros((L, LANES - HY_EMB), F32)], axis=-1)


def _rope_swap_perm():
    idx = np.arange(QK_ROPE)
    axis, half, pair = idx // (2 * ROPE_PAIRS), (idx // ROPE_PAIRS) % 2, idx % ROPE_PAIRS
    return axis * 2 * ROPE_PAIRS + (1 - half) * ROPE_PAIRS + pair


def _prep_weights(w_in, w_uq, w_ukv, w_hy_o, w_mla_o, w_out, w_router_g, w_router_e):
    perm = _rope_swap_perm()
    c_q = HY_COLS
    c_kv = HY_COLS + Q_LORA
    c_kr = c_kv + KV_LORA
    w_kr = w_in[:, :, c_kr:MLA_END]
    zeros = lambda n: jnp.zeros((DEPTH, D_MODEL, n), F32)
    wkr = jnp.concatenate([zeros(QK_NOPE), w_kr, w_kr,
                           zeros(QK_NOPE), w_kr[:, :, perm], zeros(QK_ROPE)], axis=-1)
    uq = w_uq.reshape(DEPTH, Q_LORA, MLA_HEADS, QK_NOPE + QK_ROPE)
    uq_r = uq[..., QK_NOPE:]
    wuq = jnp.concatenate([uq, uq_r], axis=-1).reshape(DEPTH, Q_LORA, MLA_HEADS * HEAD_PAD)
    wuqs = jnp.concatenate([jnp.zeros_like(uq[..., :QK_NOPE]), uq_r[..., perm], jnp.zeros_like(uq_r)],
                           axis=-1).reshape(DEPTH, Q_LORA, MLA_HEADS * HEAD_PAD)
    ukv = w_ukv.reshape(DEPTH, KV_LORA, MLA_HEADS, QK_NOPE + V_DIM)
    zpad = jnp.zeros_like(ukv[..., :HEAD_PAD - QK_NOPE])
    wuk = jnp.concatenate([ukv[..., :QK_NOPE], zpad], axis=-1).reshape(DEPTH, KV_LORA, MLA_HEADS * HEAD_PAD)
    wuv = jnp.concatenate([ukv[..., QK_NOPE:], zpad], axis=-1).reshape(DEPTH, KV_LORA, MLA_HEADS * HEAD_PAD)
    wr = jnp.concatenate([w_router_g, w_router_e,
                          jnp.zeros((DEPTH, D_MODEL, ROUTER_PAD - N_GROUPS - N_EXPERTS), F32)], axis=-1)
    bf = lambda a: a.astype(BF16)
    return dict(
        why=bf(w_in[:, :, :HY_COLS]), wgt=bf(w_in[:, :, MLA_END:]),
        wq=bf(w_in[:, :, c_q:c_kv]), wkv=bf(w_in[:, :, c_kv:c_kr]), wkr=bf(wkr),
        wuq=bf(wuq), wuqs=bf(wuqs), wuk=bf(wuk), wuv=bf(wuv),
        who=bf(w_hy_o), wmo=bf(w_mla_o), wout=bf(w_out), wr=wr,
    )


def _hyena(l, L, row0, u, fmat, gmat, zemb, deltas, flt, flt_bias):
    seg = fmat.shape[-1]
    spec = _spectrum(L, seg, fmat, _filters(l, L, zemb, deltas, flt))
    rb = row0 // L
    yf = _hy_fwd(L, seg, u, rb, 0, fmat, spec, 0)
    z1 = _hy_inv(l, L, seg, yf, gmat, u, row0, 1, u, row0, 0, flt_bias, 0)
    yf = _hy_fwd(L, seg, z1, 0, 0, fmat, spec, 1)
    return _hy_inv(l, L, seg, yf, gmat, u, row0, 2, z1, 0, 0, flt_bias, 1)


def kernel(x, c, ctx, c_ctx, w_ada, b_ada, norm1_g, w_in, q_norm_g, kv_norm_g, w_uq, w_ukv,
           sc_w, sc_b, flt_w1, flt_b1, flt_freq, flt_w2, flt_b2, flt_w3, flt_bias,
           w_hy_o, w_mla_o, w_out, norm2_g, w_router_g, b_router_g, w_router_e, b_router_e,
           w_gate_e, w_up_e, w_down_e, final_g):
    wts = _prep_weights(w_in, w_uq, w_ukv, w_hy_o, w_mla_o, w_out, w_router_g, w_router_e)
    tabs = _rope_tables()
    f_lat, g_lat = _dft_matrices(min(SEQ, HY_SEG))
    f_ctx, g_ctx = _dft_matrices(min(CTX_LEN, HY_SEG))
    z_lat, z_ctx = _filter_embedding(SEQ), _filter_embedding(CTX_LEN)
    deltas = jnp.linspace(math.log(HY_TARGET) / HY_FAST_DECAY, math.log(HY_TARGET) / HY_SLOW_DECAY,
                          HY_WIDTH, dtype=F32)[None, :]

    cc = jnp.concatenate([c, c_ctx[None, :], jnp.zeros((3, D_MODEL), F32)], axis=0)
    mod = _ada(cc, w_ada, b_ada).reshape(DEPTH, 8, N_MOD, D_MODEL)
    mods = jnp.stack([mod[:, :BATCH], jnp.broadcast_to(mod[:, BATCH:BATCH + 1], (DEPTH, BATCH, N_MOD, D_MODEL))],
                     axis=2)

    r3 = lambda a: a.reshape(DEPTH, 1, a.shape[-1])
    norm1, norm2, qg, kvg = r3(norm1_g), r3(norm2_g), r3(q_norm_g), r3(kv_norm_g)
    scb = r3(sc_b)
    fb1, ffr, fb2 = r3(flt_b1), r3(flt_freq), r3(flt_b2)
    fbias = flt_bias.reshape(DEPTH, HY_ORDER, 1, HY_WIDTH)
    w1_pad = jnp.concatenate([flt_w1, jnp.zeros((DEPTH, LANES - HY_EMB, HY_FFN), F32)], axis=1)
    flt = (w1_pad, fb1, ffr, flt_w2, fb2, flt_w3)

    b_router = jnp.concatenate([b_router_g, b_router_e,
                                jnp.zeros((DEPTH, ROUTER_PAD - N_GROUPS - N_EXPERTS), F32)], axis=-1)
    b_router = b_router.reshape(DEPTH, 1, ROUTER_PAD)

    h = jnp.concatenate([x, ctx], axis=1)
    for l in range(DEPTH):
        last = l == DEPTH - 1
        n_tiles = N_LAT_TILES if last else N_ALL_TILES
        u, sg, q, k, v = _inproj(l, h, mods, norm1, tabs, qg, kvg, sc_w, scb, wts)
        att_lat = _attention(q, k, v, latent=True)
        hy_lat = _hyena(l, SEQ, 0, u, f_lat, g_lat, z_lat, deltas, flt, fbias)
        if last:
            att_ctx, hy_ctx = att_lat, hy_lat
        else:
            att_ctx = _attention(q, k, v, latent=False)
            hy_ctx = _hyena(l, CTX_LEN, SEQ, u, f_ctx, g_ctx, z_ctx, deltas, flt, fbias)

        h1, u2, eid, ew, hist = _merge(l, n_tiles, h, sg, hy_lat, hy_ctx, att_lat, att_ctx, mods, norm2,
                                       wts["who"], wts["wmo"], wts["wout"], wts["wr"], b_router)
        counts = jnp.sum(hist, axis=(0, 1, 2))[:N_EXPERTS].astype(jnp.int32)

        T = BATCH * n_tiles * TM
        order, block_expert, block_row0, block_rows = _dispatch(eid.reshape(T * TOP_K), counts)
        y = _experts(l, block_expert, block_row0, block_rows, order, u2.reshape(T, D_MODEL),
                     w_gate_e, w_up_e, w_down_e)
        h = _combine(l, n_tiles, h1, y.reshape(TOP_K, BATCH, n_tiles * TM, D_MODEL), ew, mods,
                     final_g if last else None)
    return h
```
